```python
import math
import jax, jax.numpy as jnp
from jax import lax
import numpy as np

D_MODEL = 1024
BATCH = 32
SEQ = 256
DEPTH = 2
DEC_BATCH = 2
DEC_SEQ = 2048
PAST_LEN = 256

GRID_W = 64
N_HEADS = 16
HEAD_DIM = D_MODEL // N_HEADS
WIN_ROWS = 8
WIN_COLS = 16
Q_COLS = 16
K_COLS = 32
CTX_Q_BLOCK = 128
D_FF = 2816
N_MIXERS = 2
N_ATTN_LAYERS = (DEPTH + 1) // 2
N_HYENA_LAYERS = DEPTH // 2
N_MOD = 9
HY_ORDER = 2
HY_DIRS = 2
HY_SHORT = 3
HY_BANDS = 16
HY_EMB = 1 + 2 * HY_BANDS
HY_FILTER_W = 64
HY_FAST_DECAY = 0.3
HY_SLOW_DECAY = 1.5
HY_TARGET = 1e-2
ALPHA = (2 * DEPTH) ** 0.25
BETA = (8 * DEPTH) ** -0.25
LN_EPS = 1e-5

kernel_name = "hybrid_natten_hyena_diffusion_step"


def layer_norm(x, g, b):
    xf = x.astype(jnp.float32)
    mu = xf.mean(-1, keepdims=True)
    var = jnp.square(xf - mu).mean(-1, keepdims=True)
    return ((xf - mu) * lax.rsqrt(var + LN_EPS) * g + b).astype(x.dtype)


def modulate(x, shift, scale):
    return x * (1.0 + scale) + shift


def swiglu(h, w_in, w_out):
    g, u = jnp.split(h @ w_in, 2, axis=-1)
    return (jax.nn.silu(g) * u) @ w_out


def attention_context(h, w_qkv, w_o):
    B, S, _ = h.shape
    q, k, v = [t.reshape(B, S, N_HEADS, HEAD_DIM) for t in jnp.split(h @ w_qkv, 3, axis=-1)]
    qb = jnp.moveaxis((q * HEAD_DIM ** -0.5).reshape(B, S // CTX_Q_BLOCK, CTX_Q_BLOCK, N_HEADS, HEAD_DIM), 1, 0)

    def block(q_blk):
        s = jnp.einsum('bqhd,bkhd->bhqk', q_blk, k).astype(jnp.float32)
        p = jax.nn.softmax(s, axis=-1).astype(v.dtype)
        return jnp.einsum('bhqk,bkhd->bqhd', p, v)

    o = jnp.moveaxis(lax.map(block, qb), 0, 1).reshape(B, S, D_MODEL)
    return o @ w_o, k, v


def attention_latent(h, k_ctx, v_ctx, w_qkv, w_o, rpb):
    B, L, _ = h.shape
    rows = L // GRID_W
    kr = min(WIN_ROWS, rows)
    n_cb = GRID_W // Q_COLS
    q, k, v = [t.reshape(B, rows, GRID_W, N_HEADS, HEAD_DIM) for t in jnp.split(h @ w_qkv, 3, axis=-1)]
    q = q * HEAD_DIM ** -0.5
    qcol = np.arange(GRID_W).reshape(n_cb, Q_COLS)
    cstart = np.clip(qcol - WIN_COLS // 2, 0, GRID_W - WIN_COLS)
    kstart = np.clip(cstart[:, 0], 0, GRID_W - K_COLS)
    kcol = kstart[:, None] + np.arange(K_COLS)
    col_ok = (kcol[:, None, :] >= cstart[:, :, None]) & (kcol[:, None, :] < cstart[:, :, None] + WIN_COLS)
    dcol = np.clip(kcol[:, None, :] - qcol[:, :, None], 1 - WIN_COLS, WIN_COLS - 1) + WIN_COLS - 1
    bias_col = rpb[:, :, dcol]
    mask = jnp.asarray(col_ok)[None, None, :, :, None, :]
    n_lat = kr * K_COLS

    def one_row(args):
        r, q_r = args
        rs = jnp.clip(r - kr // 2, 0, rows - kr)
        k_blk = lax.dynamic_slice_in_dim(k, rs, kr, axis=1)[:, :, kcol]
        v_blk = lax.dynamic_slice_in_dim(v, rs, kr, axis=1)[:, :, kcol]
        drow = rs + jnp.arange(kr) - r + WIN_ROWS - 1
        bias = bias_col[:, drow].transpose(0, 2, 3, 1, 4)
        q_b = q_r.reshape(B, n_cb, Q_COLS, N_HEADS, HEAD_DIM)
        s_lat = jnp.einsum('bnqhd,brnkhd->bhnqrk', q_b, k_blk).astype(jnp.float32) + bias
        s_lat = jnp.where(mask, s_lat, -jnp.inf).reshape(B, N_HEADS, n_cb, Q_COLS, n_lat)
        s_ctx = jnp.einsum('bnqhd,bshd->bhnqs', q_b, k_ctx).astype(jnp.float32)
        p = jax.nn.softmax(jnp.concatenate([s_lat, s_ctx], axis=-1), axis=-1).astype(v.dtype)
        p_lat = p[..., :n_lat].reshape(B, N_HEADS, n_cb, Q_COLS, kr, K_COLS)
        o = (jnp.einsum('bhnqrk,brnkhd->bnqhd', p_lat, v_blk)
             + jnp.einsum('bhnqs,bshd->bnqhd', p[..., n_lat:], v_ctx))
        return o.reshape(B, GRID_W, D_MODEL)

    o = lax.map(one_row, (jnp.arange(rows), jnp.moveaxis(q, 1, 0)))
    return jnp.moveaxis(o, 0, 1).reshape(B, L, D_MODEL) @ w_o


def short_conv(u, w, b):
    up = jnp.pad(u, ((0, 0), (1, 1), (0, 0)))
    return up[:, :-2] * w[0] + up[:, 1:-1] * w[1] + up[:, 2:] * w[2] + b


def hyena_filters(L, f_w1, f_b1, f_freq1, f_w2, f_b2, f_freq2, f_w3, f_b3):
    pos = jnp.arange(L, dtype=jnp.float32)[:, None]
    t = pos / max(L - 1, 1)
    bands = jnp.linspace(1e-4, HY_BANDS - 1, HY_BANDS, dtype=jnp.float32)
    ang = 2.0 * math.pi * pos / L * bands
    z = jnp.concatenate([t, jnp.cos(ang), -jnp.sin(ang)], axis=-1)
    hid = jnp.sin(f_freq1 * (z @ f_w1 + f_b1))
    hid = jnp.sin(f_freq2 * (hid @ f_w2 + f_b2))
    filt = (hid @ f_w3 + f_b3).reshape(L, HY_ORDER, HY_DIRS, D_MODEL).astype(jnp.float32)
    deltas = jnp.abs(jnp.linspace(math.log(HY_TARGET) / HY_SLOW_DECAY, math.log(HY_TARGET) / HY_FAST_DECAY,
                                  D_MODEL, dtype=jnp.float32))
    return filt * jnp.exp(-t.reshape(L, 1, 1, 1) * deltas)


def long_conv(z, h_fwd, h_bwd, bias):
    L = z.shape[1]
    k = jnp.concatenate([h_fwd, jnp.zeros_like(h_fwd[:1]), h_bwd[:0:-1]], axis=0)
    zf = z.astype(jnp.float32)
    y = jnp.fft.irfft(jnp.fft.rfft(zf, n=2 * L, axis=1) * jnp.fft.rfft(k, n=2 * L, axis=0)[None],
                      n=2 * L, axis=1)[:, :L]
    return (y + zf * bias).astype(z.dtype)


def hyena(h, w_in, conv_w, conv_b, f_w1, f_b1, f_freq1, f_w2, f_b2, f_freq2, f_w3, f_b3, bias, w_out):
    L = h.shape[1]
    v, x1, x2 = jnp.split(short_conv(h @ w_in, conv_w, conv_b), 3, axis=-1)
    filt = hyena_filters(L, f_w1, f_b1, f_freq1, f_w2, f_b2, f_freq2, f_w3, f_b3)
    z = long_conv(v, filt[:, 0, 0], filt[:, 0, 1], bias[0]) * x1
    z = long_conv(z, filt[:, 1, 0], filt[:, 1, 1], bias[1]) * x2
    return z @ w_out


def trunk_layer(x, cond, mod_w_i, mod_b_i, ln_g_i, ln_b_i, ffn_w_in_i, ffn_w_out_i, mixer):
    sh1, sc1, g1, sh2, sc2, g2, sh3, sc3, g3 = jnp.split(jax.nn.silu(cond) @ mod_w_i + mod_b_i, N_MOD, axis=-1)
    x = layer_norm(ALPHA * x + 0.5 * g1 * swiglu(modulate(x, sh1, sc1), ffn_w_in_i[0], ffn_w_out_i[0]),
                   ln_g_i[0], ln_b_i[0])
    mix, extras = mixer(modulate(x, sh2, sc2))
    x = layer_norm(ALPHA * x + g2 * mix, ln_g_i[1], ln_b_i[1])
    x = layer_norm(ALPHA * x + 0.5 * g3 * swiglu(modulate(x, sh3, sc3), ffn_w_in_i[1], ffn_w_out_i[1]),
                   ln_g_i[2], ln_b_i[2])
    return x, extras


def setup_inputs(seed: int = 0) -> dict:
    key = jax.random.key(seed)
    ks = jax.random.split(key, 28)
    D = D_MODEL
    hd = HY_ORDER * HY_DIRS * D_MODEL

    def nrm(k, shape, s):
        return jax.random.normal(k, shape, jnp.float32) * s

    return {
        "x_prompt": nrm(ks[0], (BATCH, SEQ, D), 1.0),
        "x_sample": nrm(ks[1], (DEC_BATCH, DEC_SEQ, D), 1.0),
        "cache_k": nrm(ks[2], (DEC_BATCH, N_ATTN_LAYERS, PAST_LEN, N_HEADS, HEAD_DIM), 1.0),
        "cache_v": nrm(ks[3], (DEC_BATCH, N_ATTN_LAYERS, PAST_LEN, N_HEADS, HEAD_DIM), 1.0),
        "c": nrm(ks[4], (DEC_BATCH, D), 1.0),
        "c_ctx": nrm(ks[5], (D,), 1.0),
        "mod_w": nrm(ks[6], (DEPTH, D, N_MOD * D), 0.5 * D ** -0.5),
        "mod_b": nrm(ks[7], (DEPTH, N_MOD * D), 0.01),
        "ln_g": 1.0 + nrm(ks[8], (DEPTH, 3, D), 0.05),
        "ln_b": nrm(ks[9], (DEPTH, 3, D), 0.02),
        "ffn_w_in": nrm(ks[10], (DEPTH, 2, D, 2 * D_FF), D ** -0.5),
        "ffn_w_out": nrm(ks[11], (DEPTH, 2, D_FF, D), BETA * D_FF ** -0.5),
        "attn_w_qkv": nrm(ks[12], (N_ATTN_LAYERS, D, 3 * D), D ** -0.5),
        "attn_w_o": nrm(ks[13], (N_ATTN_LAYERS, D, D), BETA * D ** -0.5),
        "attn_rpb": nrm(ks[14], (N_ATTN_LAYERS, N_HEADS, 2 * WIN_ROWS - 1, 2 * WIN_COLS - 1), 0.1),
        "hy_w_in": nrm(ks[15], (N_HYENA_LAYERS, D, 3 * D), D ** -0.5),
        "hy_conv_w": nrm(ks[16], (N_HYENA_LAYERS, HY_SHORT, 3 * D), HY_SHORT ** -0.5),
        "hy_conv_b": nrm(ks[17], (N_HYENA_LAYERS, 3 * D), 0.02),
        "hy_f_w1": nrm(ks[18], (N_HYENA_LAYERS, HY_EMB, HY_FILTER_W), HY_EMB ** -0.5),
        "hy_f_b1": nrm(ks[19], (N_HYENA_LAYERS, HY_FILTER_W), 0.1),
        "hy_f_freq1": 1.0 + nrm(ks[20], (N_HYENA_LAYERS, HY_FILTER_W), 0.05),
        "hy_f_w2": nrm(ks[21], (N_HYENA_LAYERS, HY_FILTER_W, HY_FILTER_W), HY_FILTER_W ** -0.5),
        "hy_f_b2": nrm(ks[22], (N_HYENA_LAYERS, HY_FILTER_W), 0.1),
        "hy_f_freq2": 1.0 + nrm(ks[23], (N_HYENA_LAYERS, HY_FILTER_W), 0.05),
        "hy_f_w3": nrm(ks[24], (N_HYENA_LAYERS, HY_FILTER_W, hd), 0.1 * HY_FILTER_W ** -0.5),
        "hy_f_b3": nrm(ks[25], (N_HYENA_LAYERS, hd), 0.01),
        "hy_bias": nrm(ks[26], (N_HYENA_LAYERS, HY_ORDER, D), 0.1),
        "hy_w_out": nrm(ks[27], (N_HYENA_LAYERS, D, D), BETA * D ** -0.5),
    }


def reference(x_prompt, x_sample, cache_k, cache_v, c, c_ctx, mod_w, mod_b, ln_g, ln_b, ffn_w_in, ffn_w_out,
              attn_w_qkv, attn_w_o, attn_rpb, hy_w_in, hy_conv_w, hy_conv_b, hy_f_w1, hy_f_b1, hy_f_freq1,
              hy_f_w2, hy_f_b2, hy_f_freq2, hy_f_w3, hy_f_b3, hy_bias, hy_w_out):
    hy_params = (hy_w_in, hy_conv_w, hy_conv_b, hy_f_w1, hy_f_b1, hy_f_freq1, hy_f_w2, hy_f_b2, hy_f_freq2,
                 hy_f_w3, hy_f_b3, hy_bias, hy_w_out)

    cond_ctx = c_ctx[None, None, :]
    y_prompt = x_prompt
    k_new, v_new = [], []
    for i in range(DEPTH):
        j = i // N_MIXERS
        if i % N_MIXERS == 0:
            def mixer(h, j=j):
                out, k_c, v_c = attention_context(h, attn_w_qkv[j], attn_w_o[j])
                return out, (k_c, v_c)
        else:
            def mixer(h, j=j):
                return hyena(h, *[p[j] for p in hy_params]), ()
        y_prompt, extras = trunk_layer(y_prompt, cond_ctx, mod_w[i], mod_b[i], ln_g[i], ln_b[i],
                                       ffn_w_in[i], ffn_w_out[i], mixer)
        if i % N_MIXERS == 0:
            k_new.append(extras[0])
            v_new.append(extras[1])
    new_cache_k = jnp.stack(k_new, axis=1)
    new_cache_v = jnp.stack(v_new, axis=1)

    cond_lat = c[:, None, :]
    y_sample = x_sample
    for i in range(DEPTH):
        j = i // N_MIXERS
        if i % N_MIXERS == 0:
            def mixer(h, j=j):
                return attention_latent(h, cache_k[:, j], cache_v[:, j], attn_w_qkv[j], attn_w_o[j],
                                        attn_rpb[j]), ()
        else:
            def mixer(h, j=j):
                return hyena(h, *[p[j] for p in hy_params]), ()
        y_sample, _ = trunk_layer(y_sample, cond_lat, mod_w[i], mod_b[i], ln_g[i], ln_b[i],
                                  ffn_w_in[i], ffn_w_out[i], mixer)

    return (y_prompt, y_sample, new_cache_k, new_cache_v)
```

```python
import functools
import math

import numpy as np
import jax
import jax.numpy as jnp
from jax import lax
from jax.experimental import pallas as pl
from jax.experimental.pallas import tpu as pltpu

D_MODEL = 1024
DEPTH = 2
GRID_W = 64
N_HEADS = 16
HEAD_DIM = D_MODEL // N_HEADS
WIN_ROWS = 8
WIN_COLS = 16
D_FF = 2816
N_MOD = 9
HY_BANDS = 16
HY_EMB = 1 + 2 * HY_BANDS
HY_FILTER_W = 64
HY_FAST_DECAY = 0.3
HY_SLOW_DECAY = 1.5
HY_TARGET = 1e-2
ALPHA = (2 * DEPTH) ** 0.25
LN_EPS = 1e-5

F32 = jnp.float32
BF16 = jnp.bfloat16
NEG = -1e30
LANES = 128
VMEM_LIMIT = 56 * 1024 * 1024
COND_ROWS = 8

Q_ROWS = 8
K_ROWS = 16
BAND_ROWS = 10


def _cparams(*sem):
    return pltpu.CompilerParams(dimension_semantics=sem, vmem_limit_bytes=VMEM_LIMIT)


def _dot(a, b):
    return jnp.dot(a, b, preferred_element_type=F32)


def _dot_nt(a, b):
    return lax.dot_general(a, b, (((1,), (1,)), ((), ())), preferred_element_type=F32)


def _layer_norm(y, g, b):
    mu = jnp.mean(y, axis=-1, keepdims=True)
    yc = y - mu
    var = jnp.mean(yc * yc, axis=-1, keepdims=True)
    return yc * lax.rsqrt(var + LN_EPS) * g + b


def _mods_kernel(c_ref, w_ref, b_ref, o_ref):
    c = c_ref[...]
    s = c * jax.nn.sigmoid(c)
    o_ref[...] = _dot(s.astype(BF16), w_ref[...].astype(BF16)) + b_ref[...]


def _mods(conds, mod_w, mod_b):
    tn = 1024
    nd = N_MOD * D_MODEL
    return pl.pallas_call(
        _mods_kernel,
        grid=(DEPTH, nd // tn),
        in_specs=[
            pl.BlockSpec((COND_ROWS, D_MODEL), lambda l, j: (0, 0)),
            pl.BlockSpec((None, D_MODEL, tn), lambda l, j: (l, 0, j)),
            pl.BlockSpec((None, 1, tn), lambda l, j: (l, 0, j)),
        ],
        out_specs=pl.BlockSpec((None, COND_ROWS, tn), lambda l, j: (l, 0, j)),
        out_shape=jax.ShapeDtypeStruct((DEPTH, COND_ROWS, nd), F32),
        compiler_params=_cparams("arbitrary", "arbitrary"),
        name="mods",
    )(conds, mod_w, mod_b.reshape(DEPTH, 1, nd))


def _ffn_kernel(x_ref, mod_ref, wg_ref, wu_ref, wo_ref, g_ref, b_ref, o_ref, h_ref, acc_ref, *, mi, nk):
    k = pl.program_id(1)

    @pl.when(k == 0)
    def _():
        sh = mod_ref[3 * mi:3 * mi + 1, :]
        sc = mod_ref[3 * mi + 1:3 * mi + 2, :]
        h_ref[...] = (x_ref[...] * (1.0 + sc) + sh).astype(BF16)
        acc_ref[...] = jnp.zeros_like(acc_ref)

    h = h_ref[...]
    gate = _dot(h, wg_ref[...].astype(BF16))
    up = _dot(h, wu_ref[...].astype(BF16))
    a = (gate * jax.nn.sigmoid(gate) * up).astype(BF16)
    acc_ref[...] += _dot(a, wo_ref[...].astype(BF16))

    @pl.when(k == nk - 1)
    def _():
        gt = mod_ref[3 * mi + 2:3 * mi + 3, :]
        y = ALPHA * x_ref[...] + (0.5 * gt) * acc_ref[...]
        o_ref[...] = _layer_norm(y, g_ref[...], b_ref[...])


def _ffn(x, mods, crow, w_in, w_out, ln_g, ln_b, layer, which, tm=1024, tf=256):
    m = x.shape[0]
    nk = D_FF // tf
    mi = 0 if which == 0 else 2
    return pl.pallas_call(
        functools.partial(_ffn_kernel, mi=mi, nk=nk),
        grid=(m // tm, nk),
        in_specs=[
            pl.BlockSpec((tm, D_MODEL), lambda i, k: (i, 0)),
            pl.BlockSpec((None, None, N_MOD, D_MODEL), lambda i, k: (layer, crow(i * tm), 0, 0)),
            pl.BlockSpec((None, None, D_MODEL, tf), lambda i, k: (layer, which, 0, k)),
            pl.BlockSpec((None, None, D_MODEL, tf), lambda i, k: (layer, which, 0, k + nk)),
            pl.BlockSpec((None, None, tf, D_MODEL), lambda i, k: (layer, which, k, 0)),
            pl.BlockSpec((None, None, 1, D_MODEL), lambda i, k: (layer, mi, 0, 0)),
            pl.BlockSpec((None, None, 1, D_MODEL), lambda i, k: (layer, mi, 0, 0)),
        ],
        out_specs=pl.BlockSpec((tm, D_MODEL), lambda i, k: (i, 0)),
        out_shape=jax.ShapeDtypeStruct((m, D_MODEL), F32),
        scratch_shapes=[pltpu.VMEM((tm, D_MODEL), BF16), pltpu.VMEM((tm, D_MODEL), F32)],
        compiler_params=_cparams("arbitrary", "arbitrary"),
        name="ffn",
    )(x, mods, w_in, w_in, w_out, ln_g, ln_b)


def _proj_kernel(x_ref, mod_ref, w_ref, *rest, seq_len):
    if seq_len:
        cw_ref, cb_ref, o_ref, h_ref = rest
    else:
        o_ref, h_ref = rest
    j = pl.program_id(1)

    @pl.when(j == 0)
    def _():
        sh = mod_ref[3:4, :]
        sc = mod_ref[4:5, :]
        h_ref[...] = (x_ref[...] * (1.0 + sc) + sh).astype(BF16)

    u = _dot(h_ref[...], w_ref[...].astype(BF16))
    if seq_len:
        tm = u.shape[0]
        pos = lax.broadcasted_iota(jnp.int32, (tm, 1), 0) % seq_len
        prev = jnp.where(pos == 0, 0.0, pltpu.roll(u, 1, 0))
        nxt = jnp.where(pos == seq_len - 1, 0.0, pltpu.roll(u, tm - 1, 0))
        u = prev * cw_ref[0:1, :] + u * cw_ref[1:2, :] + nxt * cw_ref[2:3, :] + cb_ref[...]
    o_ref[...] = u.astype(o_ref.dtype)


def _proj(x, mods, crow, w, layer, widx, *, tm, tn, out_dtype, conv=None, seq_len=0):
    m = x.shape[0]
    npg = D_MODEL // tn
    in_specs = [
        pl.BlockSpec((tm, D_MODEL), lambda i, j: (i, 0)),
        pl.BlockSpec((None, None, N_MOD, D_MODEL), lambda i, j: (layer, crow(i * tm), 0, 0)),
        pl.BlockSpec((None, D_MODEL, tn), lambda i, j: (widx, 0, j)),
    ]
    args = [x, mods, w]
    if conv is not None:
        cw, cb = conv
        in_specs += [
            pl.BlockSpec((None, 3, tn), lambda i, j: (widx, 0, j)),
            pl.BlockSpec((None, 1, tn), lambda i, j: (widx, 0, j)),
        ]
        args += [cw, cb.reshape(cb.shape[0], 1, cb.shape[1])]
    return pl.pallas_call(
        functools.partial(_proj_kernel, seq_len=seq_len),
        grid=(m // tm, 3 * npg),
        in_specs=in_specs,
        out_specs=pl.BlockSpec((None, tm, tn), lambda i, j: (j // npg, i, j % npg)),
        out_shape=jax.ShapeDtypeStruct((3, m, D_MODEL), out_dtype),
        scratch_shapes=[pltpu.VMEM((tm, D_MODEL), BF16)],
        compiler_params=_cparams("arbitrary", "arbitrary"),
        name="proj_conv" if seq_len else "proj",
    )(*args)


def _outproj_kernel(x_ref, a_ref, mod_ref, w_ref, g_ref, b_ref, o_ref):
    mix = _dot(a_ref[...].astype(BF16), w_ref[...].astype(BF16))
    y = ALPHA * x_ref[...] + mod_ref[5:6, :] * mix
    o_ref[...] = _layer_norm(y, g_ref[...], b_ref[...])


def _outproj(x, a, mods, crow, w, widx, ln_g, ln_b, layer, tm=1024):
    m = x.shape[0]
    return pl.pallas_call(
        _outproj_kernel,
        grid=(m // tm,),
        in_specs=[
            pl.BlockSpec((tm, D_MODEL), lambda i: (i, 0)),
            pl.BlockSpec((tm, D_MODEL), lambda i: (i, 0)),
            pl.BlockSpec((None, None, N_MOD, D_MODEL), lambda i: (layer, crow(i * tm), 0, 0)),
            pl.BlockSpec((None, D_MODEL, D_MODEL), lambda i: (widx, 0, 0)),
            pl.BlockSpec((None, None, 1, D_MODEL), lambda i: (layer, 1, 0, 0)),
            pl.BlockSpec((None, None, 1, D_MODEL), lambda i: (layer, 1, 0, 0)),
        ],
        out_specs=pl.BlockSpec((tm, D_MODEL), lambda i: (i, 0)),
        out_shape=jax.ShapeDtypeStruct((m, D_MODEL), F32),
        compiler_params=_cparams("arbitrary"),
        name="outproj",
    )(x, a, mods, w, ln_g, ln_b)


def _head_masks():
    lane = lax.broadcasted_iota(jnp.int32, (1, LANES), 1)
    return lane < HEAD_DIM, lane >= HEAD_DIM


def _attn_ctx_kernel(q_ref, k_ref, v_ref, o_ref):
    masks = _head_masks()
    for p in range(D_MODEL // LANES):
        sl = slice(p * LANES, (p + 1) * LANES)
        q2 = (q_ref[:, sl] * HEAD_DIM ** -0.5).astype(BF16)
        k2 = k_ref[:, sl]
        v2 = v_ref[:, sl]
        o2 = jnp.zeros(q2.shape, F32)
        for hm in masks:
            kh = jnp.where(hm, k2, 0.0).astype(BF16)
            vh = jnp.where(hm, v2, 0.0).astype(BF16)
            s = _dot_nt(q2, kh)
            e = jnp.exp(s - jnp.max(s, axis=-1, keepdims=True))
            inv = 1.0 / jnp.sum(e, axis=-1, keepdims=True)
            o2 = o2 + _dot(e.astype(BF16), vh) * inv
        o_ref[:, sl] = o2.astype(o_ref.dtype)


def _attn_ctx(qkv, batch, seq):
    m = batch * seq
    spec = lambda g: pl.BlockSpec((None, seq, D_MODEL), lambda b: (g, b, 0))
    return pl.pallas_call(
        _attn_ctx_kernel,
        grid=(batch,),
        in_specs=[spec(0), spec(1), spec(2)],
        out_specs=pl.BlockSpec((seq, D_MODEL), lambda b: (b, 0)),
        out_shape=jax.ShapeDtypeStruct((m, D_MODEL), BF16),
        compiler_params=_cparams("arbitrary"),
        name="attn_ctx",
    )(qkv, qkv, qkv)


def _nbr_geometry(rows):
    kr = min(WIN_ROWS, rows)
    cases, blocks = {}, []
    for blk in range(rows // Q_ROWS):
        r0 = blk * Q_ROWS
        lo = min(max(r0 - kr // 2, 0), rows - kr)
        key_start = min(lo, rows - K_ROWS)
        bands = []
        for r in range(r0, r0 + Q_ROWS):
            rs = min(max(r - kr // 2, 0), rows - kr)
            a = rs - key_start
            band_start = min(a - a % 2, K_ROWS - BAND_ROWS)
            off = a - band_start
            assert 0 <= band_start and 0 <= off and off + kr <= BAND_ROWS
            key = (rs - r + WIN_ROWS - 1, off)
            bands.append((band_start, cases.setdefault(key, len(cases))))
        blocks.append((key_start, bands))
    return kr, list(cases), blocks


def _nbr_bias_slabs(rpb, kr, cases):
    qc = np.arange(GRID_W)[:, None]
    kc = np.arange(GRID_W)[None, :]
    cstart = np.clip(qc - WIN_COLS // 2, 0, GRID_W - WIN_COLS)
    col_ok = (kc >= cstart) & (kc < cstart + WIN_COLS)
    dcol = np.clip(kc - qc, 1 - WIN_COLS, WIN_COLS - 1) + WIN_COLS - 1
    j = np.arange(BAND_ROWS)
    drow = np.stack([np.clip(s - off + j, 0, 2 * WIN_ROWS - 2) for s, off in cases])
    row_ok = np.stack([(j >= off) & (j < off + kr) for s, off in cases])
    bias = rpb[:, drow[:, :, None, None], dcol[None, None]]
    ok = row_ok[:, :, None, None] & col_ok[None, None]
    bias = jnp.where(jnp.asarray(ok)[None], bias, NEG)
    return bias.transpose(0, 1, 3, 2, 4).reshape(N_HEADS, len(cases), GRID_W, BAND_ROWS * GRID_W)


def _attn_nbr_kernel(q_ref, k_ref, v_ref, kc_ref, vc_ref, bias_ref, o_ref,
                     s_ref, sc_ref, p_ref, pc_ref, inv_ref, *, blocks, n_cases):
    masks = _head_masks()
    qn = Q_ROWS * GRID_W
    kn = K_ROWS * GRID_W
    bw = BAND_ROWS * GRID_W
    for blk, (key_start, bands) in enumerate(blocks):
        q2 = (q_ref[blk * qn:(blk + 1) * qn, :] * HEAD_DIM ** -0.5).astype(BF16)
        k2 = k_ref[key_start * GRID_W:key_start * GRID_W + kn, :]
        v2 = v_ref[key_start * GRID_W:key_start * GRID_W + kn, :]
        o2 = jnp.zeros(q2.shape, F32)
        p_ref[...] = jnp.zeros_like(p_ref)
        for hh, hm in enumerate(masks):
            kh = jnp.where(hm, k2, 0.0).astype(BF16)
            vh = jnp.where(hm, v2, 0.0).astype(BF16)
            kch = jnp.where(hm, kc_ref[...], 0.0).astype(BF16)
            vch = jnp.where(hm, vc_ref[...], 0.0).astype(BF16)
            s_ref[...] = _dot_nt(q2, kh)
            sc_ref[...] = _dot_nt(q2, kch)
            for rl, (band_start, case) in enumerate(bands):
                rsl = slice(rl * GRID_W, (rl + 1) * GRID_W)
                csl = slice(band_start * GRID_W, band_start * GRID_W + bw)
                sb = s_ref[rsl, csl] + bias_ref[hh * n_cases + case]
                sc = sc_ref[rsl, :]
                mx = jnp.maximum(jnp.max(sb, axis=-1, keepdims=True), jnp.max(sc, axis=-1, keepdims=True))
                eb = jnp.exp(sb - mx)
                ec = jnp.exp(sc - mx)
                inv_ref[rsl, :] = 1.0 / (jnp.sum(eb, axis=-1, keepdims=True) + jnp.sum(ec, axis=-1, keepdims=True))
                p_ref[rsl, csl] = eb.astype(BF16)
                pc_ref[rsl, :] = ec.astype(BF16)
            o2 = o2 + (_dot(p_ref[...], vh) + _dot(pc_ref[...], vch)) * inv_ref[...]
        o_ref[blk * qn:(blk + 1) * qn, :] = o2.astype(o_ref.dtype)


def _attn_nbr(qkv, kc, vc, rpb, batch, seq):
    rows = seq // GRID_W
    kr, cases, blocks = _nbr_geometry(rows)
    n_cases = len(cases)
    n_pairs = D_MODEL // LANES
    bias = _nbr_bias_slabs(rpb, kr, cases).reshape(n_pairs, 2 * n_cases, GRID_W, BAND_ROWS * GRID_W)
    s_ctx = kc.shape[1]
    qspec = lambda g: pl.BlockSpec((None, seq, LANES), lambda b, p: (g, b, p))
    cspec = pl.BlockSpec((None, s_ctx, LANES), lambda b, p: (b, 0, p))
    qn, kn = Q_ROWS * GRID_W, K_ROWS * GRID_W
    return pl.pallas_call(
        functools.partial(_attn_nbr_kernel, blocks=blocks, n_cases=n_cases),
        grid=(batch, n_pairs),
        in_specs=[qspec(0), qspec(1), qspec(2), cspec, cspec,
                  pl.BlockSpec((None, 2 * n_cases, GRID_W, BAND_ROWS * GRID_W), lambda b, p: (p, 0, 0, 0))],
        out_specs=pl.BlockSpec((seq, LANES), lambda b, p: (b, p)),
        out_shape=jax.ShapeDtypeStruct((batch * seq, D_MODEL), BF16),
        scratch_shapes=[pltpu.VMEM((qn, kn), F32), pltpu.VMEM((qn, s_ctx), F32),
                        pltpu.VMEM((qn, kn), BF16), pltpu.VMEM((qn, s_ctx), BF16),
                        pltpu.VMEM((qn, 1), F32)],
        compiler_params=_cparams("arbitrary", "arbitrary"),
        name="attn_nbr",
    )(qkv, qkv, qkv, kc, vc, bias)


def _dft_tables(seq_len, tq):
    n = 2 * seq_len
    f = np.arange(seq_len, dtype=np.float64)[:, None] + 0.5
    s = np.arange(seq_len, dtype=np.float64)[None, :]
    ang = 2.0 * np.pi * f * s / n
    re = np.cos(ang).reshape(seq_len // tq, tq, seq_len)
    im = (-np.sin(ang)).reshape(seq_len // tq, tq, seq_len)
    fwd = np.concatenate([re, im], axis=1).astype(np.float32)
    return fwd, np.ascontiguousarray(fwd.transpose(0, 2, 1))


def _filter_features(seq_len):
    pos = np.arange(seq_len, dtype=np.float64)[:, None]
    t = pos / max(seq_len - 1, 1)
    bands = np.linspace(1e-4, HY_BANDS - 1, HY_BANDS)
    ang = 2.0 * np.pi * pos / seq_len * bands
    z = np.concatenate([t, np.cos(ang), -np.sin(ang)], axis=-1)
    zp = np.zeros((seq_len, HY_FILTER_W), np.float64)
    zp[:, :HY_EMB] = z
    deltas = np.abs(np.linspace(math.log(HY_TARGET) / HY_SLOW_DECAY, math.log(HY_TARGET) / HY_FAST_DECAY, D_MODEL))
    return zp.astype(np.float32), t.astype(np.float32), deltas.astype(np.float32)[None, :]


def _filter_kernel(z_ref, t_ref, dl_ref, w1_ref, b1_ref, f1_ref, w2_ref, b2_ref, f2_ref, w3_ref, b3_ref, o_ref):
    hp = lax.Precision.HIGHEST
    hid = jnp.sin(f1_ref[...] * (jnp.dot(z_ref[...], w1_ref[...], precision=hp, preferred_element_type=F32) + b1_ref[...]))
    hid = jnp.sin(f2_ref[...] * (jnp.dot(hid, w2_ref[...], precision=hp, preferred_element_type=F32) + b2_ref[...]))
    filt = jnp.dot(hid, w3_ref[...], precision=hp, preferred_element_type=F32) + b3_ref[...]
    decay = jnp.exp(-t_ref[...] * dl_ref[...])
    hf = filt[:, :D_MODEL] * decay
    hb = filt[:, D_MODEL:] * decay
    row = lax.broadcasted_iota(jnp.int32, (hb.shape[0], 1), 0)
    hb = jnp.where(row == 0, 0.0, hb)
    o_ref[:, :D_MODEL] = (hf + hb).astype(o_ref.dtype)
    o_ref[:, D_MODEL:] = (hf - hb).astype(o_ref.dtype)


def _filters(seq_len, w1, b1, f1, w2, b2, f2, w3, b3):
    z, t, deltas = _filter_features(seq_len)
    w1p = jnp.zeros((HY_FILTER_W, HY_FILTER_W), F32).at[:HY_EMB].set(w1)
    full = lambda shape: pl.BlockSpec(shape, lambda o: (0,) * len(shape))
    row = lambda a: a.reshape(1, -1)
    return pl.pallas_call(
        _filter_kernel,
        grid=(2,),
        in_specs=[full((seq_len, HY_FILTER_W)), full((seq_len, 1)), full((1, D_MODEL)),
                  full((HY_FILTER_W, HY_FILTER_W)), full((1, HY_FILTER_W)), full((1, HY_FILTER_W)),
                  full((HY_FILTER_W, HY_FILTER_W)), full((1, HY_FILTER_W)), full((1, HY_FILTER_W)),
                  pl.BlockSpec((HY_FILTER_W, 2 * D_MODEL), lambda o: (0, o)),
                  pl.BlockSpec((1, 2 * D_MODEL), lambda o: (0, o))],
        out_specs=pl.BlockSpec((None, seq_len, 2 * D_MODEL), lambda o: (o, 0, 0)),
        out_shape=jax.ShapeDtypeStruct((2, seq_len, 2 * D_MODEL), BF16),
        compiler_params=_cparams("arbitrary"),
        name="hyena_filters",
    )(jnp.asarray(z),jnp.asarray(t), jnp.asarray(deltas), w1p, row(b1), row(f1), w2, row(b2), row(f2), w3, row(b3))


def _spectrum_kernel(g_ref, hs_ref, hd_ref, o_ref, *, tq, scale):
    o_ref[:tq, :] = scale * _dot(g_ref[:tq, :], hs_ref[...])
    o_ref[tq:, :] = scale * _dot(g_ref[tq:, :], hd_ref[...])


def _spectrum(fwd, filt, seq_len, tq, tc):
    nf = seq_len // tq
    ncb = D_MODEL // tc
    return pl.pallas_call(
        functools.partial(_spectrum_kernel, tq=tq, scale=1.0 / seq_len),
        grid=(2, ncb, nf),
        in_specs=[pl.BlockSpec((None, 2 * tq, seq_len), lambda o, c, f: (f, 0, 0)),
                  pl.BlockSpec((None, seq_len, tc), lambda o, c, f: (o, 0, c)),
                  pl.BlockSpec((None, seq_len, tc), lambda o, c, f: (o, 0, c + ncb))],
        out_specs=pl.BlockSpec((None, None, 2 * tq, tc), lambda o, c, f: (o, f, 0, c)),
        out_shape=jax.ShapeDtypeStruct((2, nf, 2 * tq, D_MODEL), F32),
        compiler_params=_cparams("arbitrary", "arbitrary", "arbitrary"),
        name="hyena_spectrum",
    )(fwd, filt, filt)


def _hyena_kernel(v_ref, x1_ref, x2_ref, g_ref, gt_ref, k_ref, b_ref, o_ref, z_ref, acc_ref, *, tq, nf):
    o = pl.program_id(2)
    f = pl.program_id(3)

    @pl.when((o == 0) & (f == 0))
    def _():
        z_ref[...] = v_ref[...].astype(BF16)

    @pl.when(f == 0)
    def _():
        acc_ref[...] = jnp.zeros_like(acc_ref)

    zf = _dot(g_ref[...], z_ref[...])
    zr, zi = zf[:tq], zf[tq:]
    kr, ki = k_ref[:tq, :], k_ref[tq:, :]
    prod = jnp.concatenate([zr * kr - zi * ki, zr * ki + zi * kr], axis=0).astype(BF16)
    acc_ref[...] += _dot(gt_ref[...], prod)

    @pl.when(f == nf - 1)
    def _():
        y = acc_ref[...] + z_ref[...].astype(F32) * b_ref[...]

        @pl.when(o == 0)
        def _():
            z_ref[...] = (y * x1_ref[...].astype(F32)).astype(BF16)

        @pl.when(o == 1)
        def _():
            o_ref[...] = (y * x2_ref[...].astype(F32)).astype(o_ref.dtype)


def _hyena_core(u, fwd, bwd, spec, bias, batch, seq_len, tq, tc):
    nf = seq_len // tq
    ncb = D_MODEL // tc
    uspec = lambda g: pl.BlockSpec((None, seq_len, tc), lambda b, c, o, f: (g, b, c))
    return pl.pallas_call(
        functools.partial(_hyena_kernel, tq=tq, nf=nf),
        grid=(batch, ncb, 2, nf),
        in_specs=[uspec(0), uspec(1), uspec(2),
                  pl.BlockSpec((None, 2 * tq, seq_len), lambda b, c, o, f: (f, 0, 0)),
                  pl.BlockSpec((None, seq_len, 2 * tq), lambda b, c, o, f: (f, 0, 0)),
                  pl.BlockSpec((None, None, 2 * tq, tc), lambda b, c, o, f: (o, f, 0, c)),
                  pl.BlockSpec((None, 1, tc), lambda b, c, o, f: (o, 0, c))],
        out_specs=pl.BlockSpec((seq_len, tc), lambda b, c, o, f: (b, c)),
        out_shape=jax.ShapeDtypeStruct((batch * seq_len, D_MODEL), BF16),
        scratch_shapes=[pltpu.VMEM((seq_len, tc), BF16), pltpu.VMEM((seq_len, tc), F32)],
        compiler_params=_cparams("arbitrary", "arbitrary", "arbitrary", "arbitrary"),
        name="hyena_conv",
    )(u, u, u, fwd, bwd, spec, bias.reshape(2, 1, D_MODEL))


def _hyena_tiles(seq_len):
    tq = min(seq_len, 256)
    tc = D_MODEL if seq_len <= 256 else 512
    return tq, tc


def kernel(x_prompt, x_sample, cache_k, cache_v, c, c_ctx, mod_w, mod_b, ln_g, ln_b, ffn_w_in, ffn_w_out,
           attn_w_qkv, attn_w_o, attn_rpb, hy_w_in, hy_conv_w, hy_conv_b, hy_f_w1, hy_f_b1, hy_f_freq1,
           hy_f_w2, hy_f_b2, hy_f_freq2, hy_f_w3, hy_f_b3, hy_bias, hy_w_out):
    batch, seq, _ = x_prompt.shape
    dec_batch, dec_seq, _ = x_sample.shape
    assert dec_batch + 1 <= COND_ROWS

    conds = jnp.zeros((COND_ROWS, D_MODEL), F32).at[0].set(c_ctx).at[1:1 + dec_batch].set(c)
    mods = _mods(conds, mod_w, mod_b).reshape(DEPTH, COND_ROWS, N_MOD, D_MODEL)
    ln_g4 = ln_g.reshape(DEPTH, 3, 1, D_MODEL)
    ln_b4 = ln_b.reshape(DEPTH, 3, 1, D_MODEL)

    streams = {
        "ctx": dict(x=x_prompt.reshape(batch * seq, D_MODEL), batch=batch, seq=seq, crow=lambda t: 0),
        "lat": dict(x=x_sample.reshape(dec_batch * dec_seq, D_MODEL), batch=dec_batch, seq=dec_seq,
                    crow=lambda t: 1 + t // dec_seq),
    }
    hyena_consts = {}
    for name, st in streams.items():
        sl = st["seq"]
        tq, tc = _hyena_tiles(sl)
        fwd, bwd = _dft_tables(sl, tq)
        hyena_consts[name] = (jnp.asarray(fwd).astype(BF16), jnp.asarray(bwd).astype(BF16), tq, tc)

    new_k, new_v = [], []
    outs = {}
    for name, st in streams.items():
        x, nb, sl, crow = st["x"], st["batch"], st["seq"], st["crow"]
        for layer in range(DEPTH):
            j = layer // 2
            x = _ffn(x, mods, crow, ffn_w_in, ffn_w_out, ln_g4, ln_b4, layer, 0)
            if layer % 2 == 0:
                if name == "ctx":
                    qkv = _proj(x, mods, crow, attn_w_qkv, layer, j, tm=1024, tn=512, out_dtype=F32)
                    a = _attn_ctx(qkv, nb, sl)
                    new_k.append(qkv[1].reshape(nb, sl, N_HEADS, HEAD_DIM))
                    new_v.append(qkv[2].reshape(nb, sl, N_HEADS, HEAD_DIM))
                else:
                    qkv = _proj(x, mods, crow, attn_w_qkv, layer, j, tm=1024, tn=512, out_dtype=BF16)
                    kc = cache_k[:, j].reshape(nb, -1, D_MODEL)
                    vc = cache_v[:, j].reshape(nb, -1, D_MODEL)
                    a = _attn_nbr(qkv, kc, vc, attn_rpb[j], nb, sl)
                x = _outproj(x, a, mods, crow, attn_w_o, j, ln_g4, ln_b4, layer)
            else:
                fwd, bwd, tq, tc = hyena_consts[name]
                filt = _filters(sl, hy_f_w1[j], hy_f_b1[j], hy_f_freq1[j], hy_f_w2[j], hy_f_b2[j], hy_f_freq2[j],
                                hy_f_w3[j], hy_f_b3[j])
                spec = _spectrum(fwd, filt, sl, tq, tc)
                tm = max(sl, 1024)
                u = _proj(x, mods, crow, hy_w_in, layer, j, tm=tm, tn=256, out_dtype=BF16,
                          conv=(hy_conv_w, hy_conv_b), seq_len=sl)
                a = _hyena_core(u, fwd, bwd, spec, hy_bias[j], nb, sl, tq, tc)
                x = _outproj(x, a, mods, crow, hy_w_out, j, ln_g4, ln_b4, layer)
            x = _ffn(x, mods, crow, ffn_w_in, ffn_w_out, ln_g4, ln_b4, layer, 1)
        outs[name] = x

    y_prompt = outs["ctx"].reshape(batch, seq, D_MODEL)
    y_sample = outs["lat"].reshape(dec_batch, dec_seq, D_MODEL)
    return (y_prompt, y_sample, jnp.stack(new_k, axis=1), jnp.stack(new_v, axis=1))
```

```python
import functools
import math

import numpy as np
import jax
import jax.numpy as jnp
from jax import lax
from jax.experimental import pallas as pl
from jax.experimental.pallas import tpu as pltpu

D_MODEL = 1024
DEPTH = 2
GRID_W = 64
N_HEADS = 16
HEAD_DIM = D_MODEL // N_HEADS
WIN_ROWS = 8
WIN_COLS = 16
D_FF = 2816
N_MOD = 9
HY_BANDS = 16
HY_EMB = 1 + 2 * HY_BANDS
HY_FILTER_W = 64
HY_FAST_DECAY = 0.3
HY_SLOW_DECAY = 1.5
HY_TARGET = 1e-2
ALPHA = (2 * DEPTH) ** 0.25
LN_EPS = 1e-5

F32 = jnp.float32
BF16 = jnp.bfloat16
NEG = -1e30
LANES = 128
VMEM_LIMIT = 56 * 1024 * 1024
COND_ROWS = 8

Q_ROWS = 8
K_ROWS = 16
BAND_ROWS = 10


def _cparams(*sem):
    return pltpu.CompilerParams(dimension_semantics=sem, vmem_limit_bytes=VMEM_LIMIT)


def _dot(a, b):
    return jnp.dot(a, b, preferred_element_type=F32)


def _dot_nt(a, b):
    return lax.dot_general(a, b, (((1,), (1,)), ((), ())), preferred_element_type=F32)


def _layer_norm(y, g, b):
    mu = jnp.mean(y, axis=-1, keepdims=True)
    yc = y - mu
    var = jnp.mean(yc * yc, axis=-1, keepdims=True)
    return yc * lax.rsqrt(var + LN_EPS) * g + b


def _mods_kernel(c_ref, w_ref, b_ref, o_ref):
    c = c_ref[...]
    s = c * jax.nn.sigmoid(c)
    o_ref[...] = _dot(s.astype(BF16), w_ref[...].astype(BF16)) + b_ref[...]


def _mods(conds, mod_w, mod_b):
    tn = 1024
    nd = N_MOD * D_MODEL
    return pl.pallas_call(
        _mods_kernel,
        grid=(DEPTH, nd // tn),
        in_specs=[
            pl.BlockSpec((COND_ROWS, D_MODEL), lambda l, j: (0, 0)),
            pl.BlockSpec((None, D_MODEL, tn), lambda l, j: (l, 0, j)),
            pl.BlockSpec((None, 1, tn), lambda l, j: (l, 0, j)),
        ],
        out_specs=pl.BlockSpec((None, COND_ROWS, tn), lambda l, j: (l, 0, j)),
        out_shape=jax.ShapeDtypeStruct((DEPTH, COND_ROWS, nd), F32),
        compiler_params=_cparams("arbitrary", "arbitrary"),
        name="mods",
    )(conds, mod_w, mod_b.reshape(DEPTH, 1, nd))


def _ffn_kernel(x_ref, mod_ref, wi_ref, wo_ref, g_ref, b_ref, o_ref, *, mi, n_chunks):
    x = x_ref[...]
    sh = mod_ref[3 * mi:3 * mi + 1, :]
    sc = mod_ref[3 * mi + 1:3 * mi + 2, :]
    gt = mod_ref[3 * mi + 2:3 * mi + 3, :]
    h = (x * (1.0 + sc) + sh).astype(BF16)
    tf = D_FF // n_chunks
    f = None
    for c in range(n_chunks):
        gate = _dot(h, wi_ref[:, c * tf:(c + 1) * tf])
        up = _dot(h, wi_ref[:, D_FF + c * tf:D_FF + (c + 1) * tf])
        a = (gate * jax.nn.sigmoid(gate) * up).astype(BF16)
        part = _dot(a, wo_ref[c * tf:(c + 1) * tf, :])
        f = part if f is None else f + part
    y = ALPHA * x + (0.5 * gt) * f
    o_ref[...] = _layer_norm(y, g_ref[...], b_ref[...])


def _resident(shape, index_map):
    return pl.BlockSpec(shape, index_map, pipeline_mode=pl.Buffered(1))


def _ffn(x, mods, crow, w_in, w_out, ln_g, ln_b, layer, which, tm=1024, n_chunks=11):
    m = x.shape[0]
    mi = 0 if which == 0 else 2
    return pl.pallas_call(
        functools.partial(_ffn_kernel, mi=mi, n_chunks=n_chunks),
        grid=(m // tm,),
        in_specs=[
            pl.BlockSpec((tm, D_MODEL), lambda i: (i, 0)),
            pl.BlockSpec((None, None, N_MOD, D_MODEL), lambda i: (layer, crow(i * tm), 0, 0)),
            _resident((None, None, D_MODEL, 2 * D_FF), lambda i: (layer, which, 0, 0)),
            _resident((None, None, D_FF, D_MODEL), lambda i: (layer, which, 0, 0)),
            pl.BlockSpec((None, None, 1, D_MODEL), lambda i: (layer, mi, 0, 0)),
            pl.BlockSpec((None, None, 1, D_MODEL), lambda i: (layer, mi, 0, 0)),
        ],
        out_specs=pl.BlockSpec((tm, D_MODEL), lambda i: (i, 0)),
        out_shape=jax.ShapeDtypeStruct((m, D_MODEL), F32),
        compiler_params=_cparams("arbitrary"),
        name="ffn",
    )(x, mods, w_in, w_out, ln_g, ln_b)


def _proj_kernel(x_ref, mod_ref, w_ref, *rest, seq_len):
    if seq_len:
        cw_ref, cb_ref, o_ref, h_ref = rest
    else:
        o_ref, h_ref = rest
    j = pl.program_id(1)

    @pl.when(j == 0)
    def _():
        sh = mod_ref[3:4, :]
        sc = mod_ref[4:5, :]
        h_ref[...] = (x_ref[...] * (1.0 + sc) + sh).astype(BF16)

    u = _dot(h_ref[...], w_ref[...])
    if seq_len:
        tm = u.shape[0]
        pos = lax.broadcasted_iota(jnp.int32, (tm, 1), 0) % seq_len
        prev = jnp.where(pos == 0, 0.0, pltpu.roll(u, 1, 0))
        nxt = jnp.where(pos == seq_len - 1, 0.0, pltpu.roll(u, tm - 1, 0))
        u = prev * cw_ref[0:1, :] + u * cw_ref[1:2, :] + nxt * cw_ref[2:3, :] + cb_ref[...]
    o_ref[...] = u.astype(o_ref.dtype)


def _proj(x, mods, crow, w, layer, widx, *, tm, tn, out_dtype, conv=None, seq_len=0):
    m = x.shape[0]
    npg = D_MODEL // tn
    in_specs = [
        pl.BlockSpec((tm, D_MODEL), lambda i, j: (i, 0)),
        pl.BlockSpec((None, None, N_MOD, D_MODEL), lambda i, j: (layer, crow(i * tm), 0, 0)),
        pl.BlockSpec((None, D_MODEL, tn), lambda i, j: (widx, 0, j)),
    ]
    args = [x, mods, w]
    if conv is not None:
        cw, cb = conv
        in_specs += [
            pl.BlockSpec((None, 3, tn), lambda i, j: (widx, 0, j)),
            pl.BlockSpec((None, 1, tn), lambda i, j: (widx, 0, j)),
        ]
        args += [cw, cb.reshape(cb.shape[0], 1, cb.shape[1])]
    return pl.pallas_call(
        functools.partial(_proj_kernel, seq_len=seq_len),
        grid=(m // tm, 3 * npg),
        in_specs=in_specs,
        out_specs=pl.BlockSpec((None, tm, tn), lambda i, j: (j // npg, i, j % npg)),
        out_shape=jax.ShapeDtypeStruct((3, m, D_MODEL), out_dtype),
        scratch_shapes=[pltpu.VMEM((tm, D_MODEL), BF16)],
        compiler_params=_cparams("arbitrary", "arbitrary"),
        name="proj_conv" if seq_len else "proj",
    )(*args)


def _outproj_kernel(x_ref, a_ref, mod_ref, w_ref, g_ref, b_ref, o_ref):
    mix = _dot(a_ref[...], w_ref[...])
    y = ALPHA * x_ref[...] + mod_ref[5:6, :] * mix
    o_ref[...] = _layer_norm(y, g_ref[...], b_ref[...])


def _outproj(x, a, mods, crow, w, widx, ln_g, ln_b, layer, tm=1024):
    m = x.shape[0]
    return pl.pallas_call(
        _outproj_kernel,
        grid=(m // tm,),
        in_specs=[
            pl.BlockSpec((tm, D_MODEL), lambda i: (i, 0)),
            pl.BlockSpec((tm, D_MODEL), lambda i: (i, 0)),
            pl.BlockSpec((None, None, N_MOD, D_MODEL), lambda i: (layer, crow(i * tm), 0, 0)),
            pl.BlockSpec((None, D_MODEL, D_MODEL), lambda i: (widx, 0, 0)),
            pl.BlockSpec((None, None, 1, D_MODEL), lambda i: (layer, 1, 0, 0)),
            pl.BlockSpec((None, None, 1, D_MODEL), lambda i: (layer, 1, 0, 0)),
        ],
        out_specs=pl.BlockSpec((tm, D_MODEL), lambda i: (i, 0)),
        out_shape=jax.ShapeDtypeStruct((m, D_MODEL), F32),
        compiler_params=_cparams("arbitrary"),
        name="outproj",
    )(x, a, mods, w, ln_g, ln_b)


def _head_masks():
    lane = lax.broadcasted_iota(jnp.int32, (1, LANES), 1)
    return lane < HEAD_DIM, lane >= HEAD_DIM


def _attn_ctx_kernel(q_ref, k_ref, v_ref, o_ref, ko_ref, vo_ref):
    masks = _head_masks()
    ko_ref[...] = k_ref[...].reshape(ko_ref.shape)
    vo_ref[...] = v_ref[...].reshape(vo_ref.shape)
    for p in range(D_MODEL // LANES):
        sl = slice(p * LANES, (p + 1) * LANES)
        q2 = (q_ref[:, sl] * HEAD_DIM ** -0.5).astype(BF16)
        k2 = k_ref[:, sl]
        v2 = v_ref[:, sl]
        o2 = jnp.zeros(q2.shape, F32)
        for hm in masks:
            kh = jnp.where(hm, k2, 0.0).astype(BF16)
            vh = jnp.where(hm, v2, 0.0).astype(BF16)
            s = _dot_nt(q2, kh)
            e = jnp.exp(s - jnp.max(s, axis=-1, keepdims=True))
            inv = 1.0 / jnp.sum(e, axis=-1, keepdims=True)
            o2 = o2 + _dot(e.astype(BF16), vh) * inv
        o_ref[:, sl] = o2.astype(o_ref.dtype)


def _attn_ctx(qkv, batch, seq):
    m = batch * seq
    spec = lambda g: pl.BlockSpec((None, seq, D_MODEL), lambda b: (g, b, 0))
    cache_spec = pl.BlockSpec((None, None, seq, N_HEADS, HEAD_DIM), lambda b: (b, 0, 0, 0, 0))
    cache_shape = jax.ShapeDtypeStruct((batch, 1, seq, N_HEADS, HEAD_DIM), F32)
    return pl.pallas_call(
        _attn_ctx_kernel,
        grid=(batch,),
        in_specs=[spec(0), spec(1), spec(2)],
        out_specs=[pl.BlockSpec((seq, D_MODEL), lambda b: (b, 0)), cache_spec, cache_spec],
        out_shape=[jax.ShapeDtypeStruct((m, D_MODEL), BF16), cache_shape, cache_shape],
        compiler_params=_cparams("arbitrary"),
        name="attn_ctx",
    )(qkv, qkv, qkv)


def _nbr_geometry(rows):
    kr = min(WIN_ROWS, rows)
    cases, blocks = {}, []
    for blk in range(rows // Q_ROWS):
        r0 = blk * Q_ROWS
        lo = min(max(r0 - kr // 2, 0), rows - kr)
        key_start = min(lo, rows - K_ROWS)
        bands = []
        for r in range(r0, r0 + Q_ROWS):
            rs = min(max(r - kr // 2, 0), rows - kr)
            a = rs - key_start
            band_start = min(a - a % 2, K_ROWS - BAND_ROWS)
            off = a - band_start
            assert 0 <= band_start and 0 <= off and off + kr <= BAND_ROWS
            key = (rs - r + WIN_ROWS - 1, off)
            bands.append((band_start, cases.setdefault(key, len(cases))))
        blocks.append((key_start, bands))
    return kr, list(cases), blocks


def _nbr_bias_slabs(rpb, kr, cases):
    qc = np.arange(GRID_W)[:, None]
    kc = np.arange(GRID_W)[None, :]
    cstart = np.clip(qc - WIN_COLS // 2, 0, GRID_W - WIN_COLS)
    col_ok = (kc >= cstart) & (kc < cstart + WIN_COLS)
    n_d = 2 * WIN_ROWS - 1
    lead = GRID_W - WIN_COLS
    ext = jnp.pad(rpb, ((0, 0), (0, 0), (lead, 2 * GRID_W - lead - (2 * WIN_COLS - 1))))
    flat = jnp.tile(ext, (1, 1, GRID_W))[:, :, GRID_W - 1:GRID_W - 1 + GRID_W * (2 * GRID_W - 1)]
    toep = flat.reshape(N_HEADS, n_d, GRID_W, 2 * GRID_W - 1)[:, :, :, :GRID_W]
    toep = jnp.where(jnp.asarray(col_ok)[None, None], toep, NEG)
    masked = jnp.full((N_HEADS, GRID_W, GRID_W), NEG, F32)
    slabs = []
    for s, off in cases:
        parts = [toep[:, s - off + j] if off <= j < off + kr else masked for j in range(BAND_ROWS)]
        slabs.append(jnp.concatenate(parts, axis=-1))
    return jnp.stack(slabs, axis=1)


def _attn_nbr_kernel(q_ref, k_ref, v_ref, kc_ref, vc_ref, bias_ref, o_ref,
                     s_ref, sc_ref, p_ref, pc_ref, inv_ref, *, blocks, n_cases):
    masks = _head_masks()
    qn = Q_ROWS * GRID_W
    kn = K_ROWS * GRID_W
    bw = BAND_ROWS * GRID_W
    for blk, (key_start, bands) in enumerate(blocks):
        q2 = (q_ref[blk * qn:(blk + 1) * qn, :] * HEAD_DIM ** -0.5).astype(BF16)
        k2 = k_ref[key_start * GRID_W:key_start * GRID_W + kn, :]
        v2 = v_ref[key_start * GRID_W:key_start * GRID_W + kn, :]
        o2 = jnp.zeros(q2.shape, F32)
        p_ref[...] = jnp.zeros_like(p_ref)
        for hh, hm in enumerate(masks):
            kh = jnp.where(hm, k2, 0.0).astype(BF16)
            vh = jnp.where(hm, v2, 0.0).astype(BF16)
            kch = jnp.where(hm, kc_ref[...], 0.0).astype(BF16)
            vch = jnp.where(hm, vc_ref[...], 0.0).astype(BF16)
            s_ref[...] = _dot_nt(q2, kh)
            sc_ref[...] = _dot_nt(q2, kch)
            for rl, (band_start, case) in enumerate(bands):
                rsl = slice(rl * GRID_W, (rl + 1) * GRID_W)
                csl = slice(band_start * GRID_W, band_start * GRID_W + bw)
                sb = s_ref[rsl, csl] + bias_ref[hh * n_cases + case]
                sc = sc_ref[rsl, :]
                mx = jnp.maximum(jnp.max(sb, axis=-1, keepdims=True), jnp.max(sc, axis=-1, keepdims=True))
                eb = jnp.exp(sb - mx)
                ec = jnp.exp(sc - mx)
                inv_ref[rsl, :] = 1.0 / (jnp.sum(eb, axis=-1, keepdims=True) + jnp.sum(ec, axis=-1, keepdims=True))
                p_ref[rsl, csl] = eb.astype(BF16)
                pc_ref[rsl, :] = ec.astype(BF16)
            o2 = o2 + (_dot(p_ref[...], vh) + _dot(pc_ref[...], vch)) * inv_ref[...]
        o_ref[blk * qn:(blk + 1) * qn, :] = o2.astype(o_ref.dtype)


def _attn_nbr(qkv, kc, vc, rpb, batch, seq):
    rows = seq // GRID_W
    kr, cases, blocks = _nbr_geometry(rows)
    n_cases = len(cases)
    n_pairs = D_MODEL // LANES
    bias = _nbr_bias_slabs(rpb, kr, cases).reshape(n_pairs, 2 * n_cases, GRID_W, BAND_ROWS * GRID_W)
    s_ctx = kc.shape[1]
    qspec = lambda g: pl.BlockSpec((None, seq, LANES), lambda b, p: (g, b, p))
    cspec = pl.BlockSpec((None, s_ctx, LANES), lambda b, p: (b, 0, p))
    qn, kn = Q_ROWS * GRID_W, K_ROWS * GRID_W
    return pl.pallas_call(
        functools.partial(_attn_nbr_kernel, blocks=blocks, n_cases=n_cases),
        grid=(batch, n_pairs),
        in_specs=[qspec(0), qspec(1), qspec(2), cspec, cspec,
                  pl.BlockSpec((None, 2 * n_cases, GRID_W, BAND_ROWS * GRID_W), lambda b, p: (p, 0, 0, 0))],
        out_specs=pl.BlockSpec((seq, LANES), lambda b, p: (b, p)),
        out_shape=jax.ShapeDtypeStruct((batch * seq, D_MODEL), BF16),
        scratch_shapes=[pltpu.VMEM((qn, kn), F32), pltpu.VMEM((qn, s_ctx), F32),
                        pltpu.VMEM((qn, kn), BF16), pltpu.VMEM((qn, s_ctx), BF16),
                        pltpu.VMEM((qn, 1), F32)],
        compiler_params=_cparams("arbitrary", "arbitrary"),
        name="attn_nbr",
    )(qkv, qkv, qkv, kc, vc, bias)


def _dft_tables(seq_len, tq):
    n = 2 * seq_len
    f = np.arange(seq_len, dtype=np.float64)[:, None] + 0.5
    s = np.arange(seq_len, dtype=np.float64)[None, :]
    ang = 2.0 * np.pi * f * s / n
    re = np.cos(ang).reshape(seq_len // tq, tq, seq_len)
    im = (-np.sin(ang)).reshape(seq_len // tq, tq, seq_len)
    fwd = np.concatenate([re, im], axis=1).astype(np.float32)
    return fwd, np.ascontiguousarray(fwd.transpose(0, 2, 1))


def _filter_features(seq_len):
    pos = np.arange(seq_len, dtype=np.float64)[:, None]
    t = pos / max(seq_len - 1, 1)
    bands = np.linspace(1e-4, HY_BANDS - 1, HY_BANDS)
    ang = 2.0 * np.pi * pos / seq_len * bands
    z = np.concatenate([t, np.cos(ang), -np.sin(ang)], axis=-1)
    zp = np.zeros((seq_len, HY_FILTER_W), np.float64)
    zp[:, :HY_EMB] = z
    deltas = np.abs(np.linspace(math.log(HY_TARGET) / HY_SLOW_DECAY, math.log(HY_TARGET) / HY_FAST_DECAY, D_MODEL))
    return zp.astype(np.float32), t.astype(np.float32), deltas.astype(np.float32)[None, :]


def _filter_kernel(z_ref, t_ref, dl_ref, w1_ref, b1_ref, f1_ref, w2_ref, b2_ref, f2_ref, w3_ref, b3_ref, o_ref):
    hp = lax.Precision.HIGHEST
    hid = jnp.sin(f1_ref[...] * (jnp.dot(z_ref[...], w1_ref[...], precision=hp, preferred_element_type=F32) + b1_ref[...]))
    hid = jnp.sin(f2_ref[...] * (jnp.dot(hid, w2_ref[...], precision=hp, preferred_element_type=F32) + b2_ref[...]))
    filt = jnp.dot(hid, w3_ref[...], precision=hp, preferred_element_type=F32) + b3_ref[...]
    decay = jnp.exp(-t_ref[...] * dl_ref[...])
    hf = filt[:, :D_MODEL] * decay
    hb = filt[:, D_MODEL:] * decay
    row = lax.broadcasted_iota(jnp.int32, (hb.shape[0], 1), 0)
    hb = jnp.where(row == 0, 0.0, hb)
    o_ref[:, :D_MODEL] = (hf + hb).astype(o_ref.dtype)
    o_ref[:, D_MODEL:] = (hf - hb).astype(o_ref.dtype)


def _filters(seq_len, w1, b1, f1, w2, b2, f2, w3, b3):
    z, t, deltas = _filter_features(seq_len)
    w1p = jnp.zeros((HY_FILTER_W, HY_FILTER_W), F32).at[:HY_EMB].set(w1)
    full = lambda shape: pl.BlockSpec(shape, lambda o: (0,) * len(shape))
    row = lambda a: a.reshape(1, -1)
    return pl.pallas_call(
        _filter_kernel,
        grid=(2,),
        in_specs=[full((seq_len, HY_FILTER_W)), full((seq_len, 1)), full((1, D_MODEL)),
                  full((HY_FILTER_W, HY_FILTER_W)), full((1, HY_FILTER_W)), full((1, HY_FILTER_W)),
                  full((HY_FILTER_W, HY_FILTER_W)), full((1, HY_FILTER_W)), full((1, HY_FILTER_W)),
                  pl.BlockSpec((HY_FILTER_W, 2 * D_MODEL), lambda o: (0, o)),
                  pl.BlockSpec((1, 2 * D_MODEL), lambda o: (0, o))],
        out_specs=pl.BlockSpec((None, seq_len, 2 * D_MODEL), lambda o: (o, 0, 0)),
        out_shape=jax.ShapeDtypeStruct((2, seq_len, 2 * D_MODEL), BF16),
        compiler_params=_cparams("arbitrary"),
        name="hyena_filters",
    )(jnp.asarray(z),jnp.asarray(t), jnp.asarray(deltas), w1p, row(b1), row(f1), w2, row(b2), row(f2), w3, row(b3))


def _spectrum_kernel(g_ref, hs_ref, hd_ref, o_ref, *, tq, scale):
    o_ref[:tq, :] = scale * _dot(g_ref[:tq, :], hs_ref[...])
    o_ref[tq:, :] = scale * _dot(g_ref[tq:, :], hd_ref[...])


def _spectrum(fwd, filt, seq_len, tq, tc):
    nf = seq_len // tq
    ncb = D_MODEL // tc
    return pl.pallas_call(
        functools.partial(_spectrum_kernel, tq=tq, scale=1.0 / seq_len),
        grid=(2, ncb, nf),
        in_specs=[pl.BlockSpec((None, 2 * tq, seq_len), lambda o, c, f: (f, 0, 0)),
                  pl.BlockSpec((None, seq_len, tc), lambda o, c, f: (o, 0, c)),
                  pl.BlockSpec((None, seq_len, tc), lambda o, c, f: (o, 0, c + ncb))],
        out_specs=pl.BlockSpec((None, None, 2 * tq, tc), lambda o, c, f: (o, f, 0, c)),
        out_shape=jax.ShapeDtypeStruct((2, nf, 2 * tq, D_MODEL), F32),
        compiler_params=_cparams("arbitrary", "arbitrary", "arbitrary"),
        name="hyena_spectrum",
    )(fwd, filt, filt)


def _hyena_kernel(v_ref, x1_ref, x2_ref, g_ref, gt_ref, k_ref, b_ref, o_ref, z_ref, acc_ref, *, tq, nf, seq_len, nseq):
    o = pl.program_id(2)
    f = pl.program_id(3)

    @pl.when((o == 0) & (f == 0))
    def _():
        z_ref[...] = v_ref[...].astype(BF16)

    kr, ki = k_ref[:tq, :], k_ref[tq:, :]
    for s in range(nseq):
        rows = slice(s * seq_len, (s + 1) * seq_len)
        zf = _dot(g_ref[...], z_ref[rows, :])
        zr, zi = zf[:tq], zf[tq:]
        prod = jnp.concatenate([zr * kr - zi * ki, zr * ki + zi * kr], axis=0).astype(BF16)
        part = _dot(gt_ref[...], prod)
        if nf == 1:
            acc_ref[rows, :] = part
        else:
            @pl.when(f == 0)
            def _():
                acc_ref[rows, :] = part

            @pl.when(f > 0)
            def _():
                acc_ref[rows, :] += part

    @pl.when(f == nf - 1)
    def _():
        y = acc_ref[...] + z_ref[...].astype(F32) * b_ref[...]

        @pl.when(o == 0)
        def _():
            z_ref[...] = (y * x1_ref[...].astype(F32)).astype(BF16)

        @pl.when(o == 1)
        def _():
            o_ref[...] = (y * x2_ref[...].astype(F32)).astype(o_ref.dtype)


def _hyena_core(u, fwd, bwd, spec, bias, batch, seq_len, tq, tc, nseq):
    nf = seq_len // tq
    ncb = D_MODEL // tc
    rows = nseq * seq_len
    uspec = lambda g: pl.BlockSpec((None, rows, tc), lambda b, c, o, f: (g, b, c))
    return pl.pallas_call(
        functools.partial(_hyena_kernel, tq=tq, nf=nf, seq_len=seq_len, nseq=nseq),
        grid=(batch // nseq, ncb, 2, nf),
        in_specs=[uspec(0), uspec(1), uspec(2),
                  pl.BlockSpec((None, 2 * tq, seq_len), lambda b, c, o, f: (f, 0, 0)),
                  pl.BlockSpec((None, seq_len, 2 * tq), lambda b, c, o, f: (f, 0, 0)),
                  pl.BlockSpec((None, None, 2 * tq, tc), lambda b, c, o, f: (o, f, 0, c)),
                  pl.BlockSpec((None, 1, tc), lambda b, c, o, f: (o, 0, c))],
        out_specs=pl.BlockSpec((rows, tc), lambda b, c, o, f: (b, c)),
        out_shape=jax.ShapeDtypeStruct((batch * seq_len, D_MODEL), BF16),
        scratch_shapes=[pltpu.VMEM((rows, tc), BF16), pltpu.VMEM((rows, tc), F32)],
        compiler_params=_cparams("arbitrary", "arbitrary", "arbitrary", "arbitrary"),
        name="hyena_conv",
    )(u, u, u, fwd, bwd, spec, bias.reshape(2, 1, D_MODEL))


def _hyena_tiles(seq_len, batch):
    tq = min(seq_len, 256)
    tc = D_MODEL if seq_len <= 256 else 512
    nseq = math.gcd(batch, max(1, 1024 // seq_len))
    return tq, tc, nseq


def kernel(x_prompt, x_sample, cache_k, cache_v, c, c_ctx, mod_w, mod_b, ln_g, ln_b, ffn_w_in, ffn_w_out,
           attn_w_qkv, attn_w_o, attn_rpb, hy_w_in, hy_conv_w, hy_conv_b, hy_f_w1, hy_f_b1, hy_f_freq1,
           hy_f_w2, hy_f_b2, hy_f_freq2, hy_f_w3, hy_f_b3, hy_bias, hy_w_out):
    batch, seq, _ = x_prompt.shape
    dec_batch, dec_seq, _ = x_sample.shape
    assert dec_batch + 1 <= COND_ROWS

    conds = jnp.zeros((COND_ROWS, D_MODEL), F32).at[0].set(c_ctx).at[1:1 + dec_batch].set(c)
    mods = _mods(conds, mod_w, mod_b).reshape(DEPTH, COND_ROWS, N_MOD, D_MODEL)
    ln_g4 = ln_g.reshape(DEPTH, 3, 1, D_MODEL)
    ln_b4 = ln_b.reshape(DEPTH, 3, 1, D_MODEL)

    streams = {
        "ctx": dict(x=x_prompt.reshape(batch * seq, D_MODEL), batch=batch, seq=seq, crow=lambda t: 0),
        "lat": dict(x=x_sample.reshape(dec_batch * dec_seq, D_MODEL), batch=dec_batch, seq=dec_seq,
                    crow=lambda t: 1 + t // dec_seq),
    }
    hyena_consts = {}
    for name, st in streams.items():
        sl = st["seq"]
        tq, tc, nseq = _hyena_tiles(sl, st["batch"])
        fwd, bwd = _dft_tables(sl, tq)
        hyena_consts[name] = (jnp.asarray(fwd).astype(BF16), jnp.asarray(bwd).astype(BF16), tq, tc, nseq)

    ffn_w_in, ffn_w_out, attn_w_qkv, attn_w_o, hy_w_in, hy_w_out = (
        w.astype(BF16) for w in (ffn_w_in, ffn_w_out, attn_w_qkv, attn_w_o, hy_w_in, hy_w_out))

    new_k, new_v = [], []
    outs = {}
    for name, st in streams.items():
        x, nb, sl, crow = st["x"], st["batch"], st["seq"], st["crow"]
        for layer in range(DEPTH):
            j = layer // 2
            x = _ffn(x, mods, crow, ffn_w_in, ffn_w_out, ln_g4, ln_b4, layer, 0)
            if layer % 2 == 0:
                if name == "ctx":
                    qkv = _proj(x, mods, crow, attn_w_qkv, layer, j, tm=1024, tn=1024, out_dtype=F32)
                    a, k_new, v_new = _attn_ctx(qkv, nb, sl)
                    new_k.append(k_new)
                    new_v.append(v_new)
                else:
                    qkv = _proj(x, mods, crow, attn_w_qkv, layer, j, tm=1024, tn=1024, out_dtype=BF16)
                    kc = cache_k[:, j].reshape(nb, -1, D_MODEL)
                    vc = cache_v[:, j].reshape(nb, -1, D_MODEL)
                    a = _attn_nbr(qkv, kc, vc, attn_rpb[j], nb, sl)
                x = _outproj(x, a, mods, crow, attn_w_o, j, ln_g4, ln_b4, layer)
            else:
                fwd, bwd, tq, tc, nseq = hyena_consts[name]
                filt = _filters(sl, hy_f_w1[j], hy_f_b1[j], hy_f_freq1[j], hy_f_w2[j], hy_f_b2[j], hy_f_freq2[j],
                                hy_f_w3[j], hy_f_b3[j])
                spec = _spectrum(fwd, filt, sl, tq, tc)
                tm = max(sl, 1024)
                u = _proj(x, mods, crow, hy_w_in, layer, j, tm=tm, tn=512, out_dtype=BF16,
                          conv=(hy_conv_w, hy_conv_b), seq_len=sl)
                a = _hyena_core(u, fwd, bwd, spec, hy_bias[j], nb, sl, tq, tc, nseq)
                x = _outproj(x, a, mods, crow, hy_w_out, j, ln_g4, ln_b4, layer)
            x = _ffn(x, mods, crow, ffn_w_in, ffn_w_out, ln_g4, ln_b4, layer, 1)
        outs[name] = x

    y_prompt = outs["ctx"].reshape(batch, seq, D_MODEL)
    y_sample = outs["lat"].reshape(dec_batch, dec_seq, D_MODEL)
    return (y_prompt, y_sample, jnp.concatenate(new_k, axis=1), jnp.concatenate(new_v, axis=1))
```

```python
import functools
import math

import numpy as np
import jax
import jax.numpy as jnp
from jax import lax
from jax.experimental import pallas as pl
from jax.experimental.pallas import tpu as pltpu

D_MODEL = 1024
DEPTH = 2
GRID_W = 64
N_HEADS = 16
HEAD_DIM = D_MODEL // N_HEADS
WIN_ROWS = 8
WIN_COLS = 16
D_FF = 2816
N_MOD = 9
HY_BANDS = 16
HY_EMB = 1 + 2 * HY_BANDS
HY_FILTER_W = 64
HY_FAST_DECAY = 0.3
HY_SLOW_DECAY = 1.5
HY_TARGET = 1e-2
ALPHA = (2 * DEPTH) ** 0.25
LN_EPS = 1e-5

F32 = jnp.float32
BF16 = jnp.bfloat16
NEG = -1e30
LANES = 128
VMEM_LIMIT = 56 * 1024 * 1024
COND_ROWS = 8

Q_ROWS = 8
K_ROWS = 16
BAND_ROWS = 10


def _cparams(*sem):
    return pltpu.CompilerParams(dimension_semantics=sem, vmem_limit_bytes=VMEM_LIMIT)


def _dot(a, b):
    return jnp.dot(a, b, preferred_element_type=F32)


def _dot_nt(a, b):
    return lax.dot_general(a, b, (((1,), (1,)), ((), ())), preferred_element_type=F32)


def _layer_norm(y, g, b):
    mu = jnp.mean(y, axis=-1, keepdims=True)
    yc = y - mu
    var = jnp.mean(yc * yc, axis=-1, keepdims=True)
    return yc * lax.rsqrt(var + LN_EPS) * g + b


def _mods_kernel(c_ref, w_ref, b_ref, o_ref):
    c = c_ref[...]
    s = c * jax.nn.sigmoid(c)
    o_ref[...] = _dot(s.astype(BF16), w_ref[...].astype(BF16)) + b_ref[...]


def _mods(conds, mod_w, mod_b):
    tn = 1024
    nd = N_MOD * D_MODEL
    return pl.pallas_call(
        _mods_kernel,
        grid=(DEPTH, nd // tn),
        in_specs=[
            pl.BlockSpec((COND_ROWS, D_MODEL), lambda l, j: (0, 0)),
            pl.BlockSpec((None, D_MODEL, tn), lambda l, j: (l, 0, j)),
            pl.BlockSpec((None, 1, tn), lambda l, j: (l, 0, j)),
        ],
        out_specs=pl.BlockSpec((None, COND_ROWS, tn), lambda l, j: (l, 0, j)),
        out_shape=jax.ShapeDtypeStruct((DEPTH, COND_ROWS, nd), F32),
        compiler_params=_cparams("arbitrary", "arbitrary"),
        name="mods",
    )(conds, mod_w, mod_b.reshape(DEPTH, 1, nd))


def _ffn_kernel(x_ref, mod_ref, wi_ref, wo_ref, g_ref, b_ref, o_ref, *, mi, n_chunks):
    x = x_ref[...]
    sh = mod_ref[3 * mi:3 * mi + 1, :]
    sc = mod_ref[3 * mi + 1:3 * mi + 2, :]
    gt = mod_ref[3 * mi + 2:3 * mi + 3, :]
    h = (x * (1.0 + sc) + sh).astype(BF16)
    tf = D_FF // n_chunks
    f = None
    for c in range(n_chunks):
        gate = _dot(h, wi_ref[:, c * tf:(c + 1) * tf])
        up = _dot(h, wi_ref[:, D_FF + c * tf:D_FF + (c + 1) * tf])
        a = (gate * jax.nn.sigmoid(gate) * up).astype(BF16)
        part = _dot(a, wo_ref[c * tf:(c + 1) * tf, :])
        f = part if f is None else f + part
    y = ALPHA * x + (0.5 * gt) * f
    o_ref[...] = _layer_norm(y, g_ref[...], b_ref[...])


def _resident(shape, index_map):
    return pl.BlockSpec(shape, index_map, pipeline_mode=pl.Buffered(1))


def _ffn(x, mods, crow, w_in, w_out, ln_g, ln_b, layer, which, tm=1024, n_chunks=11):
    m = x.shape[0]
    mi = 0 if which == 0 else 2
    return pl.pallas_call(
        functools.partial(_ffn_kernel, mi=mi, n_chunks=n_chunks),
        grid=(m // tm,),
        in_specs=[
            pl.BlockSpec((tm, D_MODEL), lambda i: (i, 0)),
            pl.BlockSpec((None, None, N_MOD, D_MODEL), lambda i: (layer, crow(i * tm), 0, 0)),
            _resident((None, None, D_MODEL, 2 * D_FF), lambda i: (layer, which, 0, 0)),
            _resident((None, None, D_FF, D_MODEL), lambda i: (layer, which, 0, 0)),
            pl.BlockSpec((None, None, 1, D_MODEL), lambda i: (layer, mi, 0, 0)),
            pl.BlockSpec((None, None, 1, D_MODEL), lambda i: (layer, mi, 0, 0)),
        ],
        out_specs=pl.BlockSpec((tm, D_MODEL), lambda i: (i, 0)),
        out_shape=jax.ShapeDtypeStruct((m, D_MODEL), F32),
        compiler_params=_cparams("arbitrary"),
        name="ffn",
    )(x, mods, w_in, w_out, ln_g, ln_b)


def _proj_kernel(x_ref, mod_ref, w_ref, *rest, seq_len):
    if seq_len:
        cw_ref, cb_ref, o_ref, h_ref = rest
    else:
        o_ref, h_ref = rest
    j = pl.program_id(1)

    @pl.when(j == 0)
    def _():
        sh = mod_ref[3:4, :]
        sc = mod_ref[4:5, :]
        h_ref[...] = (x_ref[...] * (1.0 + sc) + sh).astype(BF16)

    u = _dot(h_ref[...], w_ref[...])
    if seq_len:
        tm = u.shape[0]
        pos = lax.broadcasted_iota(jnp.int32, (tm, 1), 0) % seq_len
        prev = jnp.where(pos == 0, 0.0, pltpu.roll(u, 1, 0))
        nxt = jnp.where(pos == seq_len - 1, 0.0, pltpu.roll(u, tm - 1, 0))
        u = prev * cw_ref[0:1, :] + u * cw_ref[1:2, :] + nxt * cw_ref[2:3, :] + cb_ref[...]
    o_ref[...] = u.astype(o_ref.dtype)


def _proj(x, mods, crow, w, layer, widx, *, tm, tn, out_dtype, conv=None, seq_len=0):
    m = x.shape[0]
    npg = D_MODEL // tn
    in_specs = [
        pl.BlockSpec((tm, D_MODEL), lambda i, j: (i, 0)),
        pl.BlockSpec((None, None, N_MOD, D_MODEL), lambda i, j: (layer, crow(i * tm), 0, 0)),
        pl.BlockSpec((None, D_MODEL, tn), lambda i, j: (widx, 0, j)),
    ]
    args = [x, mods, w]
    if conv is not None:
        cw, cb = conv
        in_specs += [
            pl.BlockSpec((None, 3, tn), lambda i, j: (widx, 0, j)),
            pl.BlockSpec((None, 1, tn), lambda i, j: (widx, 0, j)),
        ]
        args += [cw, cb.reshape(cb.shape[0], 1, cb.shape[1])]
    return pl.pallas_call(
        functools.partial(_proj_kernel, seq_len=seq_len),
        grid=(m // tm, 3 * npg),
        in_specs=in_specs,
        out_specs=pl.BlockSpec((None, tm, tn), lambda i, j: (j // npg, i, j % npg)),
        out_shape=jax.ShapeDtypeStruct((3, m, D_MODEL), out_dtype),
        scratch_shapes=[pltpu.VMEM((tm, D_MODEL), BF16)],
        compiler_params=_cparams("arbitrary", "arbitrary"),
        name="proj_conv" if seq_len else "proj",
    )(*args)


def _outproj_kernel(x_ref, a_ref, mod_ref, w_ref, g_ref, b_ref, o_ref):
    mix = _dot(a_ref[...], w_ref[...])
    y = ALPHA * x_ref[...] + mod_ref[5:6, :] * mix
    o_ref[...] = _layer_norm(y, g_ref[...], b_ref[...])


def _outproj(x, a, mods, crow, w, widx, ln_g, ln_b, layer, tm=1024):
    m = x.shape[0]
    return pl.pallas_call(
        _outproj_kernel,
        grid=(m // tm,),
        in_specs=[
            pl.BlockSpec((tm, D_MODEL), lambda i: (i, 0)),
            pl.BlockSpec((tm, D_MODEL), lambda i: (i, 0)),
            pl.BlockSpec((None, None, N_MOD, D_MODEL), lambda i: (layer, crow(i * tm), 0, 0)),
            pl.BlockSpec((None, D_MODEL, D_MODEL), lambda i: (widx, 0, 0)),
            pl.BlockSpec((None, None, 1, D_MODEL), lambda i: (layer, 1, 0, 0)),
            pl.BlockSpec((None, None, 1, D_MODEL), lambda i: (layer, 1, 0, 0)),
        ],
        out_specs=pl.BlockSpec((tm, D_MODEL), lambda i: (i, 0)),
        out_shape=jax.ShapeDtypeStruct((m, D_MODEL), F32),
        compiler_params=_cparams("arbitrary"),
        name="outproj",
    )(x, a, mods, w, ln_g, ln_b)


def _head_masks():
    lane = lax.broadcasted_iota(jnp.int32, (1, LANES), 1)
    return lane < HEAD_DIM, lane >= HEAD_DIM


def _attn_ctx_kernel(x_ref, mod_ref, wqkv_ref, wo_ref, g_ref, b_ref, o_ref, ko_ref, vo_ref,
                     q_scr, k_scr, v_scr, a_scr, *, seq, nseq):
    x = x_ref[...]
    h = (x * (1.0 + mod_ref[4:5, :]) + mod_ref[3:4, :]).astype(BF16)
    q_scr[...] = (_dot(h, wqkv_ref[:, :D_MODEL]) * HEAD_DIM ** -0.5).astype(BF16)
    k_scr[...] = _dot(h, wqkv_ref[:, D_MODEL:2 * D_MODEL])
    v_scr[...] = _dot(h, wqkv_ref[:, 2 * D_MODEL:])
    masks = _head_masks()
    for s in range(nseq):
        rows = slice(s * seq, (s + 1) * seq)
        ko_ref[s] = k_scr[rows, :].reshape(seq, N_HEADS, HEAD_DIM)
        vo_ref[s] = v_scr[rows, :].reshape(seq, N_HEADS, HEAD_DIM)
        for p in range(D_MODEL // LANES):
            sl = slice(p * LANES, (p + 1) * LANES)
            q2 = q_scr[rows, sl]
            k2 = k_scr[rows, sl]
            v2 = v_scr[rows, sl]
            o2 = jnp.zeros(q2.shape, F32)
            for hm in masks:
                kh = jnp.where(hm, k2, 0.0).astype(BF16)
                vh = jnp.where(hm, v2, 0.0).astype(BF16)
                sc = _dot_nt(q2, kh)
                e = jnp.exp(sc - jnp.max(sc, axis=-1, keepdims=True))
                inv = 1.0 / jnp.sum(e, axis=-1, keepdims=True)
                o2 = o2 + _dot(e.astype(BF16), vh) * inv
            a_scr[rows, sl] = o2.astype(BF16)
    y = ALPHA * x + mod_ref[5:6, :] * _dot(a_scr[...], wo_ref[...])
    o_ref[...] = _layer_norm(y, g_ref[...], b_ref[...])


def _attn_ctx(x, mods, crow, w_qkv, w_o, widx, ln_g, ln_b, layer, batch, seq, nseq=2):
    rows = nseq * seq
    cache_spec = pl.BlockSpec((nseq, None, seq, N_HEADS, HEAD_DIM), lambda i: (i, 0, 0, 0, 0))
    cache_shape = jax.ShapeDtypeStruct((batch, 1, seq, N_HEADS, HEAD_DIM), F32)
    return pl.pallas_call(
        functools.partial(_attn_ctx_kernel, seq=seq, nseq=nseq),
        grid=(batch // nseq,),
        in_specs=[
            pl.BlockSpec((rows, D_MODEL), lambda i: (i, 0)),
            pl.BlockSpec((None, None, N_MOD, D_MODEL), lambda i: (layer, crow(i * rows), 0, 0)),
            _resident((None, D_MODEL, 3 * D_MODEL), lambda i: (widx, 0, 0)),
            _resident((None, D_MODEL, D_MODEL), lambda i: (widx, 0, 0)),
            pl.BlockSpec((None, None, 1, D_MODEL), lambda i: (layer, 1, 0, 0)),
            pl.BlockSpec((None, None, 1, D_MODEL), lambda i: (layer, 1, 0, 0)),
        ],
        out_specs=[pl.BlockSpec((rows, D_MODEL), lambda i: (i, 0)), cache_spec, cache_spec],
        out_shape=[jax.ShapeDtypeStruct((batch * seq, D_MODEL), F32), cache_shape, cache_shape],
        scratch_shapes=[pltpu.VMEM((rows, D_MODEL), BF16), pltpu.VMEM((rows, D_MODEL), F32),
                        pltpu.VMEM((rows, D_MODEL), F32), pltpu.VMEM((rows, D_MODEL), BF16)],
        compiler_params=_cparams("arbitrary"),
        name="attn_ctx",
    )(x, mods, w_qkv, w_o, ln_g, ln_b)


def _nbr_geometry(rows):
    kr = min(WIN_ROWS, rows)
    cases, blocks = {}, []
    for blk in range(rows // Q_ROWS):
        r0 = blk * Q_ROWS
        lo = min(max(r0 - kr // 2, 0), rows - kr)
        key_start = min(lo, rows - K_ROWS)
        bands = []
        for r in range(r0, r0 + Q_ROWS):
            rs = min(max(r - kr // 2, 0), rows - kr)
            a = rs - key_start
            band_start = min(a - a % 2, K_ROWS - BAND_ROWS)
            off = a - band_start
            assert 0 <= band_start and 0 <= off and off + kr <= BAND_ROWS
            key = (rs - r + WIN_ROWS - 1, off)
            bands.append((band_start, cases.setdefault(key, len(cases))))
        blocks.append((key_start, bands))
    return kr, list(cases), blocks


def _nbr_bias_slabs(rpb, kr, cases):
    qc = np.arange(GRID_W)[:, None]
    kc = np.arange(GRID_W)[None, :]
    cstart = np.clip(qc - WIN_COLS // 2, 0, GRID_W - WIN_COLS)
    col_ok = (kc >= cstart) & (kc < cstart + WIN_COLS)
    n_d = 2 * WIN_ROWS - 1
    lead = GRID_W - WIN_COLS
    ext = jnp.pad(rpb, ((0, 0), (0, 0), (lead, 2 * GRID_W - lead - (2 * WIN_COLS - 1))))
    flat = jnp.tile(ext, (1, 1, GRID_W))[:, :, GRID_W - 1:GRID_W - 1 + GRID_W * (2 * GRID_W - 1)]
    toep = flat.reshape(N_HEADS, n_d, GRID_W, 2 * GRID_W - 1)[:, :, :, :GRID_W]
    toep = jnp.where(jnp.asarray(col_ok)[None, None], toep, NEG)
    masked = jnp.full((N_HEADS, GRID_W, GRID_W), NEG, F32)
    slabs = []
    for s, off in cases:
        parts = [toep[:, s - off + j] if off <= j < off + kr else masked for j in range(BAND_ROWS)]
        slabs.append(jnp.concatenate(parts, axis=-1))
    return jnp.stack(slabs, axis=1)


def _attn_nbr_kernel(q_ref, k_ref, v_ref, kc_ref, vc_ref, bias_ref, o_ref,
                     s_ref, sc_ref, p_ref, pc_ref, inv_ref, *, blocks, n_cases):
    masks = _head_masks()
    qn = Q_ROWS * GRID_W
    kn = K_ROWS * GRID_W
    bw = BAND_ROWS * GRID_W
    for blk, (key_start, bands) in enumerate(blocks):
        q2 = (q_ref[blk * qn:(blk + 1) * qn, :] * HEAD_DIM ** -0.5).astype(BF16)
        k2 = k_ref[key_start * GRID_W:key_start * GRID_W + kn, :]
        v2 = v_ref[key_start * GRID_W:key_start * GRID_W + kn, :]
        o2 = jnp.zeros(q2.shape, F32)
        p_ref[...] = jnp.zeros_like(p_ref)
        for hh, hm in enumerate(masks):
            kh = jnp.where(hm, k2, 0.0).astype(BF16)
            vh = jnp.where(hm, v2, 0.0).astype(BF16)
            kch = jnp.where(hm, kc_ref[...], 0.0).astype(BF16)
            vch = jnp.where(hm, vc_ref[...], 0.0).astype(BF16)
            s_ref[...] = _dot_nt(q2, kh)
            sc_ref[...] = _dot_nt(q2, kch)
            for rl, (band_start, case) in enumerate(bands):
                rsl = slice(rl * GRID_W, (rl + 1) * GRID_W)
                csl = slice(band_start * GRID_W, band_start * GRID_W + bw)
                sb = s_ref[rsl, csl] + bias_ref[hh * n_cases + case]
                sc = sc_ref[rsl, :]
                mx = jnp.maximum(jnp.max(sb, axis=-1, keepdims=True), jnp.max(sc, axis=-1, keepdims=True))
                eb = jnp.exp(sb - mx)
                ec = jnp.exp(sc - mx)
                inv_ref[rsl, :] = 1.0 / (jnp.sum(eb, axis=-1, keepdims=True) + jnp.sum(ec, axis=-1, keepdims=True))
                p_ref[rsl, csl] = eb.astype(BF16)
                pc_ref[rsl, :] = ec.astype(BF16)
            o2 = o2 + (_dot(p_ref[...], vh) + _dot(pc_ref[...], vch)) * inv_ref[...]
        o_ref[blk * qn:(blk + 1) * qn, :] = o2.astype(o_ref.dtype)


def _attn_nbr(qkv, kc, vc, rpb, batch, seq):
    rows = seq // GRID_W
    kr, cases, blocks = _nbr_geometry(rows)
    n_cases = len(cases)
    n_pairs = D_MODEL // LANES
    bias = _nbr_bias_slabs(rpb, kr, cases).reshape(n_pairs, 2 * n_cases, GRID_W, BAND_ROWS * GRID_W)
    s_ctx = kc.shape[1]
    qspec = lambda g: pl.BlockSpec((None, seq, LANES), lambda b, p: (g, b, p))
    cspec = pl.BlockSpec((None, s_ctx, LANES), lambda b, p: (b, 0, p))
    qn, kn = Q_ROWS * GRID_W, K_ROWS * GRID_W
    return pl.pallas_call(
        functools.partial(_attn_nbr_kernel, blocks=blocks, n_cases=n_cases),
        grid=(batch, n_pairs),
        in_specs=[qspec(0), qspec(1), qspec(2), cspec, cspec,
                  pl.BlockSpec((None, 2 * n_cases, GRID_W, BAND_ROWS * GRID_W), lambda b, p: (p, 0, 0, 0))],
        out_specs=pl.BlockSpec((seq, LANES), lambda b, p: (b, p)),
        out_shape=jax.ShapeDtypeStruct((batch * seq, D_MODEL), BF16),
        scratch_shapes=[pltpu.VMEM((qn, kn), F32), pltpu.VMEM((qn, s_ctx), F32),
                        pltpu.VMEM((qn, kn), BF16), pltpu.VMEM((qn, s_ctx), BF16),
                        pltpu.VMEM((qn, 1), F32)],
        compiler_params=_cparams("arbitrary", "arbitrary"),
        name="attn_nbr",
    )(qkv, qkv, qkv, kc, vc, bias)


def _dft_tables(seq_len, tq, tr):
    n = 2 * seq_len
    f = np.arange(seq_len, dtype=np.float64)[:, None] + 0.5
    s = np.arange(seq_len, dtype=np.float64)[None, :]
    ang = 2.0 * np.pi * f * s / n
    re = np.cos(ang).reshape(seq_len // tq, tq, seq_len)
    im = (-np.sin(ang)).reshape(seq_len // tq, tq, seq_len)
    fwd = np.concatenate([re, im], axis=1).astype(np.float32)
    bwd = fwd.reshape(2 * seq_len, seq_len).T
    return fwd, np.ascontiguousarray(bwd).reshape(seq_len // tr, tr, 2 * seq_len)


def _filter_features(seq_len):
    pos = np.arange(seq_len, dtype=np.float64)[:, None]
    t = pos / max(seq_len - 1, 1)
    bands = np.linspace(1e-4, HY_BANDS - 1, HY_BANDS)
    ang = 2.0 * np.pi * pos / seq_len * bands
    z = np.concatenate([t, np.cos(ang), -np.sin(ang)], axis=-1)
    zp = np.zeros((seq_len, HY_FILTER_W), np.float64)
    zp[:, :HY_EMB] = z
    deltas = np.abs(np.linspace(math.log(HY_TARGET) / HY_SLOW_DECAY, math.log(HY_TARGET) / HY_FAST_DECAY, D_MODEL))
    return zp.astype(np.float32), t.astype(np.float32), deltas.astype(np.float32)[None, :]


def _filter_kernel(z_ref, t_ref, dl_ref, w1_ref, b1_ref, f1_ref, w2_ref, b2_ref, f2_ref, w3_ref, b3_ref, o_ref,
                   hid_ref, decay_ref):
    hp = lax.Precision.HIGHEST

    @pl.when(pl.program_id(0) == 0)
    def _():
        hid = jnp.sin(f1_ref[...] * (jnp.dot(z_ref[...], w1_ref[...], precision=hp, preferred_element_type=F32) + b1_ref[...]))
        hid_ref[...] = jnp.sin(f2_ref[...] * (jnp.dot(hid, w2_ref[...], precision=hp, preferred_element_type=F32) + b2_ref[...]))
        decay_ref[...] = jnp.exp(-t_ref[...] * dl_ref[...])

    filt = jnp.dot(hid_ref[...], w3_ref[...], precision=hp, preferred_element_type=F32) + b3_ref[...]
    decay = decay_ref[...]
    hf = filt[:, :D_MODEL] * decay
    hb = filt[:, D_MODEL:] * decay
    row = lax.broadcasted_iota(jnp.int32, (hb.shape[0], 1), 0)
    hb = jnp.where(row == 0, 0.0, hb)
    o_ref[:, :D_MODEL] = (hf + hb).astype(o_ref.dtype)
    o_ref[:, D_MODEL:] = (hf - hb).astype(o_ref.dtype)


def _filters(seq_len, w1, b1, f1, w2, b2, f2, w3, b3):
    z, t, deltas = _filter_features(seq_len)
    w1p = jnp.zeros((HY_FILTER_W, HY_FILTER_W), F32).at[:HY_EMB].set(w1)
    full = lambda shape: pl.BlockSpec(shape, lambda o: (0,) * len(shape))
    row = lambda a: a.reshape(1, -1)
    return pl.pallas_call(
        _filter_kernel,
        grid=(2,),
        in_specs=[full((seq_len, HY_FILTER_W)), full((seq_len, 1)), full((1, D_MODEL)),
                  full((HY_FILTER_W, HY_FILTER_W)), full((1, HY_FILTER_W)), full((1, HY_FILTER_W)),
                  full((HY_FILTER_W, HY_FILTER_W)), full((1, HY_FILTER_W)), full((1, HY_FILTER_W)),
                  pl.BlockSpec((HY_FILTER_W, 2 * D_MODEL), lambda o: (0, o)),
                  pl.BlockSpec((1, 2 * D_MODEL), lambda o: (0, o))],
        out_specs=pl.BlockSpec((None, seq_len, 2 * D_MODEL), lambda o: (o, 0, 0)),
        out_shape=jax.ShapeDtypeStruct((2, seq_len, 2 * D_MODEL), BF16),
        scratch_shapes=[pltpu.VMEM((seq_len, HY_FILTER_W), F32), pltpu.VMEM((seq_len, D_MODEL), F32)],
        compiler_params=_cparams("arbitrary"),
        name="hyena_filters",
    )(jnp.asarray(z),jnp.asarray(t), jnp.asarray(deltas), w1p, row(b1), row(f1), w2, row(b2), row(f2), w3, row(b3))


def _spectrum_kernel(g_ref, hs_ref, hd_ref, o_ref, *, tq, scale):
    o_ref[:tq, :] = scale * _dot(g_ref[:tq, :], hs_ref[...])
    o_ref[tq:, :] = scale * _dot(g_ref[tq:, :], hd_ref[...])


def _spectrum(fwd, filt, seq_len, tq, tc):
    nf = seq_len // tq
    ncb = D_MODEL // tc
    return pl.pallas_call(
        functools.partial(_spectrum_kernel, tq=tq, scale=1.0 / seq_len),
        grid=(2, ncb, nf),
        in_specs=[pl.BlockSpec((None, 2 * tq, seq_len), lambda o, c, f: (f, 0, 0)),
                  pl.BlockSpec((None, seq_len, tc), lambda o, c, f: (o, 0, c)),
                  pl.BlockSpec((None, seq_len, tc), lambda o, c, f: (o, 0, c + ncb))],
        out_specs=pl.BlockSpec((None, None, 2 * tq, tc), lambda o, c, f: (o, f, 0, c)),
        out_shape=jax.ShapeDtypeStruct((2, nf, 2 * tq, D_MODEL), F32),
        compiler_params=_cparams("arbitrary", "arbitrary", "arbitrary"),
        name="hyena_spectrum",
    )(fwd, filt, filt)


def _hyena_kernel(v_ref, gate_ref, g_ref, gt_ref, k_ref, b_ref, o_ref, z_ref, p_ref, *, tq, nf, tr):
    o = pl.program_id(2)
    st = pl.program_id(3)

    @pl.when((o == 0) & (st == 0))
    def _():
        z_ref[...] = v_ref[...].astype(BF16)

    @pl.when(st < nf)
    def _():
        zf = _dot(g_ref[...], z_ref[...])
        zr, zi = zf[:tq], zf[tq:]
        kr, ki = k_ref[:tq, :], k_ref[tq:, :]
        prod = jnp.concatenate([zr * kr - zi * ki, zr * ki + zi * kr], axis=0)
        p_ref[pl.ds(pl.multiple_of(st * 2 * tq, 2 * tq), 2 * tq), :] = prod.astype(BF16)

    @pl.when(st >= nf)
    def _():
        rows = pl.ds(pl.multiple_of((st - nf) * tr, tr), tr)
        y = _dot(gt_ref[...], p_ref[...]) + z_ref[rows, :].astype(F32) * b_ref[...]
        gated = y * gate_ref[rows, :].astype(F32)

        @pl.when(o == 0)
        def _():
            z_ref[rows, :] = gated.astype(BF16)

        @pl.when(o == 1)
        def _():
            o_ref[rows, :] = gated.astype(o_ref.dtype)


def _hyena_core(u, fwd, bwd_rows, spec, bias, batch, seq_len, tq, tc):
    nf = seq_len // tq
    nr, tr, _ = bwd_rows.shape
    ncb = D_MODEL // tc
    fidx = lambda st: jnp.minimum(st, nf - 1)
    return pl.pallas_call(
        functools.partial(_hyena_kernel, tq=tq, nf=nf, tr=tr),
        grid=(batch, ncb, 2, nf + nr),
        in_specs=[pl.BlockSpec((None, seq_len, tc), lambda b, c, o, st: (0, b, c)),
                  pl.BlockSpec((None, seq_len, tc), lambda b, c, o, st: (1 + o, b, c)),
                  pl.BlockSpec((None, 2 * tq, seq_len), lambda b, c, o, st: (fidx(st), 0, 0)),
                  pl.BlockSpec((None, tr, 2 * seq_len), lambda b, c, o, st: (jnp.maximum(st - nf, 0), 0, 0)),
                  pl.BlockSpec((None, None, 2 * tq, tc), lambda b, c, o, st: (o, fidx(st), 0, c)),
                  pl.BlockSpec((None, 1, tc), lambda b, c, o, st: (o, 0, c))],
        out_specs=pl.BlockSpec((seq_len, tc), lambda b, c, o, st: (b, c)),
        out_shape=jax.ShapeDtypeStruct((batch * seq_len, D_MODEL), BF16),
        scratch_shapes=[pltpu.VMEM((seq_len, tc), BF16), pltpu.VMEM((2 * seq_len, tc), BF16)],
        compiler_params=_cparams("arbitrary", "arbitrary", "arbitrary", "arbitrary"),
        name="hyena_conv",
    )(u, u, fwd, bwd_rows, spec, bias.reshape(2, 1, D_MODEL))


def _hyena_layer_kernel(x_ref, mod_ref, win_ref, cw_ref, cb_ref, g_ref, gt_ref, k_ref, hb_ref, wout_ref,
                        lg_ref, lb_ref, o_ref, u_scr, *, seq_len, nseq, tn):
    x = x_ref[...]
    h = (x * (1.0 + mod_ref[4:5, :]) + mod_ref[3:4, :]).astype(BF16)
    rows = nseq * seq_len
    pos = lax.broadcasted_iota(jnp.int32, (rows, 1), 0) % seq_len
    first, last = pos == 0, pos == seq_len - 1
    for g in range(3):
        for cc in range(D_MODEL // tn):
            cols = slice(g * D_MODEL + cc * tn, g * D_MODEL + (cc + 1) * tn)
            u = _dot(h, win_ref[:, cols])
            prev = jnp.where(first, 0.0, pltpu.roll(u, 1, 0))
            nxt = jnp.where(last, 0.0, pltpu.roll(u, rows - 1, 0))
            u = prev * cw_ref[0:1, cols] + u * cw_ref[1:2, cols] + nxt * cw_ref[2:3, cols] + cb_ref[:, cols]
            u_scr[g, :, cc * tn:(cc + 1) * tn] = u.astype(BF16)
    for o in range(2):
        kr, ki = k_ref[o, :seq_len, :], k_ref[o, seq_len:, :]
        for s in range(nseq):
            r = slice(s * seq_len, (s + 1) * seq_len)
            z = u_scr[0, r, :]
            zf = _dot(g_ref[...], z)
            zr, zi = zf[:seq_len], zf[seq_len:]
            prod = jnp.concatenate([zr * kr - zi * ki, zr * ki + zi * kr], axis=0).astype(BF16)
            y = _dot(gt_ref[...], prod) + z.astype(F32) * hb_ref[o]
            u_scr[0, r, :] = (y * u_scr[1 + o, r, :].astype(F32)).astype(BF16)
    y = ALPHA * x + mod_ref[5:6, :] * _dot(u_scr[0], wout_ref[...])
    o_ref[...] = _layer_norm(y, lg_ref[...], lb_ref[...])


def _hyena_layer(x, mods, crow, w_in, w_out, widx, conv_w, conv_b, fwd, bwd, spec, bias, ln_g, ln_b, layer,
                 seq_len, nseq, tn=512):
    m = x.shape[0]
    rows = nseq * seq_len
    assert fwd.shape[0] == 1
    return pl.pallas_call(
        functools.partial(_hyena_layer_kernel, seq_len=seq_len, nseq=nseq, tn=tn),
        grid=(m // rows,),
        in_specs=[
            pl.BlockSpec((rows, D_MODEL), lambda i: (i, 0)),
            pl.BlockSpec((None, None, N_MOD, D_MODEL), lambda i: (layer, crow(i * rows), 0, 0)),
            _resident((None, D_MODEL, 3 * D_MODEL), lambda i: (widx, 0, 0)),
            pl.BlockSpec((None, 3, 3 * D_MODEL), lambda i: (widx, 0, 0)),
            pl.BlockSpec((None, 1, 3 * D_MODEL), lambda i: (widx, 0, 0)),
            pl.BlockSpec((None, 2 * seq_len, seq_len), lambda i: (0, 0, 0)),
            pl.BlockSpec((None, seq_len, 2 * seq_len), lambda i: (0, 0, 0)),
            _resident((2, None, 2 * seq_len, D_MODEL), lambda i: (0, 0, 0, 0)),
            pl.BlockSpec((2, 1, D_MODEL), lambda i: (0, 0, 0)),
            _resident((None, D_MODEL, D_MODEL), lambda i: (widx, 0, 0)),
            pl.BlockSpec((None, None, 1, D_MODEL), lambda i: (layer, 1, 0, 0)),
            pl.BlockSpec((None, None, 1, D_MODEL), lambda i: (layer, 1, 0, 0)),
        ],
        out_specs=pl.BlockSpec((rows, D_MODEL), lambda i: (i, 0)),
        out_shape=jax.ShapeDtypeStruct((m, D_MODEL), F32),
        scratch_shapes=[pltpu.VMEM((3, rows, D_MODEL), BF16)],
        compiler_params=_cparams("arbitrary"),
        name="hyena_layer",
    )(x, mods, w_in, conv_w, conv_b.reshape(conv_b.shape[0], 1, conv_b.shape[1]), fwd, bwd, spec,
      bias.reshape(2, 1, D_MODEL), w_out, ln_g, ln_b)


def _hyena_tiles(seq_len, batch):
    tq = tr = min(seq_len, 512)
    tc = D_MODEL if seq_len <= 256 else 512
    nseq = math.gcd(batch, max(1, 1024 // seq_len))
    return tq, tr, tc, nseq


def kernel(x_prompt, x_sample, cache_k, cache_v, c, c_ctx, mod_w, mod_b, ln_g, ln_b, ffn_w_in, ffn_w_out,
           attn_w_qkv, attn_w_o, attn_rpb, hy_w_in, hy_conv_w, hy_conv_b, hy_f_w1, hy_f_b1, hy_f_freq1,
           hy_f_w2, hy_f_b2, hy_f_freq2, hy_f_w3, hy_f_b3, hy_bias, hy_w_out):
    batch, seq, _ = x_prompt.shape
    dec_batch, dec_seq, _ = x_sample.shape
    assert dec_batch + 1 <= COND_ROWS

    conds = jnp.zeros((COND_ROWS, D_MODEL), F32).at[0].set(c_ctx).at[1:1 + dec_batch].set(c)
    mods = _mods(conds, mod_w, mod_b).reshape(DEPTH, COND_ROWS, N_MOD, D_MODEL)
    ln_g4 = ln_g.reshape(DEPTH, 3, 1, D_MODEL)
    ln_b4 = ln_b.reshape(DEPTH, 3, 1, D_MODEL)

    streams = {
        "ctx": dict(x=x_prompt.reshape(batch * seq, D_MODEL), batch=batch, seq=seq, crow=lambda t: 0),
        "lat": dict(x=x_sample.reshape(dec_batch * dec_seq, D_MODEL), batch=dec_batch, seq=dec_seq,
                    crow=lambda t: 1 + t // dec_seq),
    }
    hyena_consts = {}
    for name, st in streams.items():
        sl = st["seq"]
        tq, tr, tc, nseq = _hyena_tiles(sl, st["batch"])
        fwd, bwd = _dft_tables(sl, tq, tr)
        hyena_consts[name] = (jnp.asarray(fwd).astype(BF16), jnp.asarray(bwd).astype(BF16), tq, tc, nseq)

    ffn_w_in, ffn_w_out, attn_w_qkv, attn_w_o, hy_w_in, hy_w_out = (
        w.astype(BF16) for w in (ffn_w_in, ffn_w_out, attn_w_qkv, attn_w_o, hy_w_in, hy_w_out))

    new_k, new_v = [], []
    outs = {}
    for name, st in streams.items():
        x, nb, sl, crow = st["x"], st["batch"], st["seq"], st["crow"]
        for layer in range(DEPTH):
            j = layer // 2
            x = _ffn(x, mods, crow, ffn_w_in, ffn_w_out, ln_g4, ln_b4, layer, 0)
            if layer % 2 == 0:
                if name == "ctx":
                    x, k_new, v_new = _attn_ctx(x, mods, crow, attn_w_qkv, attn_w_o, j, ln_g4, ln_b4, layer, nb, sl)
                    new_k.append(k_new)
                    new_v.append(v_new)
                else:
                    qkv = _proj(x, mods, crow, attn_w_qkv, layer, j, tm=1024, tn=1024, out_dtype=BF16)
                    kc = cache_k[:, j].reshape(nb, -1, D_MODEL)
                    vc = cache_v[:, j].reshape(nb, -1, D_MODEL)
                    a = _attn_nbr(qkv, kc, vc, attn_rpb[j], nb, sl)
                    x = _outproj(x, a, mods, crow, attn_w_o, j, ln_g4, ln_b4, layer)
            else:
                fwd, bwd, tq, tc, nseq = hyena_consts[name]
                filt = _filters(sl, hy_f_w1[j], hy_f_b1[j], hy_f_freq1[j], hy_f_w2[j], hy_f_b2[j], hy_f_freq2[j],
                                hy_f_w3[j], hy_f_b3[j])
                spec = _spectrum(fwd, filt, sl, tq, tc)
                if sl == tq:
                    x = _hyena_layer(x, mods, crow, hy_w_in, hy_w_out, j, hy_conv_w, hy_conv_b, fwd, bwd, spec,
                                     hy_bias[j], ln_g4, ln_b4, layer, sl, nseq)
                else:
                    u = _proj(x, mods, crow, hy_w_in, layer, j, tm=max(sl, 1024), tn=512, out_dtype=BF16,
                              conv=(hy_conv_w, hy_conv_b), seq_len=sl)
                    a = _hyena_core(u, fwd, bwd, spec, hy_bias[j], nb, sl, tq, tc)
                    x = _outproj(x, a, mods, crow, hy_w_out, j, ln_g4, ln_b4, layer)
            x = _ffn(x, mods, crow, ffn_w_in, ffn_w_out, ln_g4, ln_b4, layer, 1)
        outs[name] = x

    y_prompt = outs["ctx"].reshape(batch, seq, D_MODEL)
    y_sample = outs["lat"].reshape(dec_batch, dec_seq, D_MODEL)
    return (y_prompt, y_sample, jnp.concatenate(new_k, axis=1), jnp.concatenate(new_v, axis=1))
```

```python
import functools
import math

import numpy as np
import jax
import jax.numpy as jnp
from jax import lax
from jax.experimental import pallas as pl
from jax.experimental.pallas import tpu as pltpu

D_MODEL = 1024
DEPTH = 2
GRID_W = 64
N_HEADS = 16
HEAD_DIM = D_MODEL // N_HEADS
WIN_ROWS = 8
WIN_COLS = 16
D_FF = 2816
N_MOD = 9
HY_BANDS = 16
HY_EMB = 1 + 2 * HY_BANDS
HY_FILTER_W = 64
HY_FAST_DECAY = 0.3
HY_SLOW_DECAY = 1.5
HY_TARGET = 1e-2
ALPHA = (2 * DEPTH) ** 0.25
LN_EPS = 1e-5

F32 = jnp.float32
BF16 = jnp.bfloat16
NEG = -1e30
LANES = 128
VMEM_LIMIT = 56 * 1024 * 1024
COND_ROWS = 8

Q_ROWS = 8
K_ROWS = 16
BAND_ROWS = 10


def _cparams(*sem):
    return pltpu.CompilerParams(dimension_semantics=sem, vmem_limit_bytes=VMEM_LIMIT)


def _dot(a, b):
    return jnp.dot(a, b, preferred_element_type=F32)


def _dot_nt(a, b):
    return lax.dot_general(a, b, (((1,), (1,)), ((), ())), preferred_element_type=F32)


def _layer_norm(y, g, b):
    mu = jnp.mean(y, axis=-1, keepdims=True)
    yc = y - mu
    var = jnp.mean(yc * yc, axis=-1, keepdims=True)
    return yc * lax.rsqrt(var + LN_EPS) * g + b


def _mods_kernel(c_ref, w_ref, b_ref, o_ref):
    c = c_ref[...]
    s = c * jax.nn.sigmoid(c)
    o_ref[...] = _dot(s.astype(BF16), w_ref[...].astype(BF16)) + b_ref[...]


def _mods(conds, mod_w, mod_b):
    tn = 1024
    nd = N_MOD * D_MODEL
    return pl.pallas_call(
        _mods_kernel,
        grid=(DEPTH, nd // tn),
        in_specs=[
            pl.BlockSpec((COND_ROWS, D_MODEL), lambda l, j: (0, 0)),
            pl.BlockSpec((None, D_MODEL, tn), lambda l, j: (l, 0, j)),
            pl.BlockSpec((None, 1, tn), lambda l, j: (l, 0, j)),
        ],
        out_specs=pl.BlockSpec((None, COND_ROWS, tn), lambda l, j: (l, 0, j)),
        out_shape=jax.ShapeDtypeStruct((DEPTH, COND_ROWS, nd), F32),
        compiler_params=_cparams("arbitrary", "arbitrary"),
        name="mods",
    )(conds, mod_w, mod_b.reshape(DEPTH, 1, nd))


def _ffn_kernel(x_ref, mod_ref, wi_ref, wo_ref, g_ref, b_ref, o_ref, *, mi, n_chunks):
    x = x_ref[...]
    sh = mod_ref[3 * mi:3 * mi + 1, :]
    sc = mod_ref[3 * mi + 1:3 * mi + 2, :]
    gt = mod_ref[3 * mi + 2:3 * mi + 3, :]
    h = (x * (1.0 + sc) + sh).astype(BF16)
    tf = D_FF // n_chunks
    f = None
    for c in range(n_chunks):
        gate = _dot(h, wi_ref[:, c * tf:(c + 1) * tf])
        up = _dot(h, wi_ref[:, D_FF + c * tf:D_FF + (c + 1) * tf])
        a = (gate * jax.nn.sigmoid(gate) * up).astype(BF16)
        part = _dot(a, wo_ref[c * tf:(c + 1) * tf, :])
        f = part if f is None else f + part
    y = ALPHA * x + (0.5 * gt) * f
    o_ref[...] = _layer_norm(y, g_ref[...], b_ref[...])


def _resident(shape, index_map):
    return pl.BlockSpec(shape, index_map, pipeline_mode=pl.Buffered(1))


def _ffn(x, mods, crow, w_in, w_out, ln_g, ln_b, layer, which, tm=1024, n_chunks=11):
    m = x.shape[0]
    mi = 0 if which == 0 else 2
    return pl.pallas_call(
        functools.partial(_ffn_kernel, mi=mi, n_chunks=n_chunks),
        grid=(m // tm,),
        in_specs=[
            pl.BlockSpec((tm, D_MODEL), lambda i: (i, 0)),
            pl.BlockSpec((None, None, N_MOD, D_MODEL), lambda i: (layer, crow(i * tm), 0, 0)),
            _resident((None, None, D_MODEL, 2 * D_FF), lambda i: (layer, which, 0, 0)),
            _resident((None, None, D_FF, D_MODEL), lambda i: (layer, which, 0, 0)),
            pl.BlockSpec((None, None, 1, D_MODEL), lambda i: (layer, mi, 0, 0)),
            pl.BlockSpec((None, None, 1, D_MODEL), lambda i: (layer, mi, 0, 0)),
        ],
        out_specs=pl.BlockSpec((tm, D_MODEL), lambda i: (i, 0)),
        out_shape=jax.ShapeDtypeStruct((m, D_MODEL), F32),
        compiler_params=_cparams("arbitrary"),
        name="ffn",
    )(x, mods, w_in, w_out, ln_g, ln_b)


def _proj_kernel(x_ref, mod_ref, w_ref, *rest, seq_len):
    if seq_len:
        cw_ref, cb_ref, o_ref, h_ref = rest
    else:
        o_ref, h_ref = rest
    j = pl.program_id(1)

    @pl.when(j == 0)
    def _():
        sh = mod_ref[3:4, :]
        sc = mod_ref[4:5, :]
        h_ref[...] = (x_ref[...] * (1.0 + sc) + sh).astype(BF16)

    u = _dot(h_ref[...], w_ref[...])
    if seq_len:
        tm = u.shape[0]
        pos = lax.broadcasted_iota(jnp.int32, (tm, 1), 0) % seq_len
        prev = jnp.where(pos == 0, 0.0, pltpu.roll(u, 1, 0))
        nxt = jnp.where(pos == seq_len - 1, 0.0, pltpu.roll(u, tm - 1, 0))
        u = prev * cw_ref[0:1, :] + u * cw_ref[1:2, :] + nxt * cw_ref[2:3, :] + cb_ref[...]
    o_ref[...] = u.astype(o_ref.dtype)


def _proj(x, mods, crow, w, layer, widx, *, tm, tn, out_dtype, conv=None, seq_len=0):
    m = x.shape[0]
    npg = D_MODEL // tn
    in_specs = [
        pl.BlockSpec((tm, D_MODEL), lambda i, j: (i, 0)),
        pl.BlockSpec((None, None, N_MOD, D_MODEL), lambda i, j: (layer, crow(i * tm), 0, 0)),
        pl.BlockSpec((None, D_MODEL, tn), lambda i, j: (widx, 0, j)),
    ]
    args = [x, mods, w]
    if conv is not None:
        cw, cb = conv
        in_specs += [
            pl.BlockSpec((None, 3, tn), lambda i, j: (widx, 0, j)),
            pl.BlockSpec((None, 1, tn), lambda i, j: (widx, 0, j)),
        ]
        args += [cw, cb.reshape(cb.shape[0], 1, cb.shape[1])]
    return pl.pallas_call(
        functools.partial(_proj_kernel, seq_len=seq_len),
        grid=(m // tm, 3 * npg),
        in_specs=in_specs,
        out_specs=pl.BlockSpec((None, tm, tn), lambda i, j: (j // npg, i, j % npg)),
        out_shape=jax.ShapeDtypeStruct((3, m, D_MODEL), out_dtype),
        scratch_shapes=[pltpu.VMEM((tm, D_MODEL), BF16)],
        compiler_params=_cparams("arbitrary", "arbitrary"),
        name="proj_conv" if seq_len else "proj",
    )(*args)


def _outproj_kernel(x_ref, a_ref, mod_ref, w_ref, g_ref, b_ref, o_ref):
    mix = _dot(a_ref[...], w_ref[...])
    y = ALPHA * x_ref[...] + mod_ref[5:6, :] * mix
    o_ref[...] = _layer_norm(y, g_ref[...], b_ref[...])


def _outproj(x, a, mods, crow, w, widx, ln_g, ln_b, layer, tm=1024):
    m = x.shape[0]
    return pl.pallas_call(
        _outproj_kernel,
        grid=(m // tm,),
        in_specs=[
            pl.BlockSpec((tm, D_MODEL), lambda i: (i, 0)),
            pl.BlockSpec((tm, D_MODEL), lambda i: (i, 0)),
            pl.BlockSpec((None, None, N_MOD, D_MODEL), lambda i: (layer, crow(i * tm), 0, 0)),
            pl.BlockSpec((None, D_MODEL, D_MODEL), lambda i: (widx, 0, 0)),
            pl.BlockSpec((None, None, 1, D_MODEL), lambda i: (layer, 1, 0, 0)),
            pl.BlockSpec((None, None, 1, D_MODEL), lambda i: (layer, 1, 0, 0)),
        ],
        out_specs=pl.BlockSpec((tm, D_MODEL), lambda i: (i, 0)),
        out_shape=jax.ShapeDtypeStruct((m, D_MODEL), F32),
        compiler_params=_cparams("arbitrary"),
        name="outproj",
    )(x, a, mods, w, ln_g, ln_b)


def _lane_tiles(a):
    return [a[:, i:i + LANES] for i in range(0, a.shape[1], LANES)]


def _head_masks():
    lane = lax.broadcasted_iota(jnp.int32, (1, LANES), 1)
    return lane < HEAD_DIM, lane >= HEAD_DIM


def _attn_ctx_kernel(x_ref, mod_ref, wqkv_ref, wo_ref, g_ref, b_ref, o_ref, ko_ref, vo_ref,
                     q_scr, k_scr, v_scr, a_scr, *, seq, nseq):
    x = x_ref[...]
    h = (x * (1.0 + mod_ref[4:5, :]) + mod_ref[3:4, :]).astype(BF16)
    q_scr[...] = (_dot(h, wqkv_ref[:, :D_MODEL]) * HEAD_DIM ** -0.5).astype(BF16)
    k_scr[...] = _dot(h, wqkv_ref[:, D_MODEL:2 * D_MODEL])
    v_scr[...] = _dot(h, wqkv_ref[:, 2 * D_MODEL:])
    masks = _head_masks()
    for s in range(nseq):
        rows = slice(s * seq, (s + 1) * seq)
        ko_ref[s] = k_scr[rows, :].reshape(seq, N_HEADS, HEAD_DIM)
        vo_ref[s] = v_scr[rows, :].reshape(seq, N_HEADS, HEAD_DIM)
        for p in range(D_MODEL // LANES):
            sl = slice(p * LANES, (p + 1) * LANES)
            q2 = q_scr[rows, sl]
            k2 = k_scr[rows, sl]
            v2 = v_scr[rows, sl]
            o2 = jnp.zeros(q2.shape, F32)
            for hm in masks:
                kh = jnp.where(hm, k2, 0.0).astype(BF16)
                vh = jnp.where(hm, v2, 0.0).astype(BF16)
                sc = _dot_nt(q2, kh)
                e = jnp.exp(sc - jnp.max(sc, axis=-1, keepdims=True))
                inv = 1.0 / jnp.sum(e, axis=-1, keepdims=True)
                o2 = o2 + _dot(e.astype(BF16), vh) * inv
            a_scr[rows, sl] = o2.astype(BF16)
    y = ALPHA * x + mod_ref[5:6, :] * _dot(a_scr[...], wo_ref[...])
    o_ref[...] = _layer_norm(y, g_ref[...], b_ref[...])


def _attn_ctx(x, mods, crow, w_qkv, w_o, widx, ln_g, ln_b, layer, batch, seq, nseq=2):
    rows = nseq * seq
    cache_spec = pl.BlockSpec((nseq, None, seq, N_HEADS, HEAD_DIM), lambda i: (i, 0, 0, 0, 0))
    cache_shape = jax.ShapeDtypeStruct((batch, 1, seq, N_HEADS, HEAD_DIM), F32)
    return pl.pallas_call(
        functools.partial(_attn_ctx_kernel, seq=seq, nseq=nseq),
        grid=(batch // nseq,),
        in_specs=[
            pl.BlockSpec((rows, D_MODEL), lambda i: (i, 0)),
            pl.BlockSpec((None, None, N_MOD, D_MODEL), lambda i: (layer, crow(i * rows), 0, 0)),
            _resident((None, D_MODEL, 3 * D_MODEL), lambda i: (widx, 0, 0)),
            _resident((None, D_MODEL, D_MODEL), lambda i: (widx, 0, 0)),
            pl.BlockSpec((None, None, 1, D_MODEL), lambda i: (layer, 1, 0, 0)),
            pl.BlockSpec((None, None, 1, D_MODEL), lambda i: (layer, 1, 0, 0)),
        ],
        out_specs=[pl.BlockSpec((rows, D_MODEL), lambda i: (i, 0)), cache_spec, cache_spec],
        out_shape=[jax.ShapeDtypeStruct((batch * seq, D_MODEL), F32), cache_shape, cache_shape],
        scratch_shapes=[pltpu.VMEM((rows, D_MODEL), BF16), pltpu.VMEM((rows, D_MODEL), F32),
                        pltpu.VMEM((rows, D_MODEL), F32), pltpu.VMEM((rows, D_MODEL), BF16)],
        compiler_params=_cparams("arbitrary"),
        name="attn_ctx",
    )(x, mods, w_qkv, w_o, ln_g, ln_b)


def _nbr_geometry(rows):
    kr = min(WIN_ROWS, rows)
    blocks = []
    for blk in range(rows // Q_ROWS):
        r0 = blk * Q_ROWS
        lo = min(max(r0 - kr // 2, 0), rows - kr)
        key_start = min(lo, rows - K_ROWS)
        bands = []
        for r in range(r0, r0 + Q_ROWS):
            rs = min(max(r - kr // 2, 0), rows - kr)
            a = rs - key_start
            band_start = min(a - a % 2, K_ROWS - BAND_ROWS)
            off = a - band_start
            assert 0 <= band_start and 0 <= off and off + kr <= BAND_ROWS
            bands.append((band_start, off, rs - r + WIN_ROWS - 1))
        blocks.append((key_start, bands))
    return kr, blocks


def _nbr_bias_rows(rpb):
    pad = GRID_W - (2 * WIN_COLS - 1)
    row = jnp.pad(rpb, ((0, 0), (0, 0), (0, pad)))
    return jnp.concatenate([row[:, :-1], row[:, 1:]], axis=-1)


def _attn_nbr_kernel(q_ref, k_ref, v_ref, kc_ref, vc_ref, rp_ref, o_ref,
                     s_ref, sc_ref, p_ref, pc_ref, inv_ref, tab_ref, *, blocks, kr):
    masks = _head_masks()
    left = masks[0]
    qn = Q_ROWS * GRID_W
    kn = K_ROWS * GRID_W
    qcol = lax.broadcasted_iota(jnp.int32, (GRID_W, LANES), 0)
    kcol = lax.broadcasted_iota(jnp.int32, (GRID_W, LANES), 1) % GRID_W
    cstart = jnp.clip(qcol - WIN_COLS // 2, 0, GRID_W - WIN_COLS)
    col_ok = (kcol >= cstart) & (kcol < cstart + WIN_COLS)
    for hh in range(2):
        for d in range(2 * WIN_ROWS - 2):
            row = jnp.broadcast_to(rp_ref[hh, d:d + 1, :], (GRID_W, LANES))
            toep = pltpu.roll(row, LANES - (WIN_COLS - 1), 1, stride=1, stride_axis=0)
            tab_ref[hh, d] = jnp.where(col_ok, toep, NEG)
    for blk, (key_start, bands) in enumerate(blocks):
        q2 = (q_ref[blk * qn:(blk + 1) * qn, :] * HEAD_DIM ** -0.5).astype(BF16)
        k2 = k_ref[key_start * GRID_W:key_start * GRID_W + kn, :]
        v2 = v_ref[key_start * GRID_W:key_start * GRID_W + kn, :]
        o2 = jnp.zeros(q2.shape, F32)
        p_ref[...] = jnp.zeros_like(p_ref)
        for hh, hm in enumerate(masks):
            kh = jnp.where(hm, k2, 0.0).astype(BF16)
            vh = jnp.where(hm, v2, 0.0).astype(BF16)
            kch = jnp.where(hm, kc_ref[...], 0.0).astype(BF16)
            vch = jnp.where(hm, vc_ref[...], 0.0).astype(BF16)
            s_ref[...] = _dot_nt(q2, kh)
            sc_ref[...] = _dot_nt(q2, kch)
            for rl, (band_start, off, drow) in enumerate(bands):
                rsl = slice(rl * GRID_W, (rl + 1) * GRID_W)
                tiles = []
                for j in range(0, BAND_ROWS, 2):
                    ok0, ok1 = off <= j < off + kr, off <= j + 1 < off + kr
                    if not (ok0 or ok1):
                        continue
                    bias = tab_ref[hh, drow - off + j]
                    if not ok0:
                        bias = jnp.where(left, NEG, bias)
                    if not ok1:
                        bias = jnp.where(left, bias, NEG)
                    csl = slice((band_start + j) * GRID_W, (band_start + j + 2) * GRID_W)
                    tiles.append((csl, s_ref[rsl, csl] + bias))
                sc = sc_ref[rsl, :]
                top = functools.reduce(jnp.maximum, _lane_tiles(sc) + [sb for _, sb in tiles])
                mx = jnp.max(top, axis=-1, keepdims=True)
                ec = jnp.exp(sc - mx)
                tot = functools.reduce(jnp.add, _lane_tiles(ec))
                for csl, sb in tiles:
                    eb = jnp.exp(sb - mx)
                    tot = tot + eb
                    p_ref[rsl, csl] = eb.astype(BF16)
                inv_ref[rsl, :] = 1.0 / jnp.sum(tot, axis=-1, keepdims=True)
                pc_ref[rsl, :] = ec.astype(BF16)
            o2 = o2 + (_dot(p_ref[...], vh) + _dot(pc_ref[...], vch)) * inv_ref[...]
        o_ref[blk * qn:(blk + 1) * qn, :] = o2.astype(o_ref.dtype)


def _attn_nbr(qkv, kc, vc, rpb, batch, seq):
    rows = seq // GRID_W
    kr, blocks = _nbr_geometry(rows)
    n_pairs = D_MODEL // LANES
    n_d = 2 * WIN_ROWS - 2
    rp = _nbr_bias_rows(rpb).reshape(n_pairs, 2, n_d, LANES)
    s_ctx = kc.shape[1]
    qspec = lambda g: pl.BlockSpec((None, seq, LANES), lambda b, p: (g, b, p))
    cspec = pl.BlockSpec((None, s_ctx, LANES), lambda b, p: (b, 0, p))
    qn, kn = Q_ROWS * GRID_W, K_ROWS * GRID_W
    return pl.pallas_call(
        functools.partial(_attn_nbr_kernel, blocks=blocks, kr=kr),
        grid=(batch, n_pairs),
        in_specs=[qspec(0), qspec(1), qspec(2), cspec, cspec,
                  pl.BlockSpec((None, 2, n_d, LANES), lambda b, p: (p, 0, 0, 0))],
        out_specs=pl.BlockSpec((seq, LANES), lambda b, p: (b, p)),
        out_shape=jax.ShapeDtypeStruct((batch * seq, D_MODEL), BF16),
        scratch_shapes=[pltpu.VMEM((qn, kn), F32), pltpu.VMEM((qn, s_ctx), F32),
                        pltpu.VMEM((qn, kn), BF16), pltpu.VMEM((qn, s_ctx), BF16),
                        pltpu.VMEM((qn, 1), F32), pltpu.VMEM((2, n_d, GRID_W, LANES), F32)],
        compiler_params=_cparams("arbitrary", "arbitrary"),
        name="attn_nbr",
    )(qkv, qkv, qkv, kc, vc, rp)


def _dft_tables(seq_len, tq, tr):
    n = 2 * seq_len
    f = np.arange(seq_len, dtype=np.float64)[:, None] + 0.5
    s = np.arange(seq_len, dtype=np.float64)[None, :]
    ang = 2.0 * np.pi * f * s / n
    re = np.cos(ang).reshape(seq_len // tq, tq, seq_len)
    im = (-np.sin(ang)).reshape(seq_len // tq, tq, seq_len)
    fwd = np.concatenate([re, im], axis=1).astype(np.float32)
    bwd = fwd.reshape(2 * seq_len, seq_len).T
    return fwd, np.ascontiguousarray(bwd).reshape(seq_len // tr, tr, 2 * seq_len)


def _filter_features(seq_len):
    pos = np.arange(seq_len, dtype=np.float64)[:, None]
    t = pos / max(seq_len - 1, 1)
    bands = np.linspace(1e-4, HY_BANDS - 1, HY_BANDS)
    ang = 2.0 * np.pi * pos / seq_len * bands
    z = np.concatenate([t, np.cos(ang), -np.sin(ang)], axis=-1)
    zp = np.zeros((seq_len, HY_FILTER_W), np.float64)
    zp[:, :HY_EMB] = z
    deltas = np.abs(np.linspace(math.log(HY_TARGET) / HY_SLOW_DECAY, math.log(HY_TARGET) / HY_FAST_DECAY, D_MODEL))
    return zp.astype(np.float32), t.astype(np.float32), deltas.astype(np.float32)[None, :]


def _filter_kernel(z_ref, t_ref, dl_ref, w1_ref, b1_ref, f1_ref, w2_ref, b2_ref, f2_ref, w3_ref, b3_ref, o_ref,
                   hid_ref, decay_ref):
    hp = lax.Precision.HIGHEST

    @pl.when(pl.program_id(0) == 0)
    def _():
        hid = jnp.sin(f1_ref[...] * (jnp.dot(z_ref[...], w1_ref[...], precision=hp, preferred_element_type=F32) + b1_ref[...]))
        hid_ref[...] = jnp.sin(f2_ref[...] * (jnp.dot(hid, w2_ref[...], precision=hp, preferred_element_type=F32) + b2_ref[...]))
        decay_ref[...] = jnp.exp(-t_ref[...] * dl_ref[...])

    filt = jnp.dot(hid_ref[...], w3_ref[...], precision=hp, preferred_element_type=F32) + b3_ref[...]
    decay = decay_ref[...]
    hf = filt[:, :D_MODEL] * decay
    hb = filt[:, D_MODEL:] * decay
    row = lax.broadcasted_iota(jnp.int32, (hb.shape[0], 1), 0)
    hb = jnp.where(row == 0, 0.0, hb)
    o_ref[:, :D_MODEL] = (hf + hb).astype(o_ref.dtype)
    o_ref[:, D_MODEL:] = (hf - hb).astype(o_ref.dtype)


def _filters(seq_len, w1, b1, f1, w2, b2, f2, w3, b3):
    z, t, deltas = _filter_features(seq_len)
    w1p = jnp.zeros((HY_FILTER_W, HY_FILTER_W), F32).at[:HY_EMB].set(w1)
    full = lambda shape: pl.BlockSpec(shape, lambda o: (0,) * len(shape))
    row = lambda a: a.reshape(1, -1)
    return pl.pallas_call(
        _filter_kernel,
        grid=(2,),
        in_specs=[full((seq_len, HY_FILTER_W)), full((seq_len, 1)), full((1, D_MODEL)),
                  full((HY_FILTER_W, HY_FILTER_W)), full((1, HY_FILTER_W)), full((1, HY_FILTER_W)),
                  full((HY_FILTER_W, HY_FILTER_W)), full((1, HY_FILTER_W)), full((1, HY_FILTER_W)),
                  pl.BlockSpec((HY_FILTER_W, 2 * D_MODEL), lambda o: (0, o)),
                  pl.BlockSpec((1, 2 * D_MODEL), lambda o: (0, o))],
        out_specs=pl.BlockSpec((None, seq_len, 2 * D_MODEL), lambda o: (o, 0, 0)),
        out_shape=jax.ShapeDtypeStruct((2, seq_len, 2 * D_MODEL), BF16),
        scratch_shapes=[pltpu.VMEM((seq_len, HY_FILTER_W), F32), pltpu.VMEM((seq_len, D_MODEL), F32)],
        compiler_params=_cparams("arbitrary"),
        name="hyena_filters",
    )(jnp.asarray(z),jnp.asarray(t), jnp.asarray(deltas), w1p, row(b1), row(f1), w2, row(b2), row(f2), w3, row(b3))


def _spectrum_kernel(g_ref, hs_ref, hd_ref, o_ref, *, tq, scale):
    o_ref[:tq, :] = scale * _dot(g_ref[:tq, :], hs_ref[...])
    o_ref[tq:, :] = scale * _dot(g_ref[tq:, :], hd_ref[...])


def _spectrum(fwd, filt, seq_len, tq, tc):
    nf = seq_len // tq
    ncb = D_MODEL // tc
    return pl.pallas_call(
        functools.partial(_spectrum_kernel, tq=tq, scale=1.0 / seq_len),
        grid=(2, ncb, nf),
        in_specs=[pl.BlockSpec((None, 2 * tq, seq_len), lambda o, c, f: (f, 0, 0)),
                  pl.BlockSpec((None, seq_len, tc), lambda o, c, f: (o, 0, c)),
                  pl.BlockSpec((None, seq_len, tc), lambda o, c, f: (o, 0, c + ncb))],
        out_specs=pl.BlockSpec((None, None, 2 * tq, tc), lambda o, c, f: (o, f, 0, c)),
        out_shape=jax.ShapeDtypeStruct((2, nf, 2 * tq, D_MODEL), F32),
        compiler_params=_cparams("arbitrary", "arbitrary", "arbitrary"),
        name="hyena_spectrum",
    )(fwd, filt, filt)


def _hyena_kernel(v_ref, gate_ref, g_ref, gt_ref, k_ref, b_ref, o_ref, z_ref, p_ref, *, tq, nf, tr):
    o = pl.program_id(2)
    st = pl.program_id(3)

    @pl.when((o == 0) & (st == 0))
    def _():
        z_ref[...] = v_ref[...].astype(BF16)

    @pl.when(st < nf)
    def _():
        zf = _dot(g_ref[...], z_ref[...])
        zr, zi = zf[:tq], zf[tq:]
        kr, ki = k_ref[:tq, :], k_ref[tq:, :]
        prod = jnp.concatenate([zr * kr - zi * ki, zr * ki + zi * kr], axis=0)
        p_ref[pl.ds(pl.multiple_of(st * 2 * tq, 2 * tq), 2 * tq), :] = prod.astype(BF16)

    @pl.when(st >= nf)
    def _():
        rows = pl.ds(pl.multiple_of((st - nf) * tr, tr), tr)
        y = _dot(gt_ref[...], p_ref[...]) + z_ref[rows, :].astype(F32) * b_ref[...]
        gated = y * gate_ref[rows, :].astype(F32)

        @pl.when(o == 0)
        def _():
            z_ref[rows, :] = gated.astype(BF16)

        @pl.when(o == 1)
        def _():
            o_ref[rows, :] = gated.astype(o_ref.dtype)


def _hyena_core(u, fwd, bwd_rows, spec, bias, batch, seq_len, tq, tc):
    nf = seq_len // tq
    nr, tr, _ = bwd_rows.shape
    ncb = D_MODEL // tc
    fidx = lambda st: jnp.minimum(st, nf - 1)
    return pl.pallas_call(
        functools.partial(_hyena_kernel, tq=tq, nf=nf, tr=tr),
        grid=(batch, ncb, 2, nf + nr),
        in_specs=[pl.BlockSpec((None, seq_len, tc), lambda b, c, o, st: (0, b, c)),
                  pl.BlockSpec((None, seq_len, tc), lambda b, c, o, st: (1 + o, b, c)),
                  pl.BlockSpec((None, 2 * tq, seq_len), lambda b, c, o, st: (fidx(st), 0, 0)),
                  pl.BlockSpec((None, tr, 2 * seq_len), lambda b, c, o, st: (jnp.maximum(st - nf, 0), 0, 0)),
                  pl.BlockSpec((None, None, 2 * tq, tc), lambda b, c, o, st: (o, fidx(st), 0, c)),
                  pl.BlockSpec((None, 1, tc), lambda b, c, o, st: (o, 0, c))],
        out_specs=pl.BlockSpec((seq_len, tc), lambda b, c, o, st: (b, c)),
        out_shape=jax.ShapeDtypeStruct((batch * seq_len, D_MODEL), BF16),
        scratch_shapes=[pltpu.VMEM((seq_len, tc), BF16), pltpu.VMEM((2 * seq_len, tc), BF16)],
        compiler_params=_cparams("arbitrary", "arbitrary", "arbitrary", "arbitrary"),
        name="hyena_conv",
    )(u, u, fwd, bwd_rows, spec, bias.reshape(2, 1, D_MODEL))


def _hyena_layer_kernel(x_ref, mod_ref, win_ref, cw_ref, cb_ref, g_ref, gt_ref, k_ref, hb_ref, wout_ref,
                        lg_ref, lb_ref, o_ref, u_scr, *, seq_len, nseq, tn):
    x = x_ref[...]
    h = (x * (1.0 + mod_ref[4:5, :]) + mod_ref[3:4, :]).astype(BF16)
    rows = nseq * seq_len
    pos = lax.broadcasted_iota(jnp.int32, (rows, 1), 0) % seq_len
    first, last = pos == 0, pos == seq_len - 1
    for g in range(3):
        for cc in range(D_MODEL // tn):
            cols = slice(g * D_MODEL + cc * tn, g * D_MODEL + (cc + 1) * tn)
            u = _dot(h, win_ref[:, cols])
            prev = jnp.where(first, 0.0, pltpu.roll(u, 1, 0))
            nxt = jnp.where(last, 0.0, pltpu.roll(u, rows - 1, 0))
            u = prev * cw_ref[0:1, cols] + u * cw_ref[1:2, cols] + nxt * cw_ref[2:3, cols] + cb_ref[:, cols]
            u_scr[g, :, cc * tn:(cc + 1) * tn] = u.astype(BF16)
    for o in range(2):
        kr, ki = k_ref[o, :seq_len, :], k_ref[o, seq_len:, :]
        for s in range(nseq):
            r = slice(s * seq_len, (s + 1) * seq_len)
            z = u_scr[0, r, :]
            zf = _dot(g_ref[...], z)
            zr, zi = zf[:seq_len], zf[seq_len:]
            prod = jnp.concatenate([zr * kr - zi * ki, zr * ki + zi * kr], axis=0).astype(BF16)
            y = _dot(gt_ref[...], prod) + z.astype(F32) * hb_ref[o]
            u_scr[0, r, :] = (y * u_scr[1 + o, r, :].astype(F32)).astype(BF16)
    y = ALPHA * x + mod_ref[5:6, :] * _dot(u_scr[0], wout_ref[...])
    o_ref[...] = _layer_norm(y, lg_ref[...], lb_ref[...])


def _hyena_layer(x, mods, crow, w_in, w_out, widx, conv_w, conv_b, fwd, bwd, spec, bias, ln_g, ln_b, layer,
                 seq_len, nseq, tn=512):
    m = x.shape[0]
    rows = nseq * seq_len
    assert fwd.shape[0] == 1
    return pl.pallas_call(
        functools.partial(_hyena_layer_kernel, seq_len=seq_len, nseq=nseq, tn=tn),
        grid=(m // rows,),
        in_specs=[
            pl.BlockSpec((rows, D_MODEL), lambda i: (i, 0)),
            pl.BlockSpec((None, None, N_MOD, D_MODEL), lambda i: (layer, crow(i * rows), 0, 0)),
            _resident((None, D_MODEL, 3 * D_MODEL), lambda i: (widx, 0, 0)),
            pl.BlockSpec((None, 3, 3 * D_MODEL), lambda i: (widx, 0, 0)),
            pl.BlockSpec((None, 1, 3 * D_MODEL), lambda i: (widx, 0, 0)),
            pl.BlockSpec((None, 2 * seq_len, seq_len), lambda i: (0, 0, 0)),
            pl.BlockSpec((None, seq_len, 2 * seq_len), lambda i: (0, 0, 0)),
            _resident((2, None, 2 * seq_len, D_MODEL), lambda i: (0, 0, 0, 0)),
            pl.BlockSpec((2, 1, D_MODEL), lambda i: (0, 0, 0)),
            _resident((None, D_MODEL, D_MODEL), lambda i: (widx, 0, 0)),
            pl.BlockSpec((None, None, 1, D_MODEL), lambda i: (layer, 1, 0, 0)),
            pl.BlockSpec((None, None, 1, D_MODEL), lambda i: (layer, 1, 0, 0)),
        ],
        out_specs=pl.BlockSpec((rows, D_MODEL), lambda i: (i, 0)),
        out_shape=jax.ShapeDtypeStruct((m, D_MODEL), F32),
        scratch_shapes=[pltpu.VMEM((3, rows, D_MODEL), BF16)],
        compiler_params=_cparams("arbitrary"),
        name="hyena_layer",
    )(x, mods, w_in, conv_w, conv_b.reshape(conv_b.shape[0], 1, conv_b.shape[1]), fwd, bwd, spec,
      bias.reshape(2, 1, D_MODEL), w_out, ln_g, ln_b)


def _hyena_tiles(seq_len, batch):
    tq = tr = min(seq_len, 512)
    tc = D_MODEL if seq_len <= 256 else 512
    nseq = math.gcd(batch, max(1, 1024 // seq_len))
    return tq, tr, tc, nseq


def kernel(x_prompt, x_sample, cache_k, cache_v, c, c_ctx, mod_w, mod_b, ln_g, ln_b, ffn_w_in, ffn_w_out,
           attn_w_qkv, attn_w_o, attn_rpb, hy_w_in, hy_conv_w, hy_conv_b, hy_f_w1, hy_f_b1, hy_f_freq1,
           hy_f_w2, hy_f_b2, hy_f_freq2, hy_f_w3, hy_f_b3, hy_bias, hy_w_out):
    batch, seq, _ = x_prompt.shape
    dec_batch, dec_seq, _ = x_sample.shape
    assert dec_batch + 1 <= COND_ROWS

    conds = jnp.zeros((COND_ROWS, D_MODEL), F32).at[0].set(c_ctx).at[1:1 + dec_batch].set(c)
    mods = _mods(conds, mod_w, mod_b).reshape(DEPTH, COND_ROWS, N_MOD, D_MODEL)
    ln_g4 = ln_g.reshape(DEPTH, 3, 1, D_MODEL)
    ln_b4 = ln_b.reshape(DEPTH, 3, 1, D_MODEL)

    streams = {
        "ctx": dict(x=x_prompt.reshape(batch * seq, D_MODEL), batch=batch, seq=seq, crow=lambda t: 0),
        "lat": dict(x=x_sample.reshape(dec_batch * dec_seq, D_MODEL), batch=dec_batch, seq=dec_seq,
                    crow=lambda t: 1 + t // dec_seq),
    }
    hyena_consts = {}
    for name, st in streams.items():
        sl = st["seq"]
        tq, tr, tc, nseq = _hyena_tiles(sl, st["batch"])
        fwd, bwd = _dft_tables(sl, tq, tr)
        hyena_consts[name] = (jnp.asarray(fwd).astype(BF16), jnp.asarray(bwd).astype(BF16), tq, tc, nseq)

    ffn_w_in, ffn_w_out, attn_w_qkv, attn_w_o, hy_w_in, hy_w_out = (
        w.astype(BF16) for w in (ffn_w_in, ffn_w_out, attn_w_qkv, attn_w_o, hy_w_in, hy_w_out))

    new_k, new_v = [], []
    outs = {}
    for name, st in streams.items():
        x, nb, sl, crow = st["x"], st["batch"], st["seq"], st["crow"]
        for layer in range(DEPTH):
            j = layer // 2
            x = _ffn(x, mods, crow, ffn_w_in, ffn_w_out, ln_g4, ln_b4, layer, 0)
            if layer % 2 == 0:
                if name == "ctx":
                    x, k_new, v_new = _attn_ctx(x, mods, crow, attn_w_qkv, attn_w_o, j, ln_g4, ln_b4, layer, nb, sl)
                    new_k.append(k_new)
                    new_v.append(v_new)
                else:
                    qkv = _proj(x, mods, crow, attn_w_qkv, layer, j, tm=1024, tn=1024, out_dtype=BF16)
                    kc = cache_k[:, j].reshape(nb, -1, D_MODEL)
                    vc = cache_v[:, j].reshape(nb, -1, D_MODEL)
                    a = _attn_nbr(qkv, kc, vc, attn_rpb[j], nb, sl)
                    x = _outproj(x, a, mods, crow, attn_w_o, j, ln_g4, ln_b4, layer)
            else:
                fwd, bwd, tq, tc, nseq = hyena_consts[name]
                filt = _filters(sl, hy_f_w1[j], hy_f_b1[j], hy_f_freq1[j], hy_f_w2[j], hy_f_b2[j], hy_f_freq2[j],
                                hy_f_w3[j], hy_f_b3[j])
                spec = _spectrum(fwd, filt, sl, tq, tc)
                if sl == tq:
                    x = _hyena_layer(x, mods, crow, hy_w_in, hy_w_out, j, hy_conv_w, hy_conv_b, fwd, bwd, spec,
                                     hy_bias[j], ln_g4, ln_b4, layer, sl, nseq)
                else:
                    u = _proj(x, mods, crow, hy_w_in, layer, j, tm=max(sl, 1024), tn=512, out_dtype=BF16,
                              conv=(hy_conv_w, hy_conv_b), seq_len=sl)
                    a = _hyena_core(u, fwd, bwd, spec, hy_bias[j], nb, sl, tq, tc)
                    x = _outproj(x, a, mods, crow, hy_w_out, j, ln_g4, ln_b4, layer)
            x = _ffn(x, mods, crow, ffn_w_in, ffn_w_out, ln_g4, ln_b4, layer, 1)
        outs[name] = x

    y_prompt = outs["ctx"].reshape(batch, seq, D_MODEL)
    y_sample = outs["lat"].reshape(dec_batch, dec_seq, D_MODEL)
    return (y_prompt, y_sample, jnp.concatenate(new_k, axis=1), jnp.concatenate(new_v, axis=1))
```

```python
import functools
import math

import numpy as np
import jax
import jax.numpy as jnp
from jax import lax
from jax.experimental import pallas as pl
from jax.experimental.pallas import tpu as pltpu

D_MODEL = 1024
DEPTH = 2
GRID_W = 64
N_HEADS = 16
HEAD_DIM = D_MODEL // N_HEADS
WIN_ROWS = 8
WIN_COLS = 16
D_FF = 2816
N_MOD = 9
HY_BANDS = 16
HY_EMB = 1 + 2 * HY_BANDS
HY_FILTER_W = 64
HY_FAST_DECAY = 0.3
HY_SLOW_DECAY = 1.5
HY_TARGET = 1e-2
ALPHA = (2 * DEPTH) ** 0.25
LN_EPS = 1e-5

F32 = jnp.float32
BF16 = jnp.bfloat16
NEG = -1e30
LANES = 128
VMEM_LIMIT = 56 * 1024 * 1024
COND_ROWS = 8

Q_ROWS = 8
K_ROWS = 16
BAND_ROWS = 10


def _cparams(*sem):
    return pltpu.CompilerParams(dimension_semantics=sem, vmem_limit_bytes=VMEM_LIMIT)


def _dot(a, b):
    return jnp.dot(a, b, preferred_element_type=F32)


def _dot_nt(a, b):
    return lax.dot_general(a, b, (((1,), (1,)), ((), ())), preferred_element_type=F32)


def _layer_norm(y, g, b):
    mu = jnp.mean(y, axis=-1, keepdims=True)
    yc = y - mu
    var = jnp.mean(yc * yc, axis=-1, keepdims=True)
    return yc * lax.rsqrt(var + LN_EPS) * g + b


def _mods_kernel(c_ref, w_ref, b_ref, o_ref):
    c = c_ref[...]
    s = c * jax.nn.sigmoid(c)
    o_ref[...] = _dot(s.astype(BF16), w_ref[...].astype(BF16)) + b_ref[...]


def _mods(conds, mod_w, mod_b):
    tn = 1024
    nd = N_MOD * D_MODEL
    return pl.pallas_call(
        _mods_kernel,
        grid=(DEPTH, nd // tn),
        in_specs=[
            pl.BlockSpec((COND_ROWS, D_MODEL), lambda l, j: (0, 0)),
            pl.BlockSpec((None, D_MODEL, tn), lambda l, j: (l, 0, j)),
            pl.BlockSpec((None, 1, tn), lambda l, j: (l, 0, j)),
        ],
        out_specs=pl.BlockSpec((None, COND_ROWS, tn), lambda l, j: (l, 0, j)),
        out_shape=jax.ShapeDtypeStruct((DEPTH, COND_ROWS, nd), F32),
        compiler_params=_cparams("arbitrary", "arbitrary"),
        name="mods",
    )(conds, mod_w, mod_b.reshape(DEPTH, 1, nd))


def _ffn_kernel(x_ref, mod_ref, wg_ref, wu_ref, wo_ref, g_ref, b_ref, o_ref,
                wg_s, wu_s, wo_s, h_s, acc_s, *, mi, nk):
    s = pl.program_id(0)
    sh = mod_ref[3 * mi:3 * mi + 1, :]
    sc = mod_ref[3 * mi + 1:3 * mi + 2, :]
    gt = mod_ref[3 * mi + 2:3 * mi + 3, :]

    def swiglu_chunk(h, c):
        gate = _dot(h, wg_s[c])
        up = _dot(h, wu_s[c])
        return _dot((gate * jax.nn.sigmoid(gate) * up).astype(BF16), wo_s[c])

    def cast_chunk():
        wg_s[s] = wg_ref[...].astype(BF16)
        wu_s[s] = wu_ref[...].astype(BF16)
        wo_s[s] = wo_ref[...].astype(BF16)

    def finish(f):
        y = ALPHA * x_ref[...] + (0.5 * gt) * f
        o_ref[...] = _layer_norm(y, g_ref[...], b_ref[...])

    @pl.when(s == 0)
    def _():
        cast_chunk()
        h_s[...] = (x_ref[...] * (1.0 + sc) + sh).astype(BF16)
        acc_s[...] = jnp.zeros_like(acc_s)

    @pl.when((s >= 1) & (s < nk))
    def _():
        cast_chunk()
        acc_s[...] += swiglu_chunk(h_s[...], s - 1)

    @pl.when(s == nk)
    def _():
        finish(acc_s[...] + swiglu_chunk(h_s[...], nk - 1))

    @pl.when(s > nk)
    def _():
        h = (x_ref[...] * (1.0 + sc) + sh).astype(BF16)
        f = swiglu_chunk(h, 0)
        for c in range(1, nk):
            f = f + swiglu_chunk(h, c)
        finish(f)


def _resident(shape, index_map):
    return pl.BlockSpec(shape, index_map, pipeline_mode=pl.Buffered(1))


def _ffn(x, mods, crow, w_in, w_out, ln_g, ln_b, layer, which, tm=1024, tf=256):
    m = x.shape[0]
    mi = 0 if which == 0 else 2
    nk = D_FF // tf
    tile = lambda s: jnp.maximum(s - nk, 0)
    chunk = lambda s: jnp.minimum(s, nk - 1)
    return pl.pallas_call(
        functools.partial(_ffn_kernel, mi=mi, nk=nk),
        grid=(nk + m // tm,),
        in_specs=[
            pl.BlockSpec((tm, D_MODEL), lambda s: (tile(s), 0)),
            pl.BlockSpec((None, None, N_MOD, D_MODEL), lambda s: (layer, crow(tile(s) * tm), 0, 0)),
            pl.BlockSpec((None, None, D_MODEL, tf), lambda s: (layer, which, 0, chunk(s))),
            pl.BlockSpec((None, None, D_MODEL, tf), lambda s: (layer, which, 0, chunk(s) + nk)),
            pl.BlockSpec((None, None, tf, D_MODEL), lambda s: (layer, which, chunk(s), 0)),
            pl.BlockSpec((None, None, 1, D_MODEL), lambda s: (layer, mi, 0, 0)),
            pl.BlockSpec((None, None, 1, D_MODEL), lambda s: (layer, mi, 0, 0)),
        ],
        out_specs=pl.BlockSpec((tm, D_MODEL), lambda s: (tile(s), 0)),
        out_shape=jax.ShapeDtypeStruct((m, D_MODEL), F32),
        scratch_shapes=[pltpu.VMEM((nk, D_MODEL, tf), BF16), pltpu.VMEM((nk, D_MODEL, tf), BF16),
                        pltpu.VMEM((nk, tf, D_MODEL), BF16), pltpu.VMEM((tm, D_MODEL), BF16),
                        pltpu.VMEM((tm, D_MODEL), F32)],
        compiler_params=_cparams("arbitrary"),
        name="ffn",
    )(x, mods, w_in, w_in, w_out, ln_g, ln_b)


def _proj_kernel(x_ref, mod_ref, w_ref, *rest, seq_len):
    if seq_len:
        cw_ref, cb_ref, o_ref, h_ref = rest
    else:
        o_ref, h_ref = rest
    j = pl.program_id(1)

    @pl.when(j == 0)
    def _():
        sh = mod_ref[3:4, :]
        sc = mod_ref[4:5, :]
        h_ref[...] = (x_ref[...] * (1.0 + sc) + sh).astype(BF16)

    u = _dot(h_ref[...], w_ref[...])
    if seq_len:
        tm = u.shape[0]
        pos = lax.broadcasted_iota(jnp.int32, (tm, 1), 0) % seq_len
        prev = jnp.where(pos == 0, 0.0, pltpu.roll(u, 1, 0))
        nxt = jnp.where(pos == seq_len - 1, 0.0, pltpu.roll(u, tm - 1, 0))
        u = prev * cw_ref[0:1, :] + u * cw_ref[1:2, :] + nxt * cw_ref[2:3, :] + cb_ref[...]
    o_ref[...] = u.astype(o_ref.dtype)


def _proj(x, mods, crow, w, layer, widx, *, tm, tn, out_dtype, conv=None, seq_len=0):
    m = x.shape[0]
    npg = D_MODEL // tn
    in_specs = [
        pl.BlockSpec((tm, D_MODEL), lambda i, j: (i, 0)),
        pl.BlockSpec((None, None, N_MOD, D_MODEL), lambda i, j: (layer, crow(i * tm), 0, 0)),
        pl.BlockSpec((None, D_MODEL, tn), lambda i, j: (widx, 0, j)),
    ]
    args = [x, mods, w]
    if conv is not None:
        cw, cb = conv
        in_specs += [
            pl.BlockSpec((None, 3, tn), lambda i, j: (widx, 0, j)),
            pl.BlockSpec((None, 1, tn), lambda i, j: (widx, 0, j)),
        ]
        args += [cw, cb.reshape(cb.shape[0], 1, cb.shape[1])]
    return pl.pallas_call(
        functools.partial(_proj_kernel, seq_len=seq_len),
        grid=(m // tm, 3 * npg),
        in_specs=in_specs,
        out_specs=pl.BlockSpec((None, tm, tn), lambda i, j: (j // npg, i, j % npg)),
        out_shape=jax.ShapeDtypeStruct((3, m, D_MODEL), out_dtype),
        scratch_shapes=[pltpu.VMEM((tm, D_MODEL), BF16)],
        compiler_params=_cparams("arbitrary", "arbitrary"),
        name="proj_conv" if seq_len else "proj",
    )(*args)


def _outproj_kernel(x_ref, a_ref, mod_ref, w_ref, g_ref, b_ref, o_ref):
    mix = _dot(a_ref[...], w_ref[...])
    y = ALPHA * x_ref[...] + mod_ref[5:6, :] * mix
    o_ref[...] = _layer_norm(y, g_ref[...], b_ref[...])


def _outproj(x, a, mods, crow, w, widx, ln_g, ln_b, layer, tm=1024):
    m = x.shape[0]
    return pl.pallas_call(
        _outproj_kernel,
        grid=(m // tm,),
        in_specs=[
            pl.BlockSpec((tm, D_MODEL), lambda i: (i, 0)),
            pl.BlockSpec((tm, D_MODEL), lambda i: (i, 0)),
            pl.BlockSpec((None, None, N_MOD, D_MODEL), lambda i: (layer, crow(i * tm), 0, 0)),
            pl.BlockSpec((None, D_MODEL, D_MODEL), lambda i: (widx, 0, 0)),
            pl.BlockSpec((None, None, 1, D_MODEL), lambda i: (layer, 1, 0, 0)),
            pl.BlockSpec((None, None, 1, D_MODEL), lambda i: (layer, 1, 0, 0)),
        ],
        out_specs=pl.BlockSpec((tm, D_MODEL), lambda i: (i, 0)),
        out_shape=jax.ShapeDtypeStruct((m, D_MODEL), F32),
        compiler_params=_cparams("arbitrary"),
        name="outproj",
    )(x, a, mods, w, ln_g, ln_b)


def _lane_tiles(a):
    return [a[:, i:i + LANES] for i in range(0, a.shape[1], LANES)]


def _head_masks():
    lane = lax.broadcasted_iota(jnp.int32, (1, LANES), 1)
    return lane < HEAD_DIM, lane >= HEAD_DIM


def _attn_ctx_kernel(x_ref, mod_ref, wqkv_ref, wo_ref, g_ref, b_ref, o_ref, ko_ref, vo_ref,
                     q_scr, k_scr, v_scr, a_scr, *, seq, nseq):
    x = x_ref[...]
    h = (x * (1.0 + mod_ref[4:5, :]) + mod_ref[3:4, :]).astype(BF16)
    q_scr[...] = (_dot(h, wqkv_ref[:, :D_MODEL]) * HEAD_DIM ** -0.5).astype(BF16)
    k_scr[...] = _dot(h, wqkv_ref[:, D_MODEL:2 * D_MODEL])
    v_scr[...] = _dot(h, wqkv_ref[:, 2 * D_MODEL:])
    masks = _head_masks()
    for s in range(nseq):
        rows = slice(s * seq, (s + 1) * seq)
        ko_ref[s] = k_scr[rows, :].reshape(seq, N_HEADS, HEAD_DIM)
        vo_ref[s] = v_scr[rows, :].reshape(seq, N_HEADS, HEAD_DIM)
        for p in range(D_MODEL // LANES):
            sl = slice(p * LANES, (p + 1) * LANES)
            q2 = q_scr[rows, sl]
            k2 = k_scr[rows, sl]
            v2 = v_scr[rows, sl]
            o2 = jnp.zeros(q2.shape, F32)
            for hm in masks:
                kh = jnp.where(hm, k2, 0.0).astype(BF16)
                vh = jnp.where(hm, v2, 0.0).astype(BF16)
                sc = _dot_nt(q2, kh)
                e = jnp.exp(sc - jnp.max(sc, axis=-1, keepdims=True))
                inv = 1.0 / jnp.sum(e, axis=-1, keepdims=True)
                o2 = o2 + _dot(e.astype(BF16), vh) * inv
            a_scr[rows, sl] = o2.astype(BF16)
    y = ALPHA * x + mod_ref[5:6, :] * _dot(a_scr[...], wo_ref[...])
    o_ref[...] = _layer_norm(y, g_ref[...], b_ref[...])


def _attn_ctx(x, mods, crow, w_qkv, w_o, widx, ln_g, ln_b, layer, batch, seq, nseq=2):
    rows = nseq * seq
    cache_spec = pl.BlockSpec((nseq, None, seq, N_HEADS, HEAD_DIM), lambda i: (i, 0, 0, 0, 0))
    cache_shape = jax.ShapeDtypeStruct((batch, 1, seq, N_HEADS, HEAD_DIM), F32)
    return pl.pallas_call(
        functools.partial(_attn_ctx_kernel, seq=seq, nseq=nseq),
        grid=(batch // nseq,),
        in_specs=[
            pl.BlockSpec((rows, D_MODEL), lambda i: (i, 0)),
            pl.BlockSpec((None, None, N_MOD, D_MODEL), lambda i: (layer, crow(i * rows), 0, 0)),
            _resident((None, D_MODEL, 3 * D_MODEL), lambda i: (widx, 0, 0)),
            _resident((None, D_MODEL, D_MODEL), lambda i: (widx, 0, 0)),
            pl.BlockSpec((None, None, 1, D_MODEL), lambda i: (layer, 1, 0, 0)),
            pl.BlockSpec((None, None, 1, D_MODEL), lambda i: (layer, 1, 0, 0)),
        ],
        out_specs=[pl.BlockSpec((rows, D_MODEL), lambda i: (i, 0)), cache_spec, cache_spec],
        out_shape=[jax.ShapeDtypeStruct((batch * seq, D_MODEL), F32), cache_shape, cache_shape],
        scratch_shapes=[pltpu.VMEM((rows, D_MODEL), BF16), pltpu.VMEM((rows, D_MODEL), F32),
                        pltpu.VMEM((rows, D_MODEL), F32), pltpu.VMEM((rows, D_MODEL), BF16)],
        compiler_params=_cparams("arbitrary"),
        name="attn_ctx",
    )(x, mods, w_qkv, w_o, ln_g, ln_b)


def _nbr_geometry(rows):
    kr = min(WIN_ROWS, rows)
    blocks = []
    for blk in range(rows // Q_ROWS):
        r0 = blk * Q_ROWS
        lo = min(max(r0 - kr // 2, 0), rows - kr)
        key_start = min(lo, rows - K_ROWS)
        bands = []
        for r in range(r0, r0 + Q_ROWS):
            rs = min(max(r - kr // 2, 0), rows - kr)
            a = rs - key_start
            band_start = min(a - a % 2, K_ROWS - BAND_ROWS)
            off = a - band_start
            assert 0 <= band_start and 0 <= off and off + kr <= BAND_ROWS
            bands.append((band_start, off, rs - r + WIN_ROWS - 1))
        blocks.append((key_start, bands))
    return kr, blocks


def _nbr_bias_rows(rpb):
    pad = GRID_W - (2 * WIN_COLS - 1)
    row = jnp.pad(rpb, ((0, 0), (0, 0), (0, pad)))
    return jnp.concatenate([row[:, :-1], row[:, 1:]], axis=-1)


def _attn_nbr_kernel(q_ref, k_ref, v_ref, kc_ref, vc_ref, rp_ref, o_ref,
                     s_ref, sc_ref, p_ref, pc_ref, inv_ref, tab_ref, *, blocks, kr):
    masks = _head_masks()
    left = masks[0]
    qn = Q_ROWS * GRID_W
    kn = K_ROWS * GRID_W
    qcol = lax.broadcasted_iota(jnp.int32, (GRID_W, LANES), 0)
    kcol = lax.broadcasted_iota(jnp.int32, (GRID_W, LANES), 1) % GRID_W
    cstart = jnp.clip(qcol - WIN_COLS // 2, 0, GRID_W - WIN_COLS)
    col_ok = (kcol >= cstart) & (kcol < cstart + WIN_COLS)
    for hh in range(2):
        for d in range(2 * WIN_ROWS - 2):
            row = jnp.broadcast_to(rp_ref[hh, d:d + 1, :], (GRID_W, LANES))
            toep = pltpu.roll(row, LANES - (WIN_COLS - 1), 1, stride=1, stride_axis=0)
            tab_ref[hh, d] = jnp.where(col_ok, toep, NEG)
    for blk, (key_start, bands) in enumerate(blocks):
        q2 = (q_ref[blk * qn:(blk + 1) * qn, :] * HEAD_DIM ** -0.5).astype(BF16)
        k2 = k_ref[key_start * GRID_W:key_start * GRID_W + kn, :]
        v2 = v_ref[key_start * GRID_W:key_start * GRID_W + kn, :]
        o2 = jnp.zeros(q2.shape, F32)
        p_ref[...] = jnp.zeros_like(p_ref)
        for hh, hm in enumerate(masks):
            kh = jnp.where(hm, k2, 0.0).astype(BF16)
            vh = jnp.where(hm, v2, 0.0).astype(BF16)
            kch = jnp.where(hm, kc_ref[...], 0.0).astype(BF16)
            vch = jnp.where(hm, vc_ref[...], 0.0).astype(BF16)
            s_ref[...] = _dot_nt(q2, kh)
            sc_ref[...] = _dot_nt(q2, kch)
            for rl, (band_start, off, drow) in enumerate(bands):
                rsl = slice(rl * GRID_W, (rl + 1) * GRID_W)
                tiles = []
                for j in range(0, BAND_ROWS, 2):
                    ok0, ok1 = off <= j < off + kr, off <= j + 1 < off + kr
                    if not (ok0 or ok1):
                        continue
                    bias = tab_ref[hh, drow - off + j]
                    if not ok0:
                        bias = jnp.where(left, NEG, bias)
                    if not ok1:
                        bias = jnp.where(left, bias, NEG)
                    csl = slice((band_start + j) * GRID_W, (band_start + j + 2) * GRID_W)
                    tiles.append((csl, s_ref[rsl, csl] + bias))
                sc = sc_ref[rsl, :]
                top = functools.reduce(jnp.maximum, _lane_tiles(sc) + [sb for _, sb in tiles])
                mx = jnp.max(top, axis=-1, keepdims=True)
                ec = jnp.exp(sc - mx)
                tot = functools.reduce(jnp.add, _lane_tiles(ec))
                for csl, sb in tiles:
                    eb = jnp.exp(sb - mx)
                    tot = tot + eb
                    p_ref[rsl, csl] = eb.astype(BF16)
                inv_ref[rsl, :] = 1.0 / jnp.sum(tot, axis=-1, keepdims=True)
                pc_ref[rsl, :] = ec.astype(BF16)
            o2 = o2 + (_dot(p_ref[...], vh) + _dot(pc_ref[...], vch)) * inv_ref[...]
        o_ref[blk * qn:(blk + 1) * qn, :] = o2.astype(o_ref.dtype)


def _attn_nbr(qkv, kc, vc, rpb, batch, seq):
    rows = seq // GRID_W
    kr, blocks = _nbr_geometry(rows)
    n_pairs = D_MODEL // LANES
    n_d = 2 * WIN_ROWS - 2
    rp = _nbr_bias_rows(rpb).reshape(n_pairs, 2, n_d, LANES)
    s_ctx = kc.shape[1]
    qspec = lambda g: pl.BlockSpec((None, seq, LANES), lambda b, p: (g, b, p))
    cspec = pl.BlockSpec((None, s_ctx, LANES), lambda b, p: (b, 0, p))
    qn, kn = Q_ROWS * GRID_W, K_ROWS * GRID_W
    return pl.pallas_call(
        functools.partial(_attn_nbr_kernel, blocks=blocks, kr=kr),
        grid=(batch, n_pairs),
        in_specs=[qspec(0), qspec(1), qspec(2), cspec, cspec,
                  pl.BlockSpec((None, 2, n_d, LANES), lambda b, p: (p, 0, 0, 0))],
        out_specs=pl.BlockSpec((seq, LANES), lambda b, p: (b, p)),
        out_shape=jax.ShapeDtypeStruct((batch * seq, D_MODEL), BF16),
        scratch_shapes=[pltpu.VMEM((qn, kn), F32), pltpu.VMEM((qn, s_ctx), F32),
                        pltpu.VMEM((qn, kn), BF16), pltpu.VMEM((qn, s_ctx), BF16),
                        pltpu.VMEM((qn, 1), F32), pltpu.VMEM((2, n_d, GRID_W, LANES), F32)],
        compiler_params=_cparams("arbitrary", "arbitrary"),
        name="attn_nbr",
    )(qkv, qkv, qkv, kc, vc, rp)


def _dft_tables(seq_len, tq, tr):
    n = 2 * seq_len
    f = np.arange(seq_len, dtype=np.float64)[:, None] + 0.5
    s = np.arange(seq_len, dtype=np.float64)[None, :]
    ang = 2.0 * np.pi * f * s / n
    re = np.cos(ang).reshape(seq_len // tq, tq, seq_len)
    im = (-np.sin(ang)).reshape(seq_len // tq, tq, seq_len)
    fwd = np.concatenate([re, im], axis=1).astype(np.float32)
    bwd = fwd.reshape(2 * seq_len, seq_len).T
    return fwd, np.ascontiguousarray(bwd).reshape(seq_len // tr, tr, 2 * seq_len)


def _filter_features(seq_len):
    pos = np.arange(seq_len, dtype=np.float64)[:, None]
    t = pos / max(seq_len - 1, 1)
    bands = np.linspace(1e-4, HY_BANDS - 1, HY_BANDS)
    ang = 2.0 * np.pi * pos / seq_len * bands
    z = np.concatenate([t, np.cos(ang), -np.sin(ang)], axis=-1)
    zp = np.zeros((seq_len, HY_FILTER_W), np.float64)
    zp[:, :HY_EMB] = z
    deltas = np.abs(np.linspace(math.log(HY_TARGET) / HY_SLOW_DECAY, math.log(HY_TARGET) / HY_FAST_DECAY, D_MODEL))
    return zp.astype(np.float32), t.astype(np.float32), deltas.astype(np.float32)[None, :]


def _filter_kernel(z_ref, t_ref, dl_ref, w1_ref, b1_ref, f1_ref, w2_ref, b2_ref, f2_ref, w3_ref, b3_ref, o_ref,
                   hid_ref, decay_ref):
    hp = lax.Precision.HIGHEST

    @pl.when(pl.program_id(0) == 0)
    def _():
        hid = jnp.sin(f1_ref[...] * (jnp.dot(z_ref[...], w1_ref[...], precision=hp, preferred_element_type=F32) + b1_ref[...]))
        hid_ref[...] = jnp.sin(f2_ref[...] * (jnp.dot(hid, w2_ref[...], precision=hp, preferred_element_type=F32) + b2_ref[...]))
        decay_ref[...] = jnp.exp(-t_ref[...] * dl_ref[...])

    hid, w3 = hid_ref[...], w3_ref[...]
    hid_hi, w3_hi = hid.astype(BF16), w3.astype(BF16)
    hid_lo = (hid - hid_hi.astype(F32)).astype(BF16)
    w3_lo = (w3 - w3_hi.astype(F32)).astype(BF16)
    filt = _dot(hid_hi, w3_hi) + (_dot(hid_lo, w3_hi) + _dot(hid_hi, w3_lo)) + b3_ref[...]
    decay = decay_ref[...]
    hf = filt[:, :D_MODEL] * decay
    hb = filt[:, D_MODEL:] * decay
    row = lax.broadcasted_iota(jnp.int32, (hb.shape[0], 1), 0)
    hb = jnp.where(row == 0, 0.0, hb)
    o_ref[:, :D_MODEL] = (hf + hb).astype(o_ref.dtype)
    o_ref[:, D_MODEL:] = (hf - hb).astype(o_ref.dtype)


def _filters(seq_len, w1, b1, f1, w2, b2, f2, w3, b3):
    z, t, deltas = _filter_features(seq_len)
    w1p = jnp.zeros((HY_FILTER_W, HY_FILTER_W), F32).at[:HY_EMB].set(w1)
    full = lambda shape: pl.BlockSpec(shape, lambda o: (0,) * len(shape))
    row = lambda a: a.reshape(1, -1)
    return pl.pallas_call(
        _filter_kernel,
        grid=(2,),
        in_specs=[full((seq_len, HY_FILTER_W)), full((seq_len, 1)), full((1, D_MODEL)),
                  full((HY_FILTER_W, HY_FILTER_W)), full((1, HY_FILTER_W)), full((1, HY_FILTER_W)),
                  full((HY_FILTER_W, HY_FILTER_W)), full((1, HY_FILTER_W)), full((1, HY_FILTER_W)),
                  pl.BlockSpec((HY_FILTER_W, 2 * D_MODEL), lambda o: (0, o)),
                  pl.BlockSpec((1, 2 * D_MODEL), lambda o: (0, o))],
        out_specs=pl.BlockSpec((None, seq_len, 2 * D_MODEL), lambda o: (o, 0, 0)),
        out_shape=jax.ShapeDtypeStruct((2, seq_len, 2 * D_MODEL), BF16),
        scratch_shapes=[pltpu.VMEM((seq_len, HY_FILTER_W), F32), pltpu.VMEM((seq_len, D_MODEL), F32)],
        compiler_params=_cparams("arbitrary"),
        name="hyena_filters",
    )(jnp.asarray(z),jnp.asarray(t), jnp.asarray(deltas), w1p, row(b1), row(f1), w2, row(b2), row(f2), w3, row(b3))


def _spectrum_kernel(g_ref, hs_ref, hd_ref, o_ref, *, tq, scale):
    o_ref[:tq, :] = scale * _dot(g_ref[:tq, :], hs_ref[...])
    o_ref[tq:, :] = scale * _dot(g_ref[tq:, :], hd_ref[...])


def _spectrum(fwd, filt, seq_len, tq, tc):
    nf = seq_len // tq
    ncb = D_MODEL // tc
    return pl.pallas_call(
        functools.partial(_spectrum_kernel, tq=tq, scale=1.0 / seq_len),
        grid=(2, ncb, nf),
        in_specs=[pl.BlockSpec((None, 2 * tq, seq_len), lambda o, c, f: (f, 0, 0)),
                  pl.BlockSpec((None, seq_len, tc), lambda o, c, f: (o, 0, c)),
                  pl.BlockSpec((None, seq_len, tc), lambda o, c, f: (o, 0, c + ncb))],
        out_specs=pl.BlockSpec((None, None, 2 * tq, tc), lambda o, c, f: (o, f, 0, c)),
        out_shape=jax.ShapeDtypeStruct((2, nf, 2 * tq, D_MODEL), F32),
        compiler_params=_cparams("arbitrary", "arbitrary", "arbitrary"),
        name="hyena_spectrum",
    )(fwd, filt, filt)


def _hyena_kernel(v_ref, gate_ref, g_ref, gt_ref, k_ref, b_ref, o_ref, z_ref, p_ref, *, tq, nf, tr):
    o = pl.program_id(2)
    st = pl.program_id(3)

    @pl.when((o == 0) & (st == 0))
    def _():
        z_ref[...] = v_ref[...].astype(BF16)

    @pl.when(st < nf)
    def _():
        zf = _dot(g_ref[...], z_ref[...])
        zr, zi = zf[:tq], zf[tq:]
        kr, ki = k_ref[:tq, :], k_ref[tq:, :]
        prod = jnp.concatenate([zr * kr - zi * ki, zr * ki + zi * kr], axis=0)
        p_ref[pl.ds(pl.multiple_of(st * 2 * tq, 2 * tq), 2 * tq), :] = prod.astype(BF16)

    @pl.when(st >= nf)
    def _():
        rows = pl.ds(pl.multiple_of((st - nf) * tr, tr), tr)
        y = _dot(gt_ref[...], p_ref[...]) + z_ref[rows, :].astype(F32) * b_ref[...]
        gated = y * gate_ref[rows, :].astype(F32)

        @pl.when(o == 0)
        def _():
            z_ref[rows, :] = gated.astype(BF16)

        @pl.when(o == 1)
        def _():
            o_ref[rows, :] = gated.astype(o_ref.dtype)


def _hyena_core(u, fwd, bwd_rows, spec, bias, batch, seq_len, tq, tc):
    nf = seq_len // tq
    nr, tr, _ = bwd_rows.shape
    ncb = D_MODEL // tc
    fidx = lambda st: jnp.minimum(st, nf - 1)
    return pl.pallas_call(
        functools.partial(_hyena_kernel, tq=tq, nf=nf, tr=tr),
        grid=(batch, ncb, 2, nf + nr),
        in_specs=[pl.BlockSpec((None, seq_len, tc), lambda b, c, o, st: (0, b, c)),
                  pl.BlockSpec((None, seq_len, tc), lambda b, c, o, st: (1 + o, b, c)),
                  pl.BlockSpec((None, 2 * tq, seq_len), lambda b, c, o, st: (fidx(st), 0, 0)),
                  pl.BlockSpec((None, tr, 2 * seq_len), lambda b, c, o, st: (jnp.maximum(st - nf, 0), 0, 0)),
                  pl.BlockSpec((None, None, 2 * tq, tc), lambda b, c, o, st: (o, fidx(st), 0, c)),
                  pl.BlockSpec((None, 1, tc), lambda b, c, o, st: (o, 0, c))],
        out_specs=pl.BlockSpec((seq_len, tc), lambda b, c, o, st: (b, c)),
        out_shape=jax.ShapeDtypeStruct((batch * seq_len, D_MODEL), BF16),
        scratch_shapes=[pltpu.VMEM((seq_len, tc), BF16), pltpu.VMEM((2 * seq_len, tc), BF16)],
        compiler_params=_cparams("arbitrary", "arbitrary", "arbitrary", "arbitrary"),
        name="hyena_conv",
    )(u, u, fwd, bwd_rows, spec, bias.reshape(2, 1, D_MODEL))


def _hyena_layer_kernel(x_ref, mod_ref, win_ref, cw_ref, cb_ref, g_ref, gt_ref, k_ref, hb_ref, wout_ref,
                        lg_ref, lb_ref, o_ref, u_scr, *, seq_len, nseq, tn):
    x = x_ref[...]
    h = (x * (1.0 + mod_ref[4:5, :]) + mod_ref[3:4, :]).astype(BF16)
    rows = nseq * seq_len
    pos = lax.broadcasted_iota(jnp.int32, (rows, 1), 0) % seq_len
    first, last = pos == 0, pos == seq_len - 1
    for g in range(3):
        for cc in range(D_MODEL // tn):
            cols = slice(g * D_MODEL + cc * tn, g * D_MODEL + (cc + 1) * tn)
            u = _dot(h, win_ref[:, cols])
            prev = jnp.where(first, 0.0, pltpu.roll(u, 1, 0))
            nxt = jnp.where(last, 0.0, pltpu.roll(u, rows - 1, 0))
            u = prev * cw_ref[0:1, cols] + u * cw_ref[1:2, cols] + nxt * cw_ref[2:3, cols] + cb_ref[:, cols]
            u_scr[g, :, cc * tn:(cc + 1) * tn] = u.astype(BF16)
    for o in range(2):
        kr, ki = k_ref[o, :seq_len, :], k_ref[o, seq_len:, :]
        for s in range(nseq):
            r = slice(s * seq_len, (s + 1) * seq_len)
            z = u_scr[0, r, :]
            zf = _dot(g_ref[...], z)
            zr, zi = zf[:seq_len], zf[seq_len:]
            prod = jnp.concatenate([zr * kr - zi * ki, zr * ki + zi * kr], axis=0).astype(BF16)
            y = _dot(gt_ref[...], prod) + z.astype(F32) * hb_ref[o]
            u_scr[0, r, :] = (y * u_scr[1 + o, r, :].astype(F32)).astype(BF16)
    y = ALPHA * x + mod_ref[5:6, :] * _dot(u_scr[0], wout_ref[...])
    o_ref[...] = _layer_norm(y, lg_ref[...], lb_ref[...])


def _hyena_layer(x, mods, crow, w_in, w_out, widx, conv_w, conv_b, fwd, bwd, spec, bias, ln_g, ln_b, layer,
                 seq_len, nseq, tn=512):
    m = x.shape[0]
    rows = nseq * seq_len
    assert fwd.shape[0] == 1
    return pl.pallas_call(
        functools.partial(_hyena_layer_kernel, seq_len=seq_len, nseq=nseq, tn=tn),
        grid=(m // rows,),
        in_specs=[
            pl.BlockSpec((rows, D_MODEL), lambda i: (i, 0)),
            pl.BlockSpec((None, None, N_MOD, D_MODEL), lambda i: (layer, crow(i * rows), 0, 0)),
            _resident((None, D_MODEL, 3 * D_MODEL), lambda i: (widx, 0, 0)),
            pl.BlockSpec((None, 3, 3 * D_MODEL), lambda i: (widx, 0, 0)),
            pl.BlockSpec((None, 1, 3 * D_MODEL), lambda i: (widx, 0, 0)),
            pl.BlockSpec((None, 2 * seq_len, seq_len), lambda i: (0, 0, 0)),
            pl.BlockSpec((None, seq_len, 2 * seq_len), lambda i: (0, 0, 0)),
            _resident((2, None, 2 * seq_len, D_MODEL), lambda i: (0, 0, 0, 0)),
            pl.BlockSpec((2, 1, D_MODEL), lambda i: (0, 0, 0)),
            _resident((None, D_MODEL, D_MODEL), lambda i: (widx, 0, 0)),
            pl.BlockSpec((None, None, 1, D_MODEL), lambda i: (layer, 1, 0, 0)),
            pl.BlockSpec((None, None, 1, D_MODEL), lambda i: (layer, 1, 0, 0)),
        ],
        out_specs=pl.BlockSpec((rows, D_MODEL), lambda i: (i, 0)),
        out_shape=jax.ShapeDtypeStruct((m, D_MODEL), F32),
        scratch_shapes=[pltpu.VMEM((3, rows, D_MODEL), BF16)],
        compiler_params=_cparams("arbitrary"),
        name="hyena_layer",
    )(x, mods, w_in, conv_w, conv_b.reshape(conv_b.shape[0], 1, conv_b.shape[1]), fwd, bwd, spec,
      bias.reshape(2, 1, D_MODEL), w_out, ln_g, ln_b)


def _hyena_tiles(seq_len, batch):
    tq = tr = min(seq_len, 512)
    tc = D_MODEL if seq_len <= 256 else 512
    nseq = math.gcd(batch, max(1, 1024 // seq_len))
    return tq, tr, tc, nseq


def kernel(x_prompt, x_sample, cache_k, cache_v, c, c_ctx, mod_w, mod_b, ln_g, ln_b, ffn_w_in, ffn_w_out,
           attn_w_qkv, attn_w_o, attn_rpb, hy_w_in, hy_conv_w, hy_conv_b, hy_f_w1, hy_f_b1, hy_f_freq1,
           hy_f_w2, hy_f_b2, hy_f_freq2, hy_f_w3, hy_f_b3, hy_bias, hy_w_out):
    batch, seq, _ = x_prompt.shape
    dec_batch, dec_seq, _ = x_sample.shape
    assert dec_batch + 1 <= COND_ROWS

    conds = jnp.zeros((COND_ROWS, D_MODEL), F32).at[0].set(c_ctx).at[1:1 + dec_batch].set(c)
    mods = _mods(conds, mod_w, mod_b).reshape(DEPTH, COND_ROWS, N_MOD, D_MODEL)
    ln_g4 = ln_g.reshape(DEPTH, 3, 1, D_MODEL)
    ln_b4 = ln_b.reshape(DEPTH, 3, 1, D_MODEL)

    streams = {
        "ctx": dict(x=x_prompt.reshape(batch * seq, D_MODEL), batch=batch, seq=seq, crow=lambda t: 0),
        "lat": dict(x=x_sample.reshape(dec_batch * dec_seq, D_MODEL), batch=dec_batch, seq=dec_seq,
                    crow=lambda t: 1 + t // dec_seq),
    }
    hyena_consts = {}
    for name, st in streams.items():
        sl = st["seq"]
        tq, tr, tc, nseq = _hyena_tiles(sl, st["batch"])
        fwd, bwd = _dft_tables(sl, tq, tr)
        hyena_consts[name] = (jnp.asarray(fwd).astype(BF16), jnp.asarray(bwd).astype(BF16), tq, tc, nseq)

    attn_w_qkv, attn_w_o, hy_w_in, hy_w_out = (w.astype(BF16) for w in (attn_w_qkv, attn_w_o, hy_w_in, hy_w_out))

    new_k, new_v = [], []
    outs = {}
    for name, st in streams.items():
        x, nb, sl, crow = st["x"], st["batch"], st["seq"], st["crow"]
        for layer in range(DEPTH):
            j = layer // 2
            x = _ffn(x, mods, crow, ffn_w_in, ffn_w_out, ln_g4, ln_b4, layer, 0)
            if layer % 2 == 0:
                if name == "ctx":
                    x, k_new, v_new = _attn_ctx(x, mods, crow, attn_w_qkv, attn_w_o, j, ln_g4, ln_b4, layer, nb, sl)
                    new_k.append(k_new)
                    new_v.append(v_new)
                else:
                    qkv = _proj(x, mods, crow, attn_w_qkv, layer, j, tm=1024, tn=1024, out_dtype=BF16)
                    kc = cache_k[:, j].reshape(nb, -1, D_MODEL)
                    vc = cache_v[:, j].reshape(nb, -1, D_MODEL)
                    a = _attn_nbr(qkv, kc, vc, attn_rpb[j], nb, sl)
                    x = _outproj(x, a, mods, crow, attn_w_o, j, ln_g4, ln_b4, layer)
            else:
                fwd, bwd, tq, tc, nseq = hyena_consts[name]
                filt = _filters(sl, hy_f_w1[j], hy_f_b1[j], hy_f_freq1[j], hy_f_w2[j], hy_f_b2[j], hy_f_freq2[j],
                                hy_f_w3[j], hy_f_b3[j])
                spec = _spectrum(fwd, filt, sl, tq, tc)
                if sl == tq:
                    x = _hyena_layer(x, mods, crow, hy_w_in, hy_w_out, j, hy_conv_w, hy_conv_b, fwd, bwd, spec,
                                     hy_bias[j], ln_g4, ln_b4, layer, sl, nseq)
                else:
                    u = _proj(x, mods, crow, hy_w_in, layer, j, tm=max(sl, 1024), tn=512, out_dtype=BF16,
                              conv=(hy_conv_w, hy_conv_b), seq_len=sl)
                    a = _hyena_core(u, fwd, bwd, spec, hy_bias[j], nb, sl, tq, tc)
                    x = _outproj(x, a, mods, crow, hy_w_out, j, ln_g4, ln_b4, layer)
            x = _ffn(x, mods, crow, ffn_w_in, ffn_w_out, ln_g4, ln_b4, layer, 1)
        outs[name] = x

    y_prompt = outs["ctx"].reshape(batch, seq, D_MODEL)
    y_sample = outs["lat"].reshape(dec_batch, dec_seq, D_MODEL)
    return (y_prompt, y_sample, jnp.concatenate(new_k, axis=1), jnp.concatenate(new_v, axis=1))
```

```python
import functools
import math

import numpy as np
import jax
import jax.numpy as jnp
from jax import lax
from jax.experimental import pallas as pl
from jax.experimental.pallas import tpu as pltpu

D_MODEL = 1024
DEPTH = 2
GRID_W = 64
N_HEADS = 16
HEAD_DIM = D_MODEL // N_HEADS
WIN_ROWS = 8
WIN_COLS = 16
D_FF = 2816
N_MOD = 9
HY_BANDS = 16
HY_EMB = 1 + 2 * HY_BANDS
HY_FILTER_W = 64
HY_FAST_DECAY = 0.3
HY_SLOW_DECAY = 1.5
HY_TARGET = 1e-2
ALPHA = (2 * DEPTH) ** 0.25
LN_EPS = 1e-5

F32 = jnp.float32
BF16 = jnp.bfloat16
NEG = -1e30
LANES = 128
VMEM_LIMIT = 56 * 1024 * 1024
COND_ROWS = 8

Q_ROWS = 8
K_ROWS = 16
BAND_ROWS = 10
SPLIT_ROWS = 256


def _cparams(*sem):
    return pltpu.CompilerParams(dimension_semantics=sem, vmem_limit_bytes=VMEM_LIMIT)


def _dot(a, b):
    return jnp.dot(a, b, preferred_element_type=F32)


def _dot_nt(a, b):
    return lax.dot_general(a, b, (((1,), (1,)), ((), ())), preferred_element_type=F32)


def _layer_norm(y, g, b):
    mu = jnp.mean(y, axis=-1, keepdims=True)
    yc = y - mu
    var = jnp.mean(yc * yc, axis=-1, keepdims=True)
    return yc * lax.rsqrt(var + LN_EPS) * g + b


def _mods_kernel(c_ref, w_ref, b_ref, o_ref):
    c = c_ref[...]
    s = c * jax.nn.sigmoid(c)
    o_ref[...] = _dot(s.astype(BF16), w_ref[...].astype(BF16)) + b_ref[...]


def _mods(conds, mod_w, mod_b):
    tn = 1024
    nd = N_MOD * D_MODEL
    return pl.pallas_call(
        _mods_kernel,
        grid=(DEPTH, nd // tn),
        in_specs=[
            pl.BlockSpec((COND_ROWS, D_MODEL), lambda l, j: (0, 0)),
            pl.BlockSpec((None, D_MODEL, tn), lambda l, j: (l, 0, j)),
            pl.BlockSpec((None, 1, tn), lambda l, j: (l, 0, j)),
        ],
        out_specs=pl.BlockSpec((None, COND_ROWS, tn), lambda l, j: (l, 0, j)),
        out_shape=jax.ShapeDtypeStruct((DEPTH, COND_ROWS, nd), F32),
        compiler_params=_cparams("arbitrary", "arbitrary"),
        name="mods",
    )(conds, mod_w, mod_b.reshape(DEPTH, 1, nd))


def _ffn_kernel(x_ref, mod_ref, wi_ref, wo_ref, g_ref, b_ref, o_ref, *, mi, n_chunks):
    x = x_ref[...]
    sh = mod_ref[3 * mi:3 * mi + 1, :]
    sc = mod_ref[3 * mi + 1:3 * mi + 2, :]
    gt = mod_ref[3 * mi + 2:3 * mi + 3, :]
    h = (x * (1.0 + sc) + sh).astype(BF16)
    tf = D_FF // n_chunks
    f = None
    for c in range(n_chunks):
        gate = _dot(h, wi_ref[:, c * tf:(c + 1) * tf])
        up = _dot(h, wi_ref[:, D_FF + c * tf:D_FF + (c + 1) * tf])
        a = (gate * jax.nn.sigmoid(gate) * up).astype(BF16)
        part = _dot(a, wo_ref[c * tf:(c + 1) * tf, :])
        f = part if f is None else f + part
    y = ALPHA * x + (0.5 * gt) * f
    o_ref[...] = _layer_norm(y, g_ref[...], b_ref[...])


def _resident(shape, index_map):
    return pl.BlockSpec(shape, index_map, pipeline_mode=pl.Buffered(1))


def _ffn(x, mods, crow, w_in, w_out, ln_g, ln_b, layer, which, tm=1024, n_chunks=11):
    m = x.shape[0]
    mi = 0 if which == 0 else 2
    return pl.pallas_call(
        functools.partial(_ffn_kernel, mi=mi, n_chunks=n_chunks),
        grid=(m // tm,),
        in_specs=[
            pl.BlockSpec((tm, D_MODEL), lambda i: (i, 0)),
            pl.BlockSpec((None, None, N_MOD, D_MODEL), lambda i: (layer, crow(i * tm), 0, 0)),
            _resident((None, None, D_MODEL, 2 * D_FF), lambda i: (layer, which, 0, 0)),
            _resident((None, None, D_FF, D_MODEL), lambda i: (layer, which, 0, 0)),
            pl.BlockSpec((None, None, 1, D_MODEL), lambda i: (layer, mi, 0, 0)),
            pl.BlockSpec((None, None, 1, D_MODEL), lambda i: (layer, mi, 0, 0)),
        ],
        out_specs=pl.BlockSpec((tm, D_MODEL), lambda i: (i, 0)),
        out_shape=jax.ShapeDtypeStruct((m, D_MODEL), F32),
        compiler_params=_cparams("arbitrary"),
        name="ffn",
    )(x, mods, w_in, w_out, ln_g, ln_b)


def _proj_kernel(x_ref, mod_ref, w_ref, *rest, seq_len):
    if seq_len:
        cw_ref, cb_ref, o_ref, h_ref = rest
    else:
        o_ref, h_ref = rest
    j = pl.program_id(1)

    @pl.when(j == 0)
    def _():
        sh = mod_ref[3:4, :]
        sc = mod_ref[4:5, :]
        h_ref[...] = (x_ref[...] * (1.0 + sc) + sh).astype(BF16)

    u = _dot(h_ref[...], w_ref[...])
    if seq_len:
        tm = u.shape[0]
        pos = lax.broadcasted_iota(jnp.int32, (tm, 1), 0) % seq_len
        prev = jnp.where(pos == 0, 0.0, pltpu.roll(u, 1, 0))
        nxt = jnp.where(pos == seq_len - 1, 0.0, pltpu.roll(u, tm - 1, 0))
        u = prev * cw_ref[0:1, :] + u * cw_ref[1:2, :] + nxt * cw_ref[2:3, :] + cb_ref[...]
    o_ref[...] = u.astype(o_ref.dtype)


def _proj(x, mods, crow, w, layer, widx, *, tm, tn, out_dtype, conv=None, seq_len=0):
    m = x.shape[0]
    npg = D_MODEL // tn
    in_specs = [
        pl.BlockSpec((tm, D_MODEL), lambda i, j: (i, 0)),
        pl.BlockSpec((None, None, N_MOD, D_MODEL), lambda i, j: (layer, crow(i * tm), 0, 0)),
        pl.BlockSpec((None, D_MODEL, tn), lambda i, j: (widx, 0, j)),
    ]
    args = [x, mods, w]
    if conv is not None:
        cw, cb = conv
        in_specs += [
            pl.BlockSpec((None, 3, tn), lambda i, j: (widx, 0, j)),
            pl.BlockSpec((None, 1, tn), lambda i, j: (widx, 0, j)),
        ]
        args += [cw, cb.reshape(cb.shape[0], 1, cb.shape[1])]
    return pl.pallas_call(
        functools.partial(_proj_kernel, seq_len=seq_len),
        grid=(m // tm, 3 * npg),
        in_specs=in_specs,
        out_specs=pl.BlockSpec((None, tm, tn), lambda i, j: (j // npg, i, j % npg)),
        out_shape=jax.ShapeDtypeStruct((3, m, D_MODEL), out_dtype),
        scratch_shapes=[pltpu.VMEM((tm, D_MODEL), BF16)],
        compiler_params=_cparams("arbitrary", "arbitrary"),
        name="proj_conv" if seq_len else "proj",
    )(*args)


def _outproj_kernel(x_ref, a_ref, mod_ref, w_ref, g_ref, b_ref, o_ref):
    mix = _dot(a_ref[...], w_ref[...])
    y = ALPHA * x_ref[...] + mod_ref[5:6, :] * mix
    o_ref[...] = _layer_norm(y, g_ref[...], b_ref[...])


def _outproj(x, a, mods, crow, w, widx, ln_g, ln_b, layer, tm=1024):
    m = x.shape[0]
    return pl.pallas_call(
        _outproj_kernel,
        grid=(m // tm,),
        in_specs=[
            pl.BlockSpec((tm, D_MODEL), lambda i: (i, 0)),
            pl.BlockSpec((tm, D_MODEL), lambda i: (i, 0)),
            pl.BlockSpec((None, None, N_MOD, D_MODEL), lambda i: (layer, crow(i * tm), 0, 0)),
            pl.BlockSpec((None, D_MODEL, D_MODEL), lambda i: (widx, 0, 0)),
            pl.BlockSpec((None, None, 1, D_MODEL), lambda i: (layer, 1, 0, 0)),
            pl.BlockSpec((None, None, 1, D_MODEL), lambda i: (layer, 1, 0, 0)),
        ],
        out_specs=pl.BlockSpec((tm, D_MODEL), lambda i: (i, 0)),
        out_shape=jax.ShapeDtypeStruct((m, D_MODEL), F32),
        compiler_params=_cparams("arbitrary"),
        name="outproj",
    )(x, a, mods, w, ln_g, ln_b)


def _lane_tiles(a):
    return [a[:, i:i + LANES] for i in range(0, a.shape[1], LANES)]


def _head_masks():
    lane = lax.broadcasted_iota(jnp.int32, (1, LANES), 1)
    return lane < HEAD_DIM, lane >= HEAD_DIM


def _attn_ctx_kernel(x_ref, mod_ref, wqkv_ref, wo_ref, g_ref, b_ref, o_ref, ko_ref, vo_ref,
                     q_scr, k_scr, v_scr, a_scr, *, seq, nseq):
    x = x_ref[...]
    h = (x * (1.0 + mod_ref[4:5, :]) + mod_ref[3:4, :]).astype(BF16)
    q_scr[...] = (_dot(h, wqkv_ref[:, :D_MODEL]) * HEAD_DIM ** -0.5).astype(BF16)
    k_scr[...] = _dot(h, wqkv_ref[:, D_MODEL:2 * D_MODEL])
    v_scr[...] = _dot(h, wqkv_ref[:, 2 * D_MODEL:])
    masks = _head_masks()
    for s in range(nseq):
        rows = slice(s * seq, (s + 1) * seq)
        ko_ref[s] = k_scr[rows, :].reshape(seq, N_HEADS, HEAD_DIM)
        vo_ref[s] = v_scr[rows, :].reshape(seq, N_HEADS, HEAD_DIM)
        for p in range(D_MODEL // LANES):
            sl = slice(p * LANES, (p + 1) * LANES)
            q2 = q_scr[rows, sl]
            k2 = k_scr[rows, sl]
            v2 = v_scr[rows, sl]
            o2 = jnp.zeros(q2.shape, F32)
            for hm in masks:
                kh = jnp.where(hm, k2, 0.0).astype(BF16)
                vh = jnp.where(hm, v2, 0.0).astype(BF16)
                sc = _dot_nt(q2, kh)
                e = jnp.exp(sc - jnp.max(sc, axis=-1, keepdims=True))
                inv = 1.0 / jnp.sum(e, axis=-1, keepdims=True)
                o2 = o2 + _dot(e.astype(BF16), vh) * inv
            a_scr[rows, sl] = o2.astype(BF16)
    y = ALPHA * x + mod_ref[5:6, :] * _dot(a_scr[...], wo_ref[...])
    o_ref[...] = _layer_norm(y, g_ref[...], b_ref[...])


def _attn_ctx(x, mods, crow, w_qkv, w_o, widx, ln_g, ln_b, layer, batch, seq, nseq=2):
    rows = nseq * seq
    cache_spec = pl.BlockSpec((nseq, None, seq, N_HEADS, HEAD_DIM), lambda i: (i, 0, 0, 0, 0))
    cache_shape = jax.ShapeDtypeStruct((batch, 1, seq, N_HEADS, HEAD_DIM), F32)
    return pl.pallas_call(
        functools.partial(_attn_ctx_kernel, seq=seq, nseq=nseq),
        grid=(batch // nseq,),
        in_specs=[
            pl.BlockSpec((rows, D_MODEL), lambda i: (i, 0)),
            pl.BlockSpec((None, None, N_MOD, D_MODEL), lambda i: (layer, crow(i * rows), 0, 0)),
            _resident((None, D_MODEL, 3 * D_MODEL), lambda i: (widx, 0, 0)),
            _resident((None, D_MODEL, D_MODEL), lambda i: (widx, 0, 0)),
            pl.BlockSpec((None, None, 1, D_MODEL), lambda i: (layer, 1, 0, 0)),
            pl.BlockSpec((None, None, 1, D_MODEL), lambda i: (layer, 1, 0, 0)),
        ],
        out_specs=[pl.BlockSpec((rows, D_MODEL), lambda i: (i, 0)), cache_spec, cache_spec],
        out_shape=[jax.ShapeDtypeStruct((batch * seq, D_MODEL), F32), cache_shape, cache_shape],
        scratch_shapes=[pltpu.VMEM((rows, D_MODEL), BF16), pltpu.VMEM((rows, D_MODEL), F32),
                        pltpu.VMEM((rows, D_MODEL), F32), pltpu.VMEM((rows, D_MODEL), BF16)],
        compiler_params=_cparams("arbitrary"),
        name="attn_ctx",
    )(x, mods, w_qkv, w_o, ln_g, ln_b)


def _nbr_geometry(rows):
    kr = min(WIN_ROWS, rows)
    blocks = []
    for blk in range(rows // Q_ROWS):
        r0 = blk * Q_ROWS
        lo = min(max(r0 - kr // 2, 0), rows - kr)
        key_start = min(lo, rows - K_ROWS)
        bands = []
        for r in range(r0, r0 + Q_ROWS):
            rs = min(max(r - kr // 2, 0), rows - kr)
            a = rs - key_start
            band_start = min(a - a % 2, K_ROWS - BAND_ROWS)
            off = a - band_start
            assert 0 <= band_start and 0 <= off and off + kr <= BAND_ROWS
            bands.append((band_start, off, rs - r + WIN_ROWS - 1))
        blocks.append((key_start, bands))
    return kr, blocks


def _nbr_bias_rows(rpb):
    pad = GRID_W - (2 * WIN_COLS - 1)
    row = jnp.pad(rpb, ((0, 0), (0, 0), (0, pad)))
    return jnp.concatenate([row[:, :-1], row[:, 1:]], axis=-1)


def _attn_nbr_kernel(q_ref, k_ref, v_ref, kc_ref, vc_ref, rp_ref, o_ref,
                     s_ref, sc_ref, p_ref, pc_ref, inv_ref, tab_ref, *, blocks, kr):
    masks = _head_masks()
    left = masks[0]
    qn = Q_ROWS * GRID_W
    kn = K_ROWS * GRID_W
    qcol = lax.broadcasted_iota(jnp.int32, (GRID_W, LANES), 0)
    kcol = lax.broadcasted_iota(jnp.int32, (GRID_W, LANES), 1) % GRID_W
    cstart = jnp.clip(qcol - WIN_COLS // 2, 0, GRID_W - WIN_COLS)
    col_ok = (kcol >= cstart) & (kcol < cstart + WIN_COLS)
    for hh in range(2):
        for d in range(2 * WIN_ROWS - 2):
            row = jnp.broadcast_to(rp_ref[hh, d:d + 1, :], (GRID_W, LANES))
            toep = pltpu.roll(row, LANES - (WIN_COLS - 1), 1, stride=1, stride_axis=0)
            tab_ref[hh, d] = jnp.where(col_ok, toep, NEG)
    for blk, (key_start, bands) in enumerate(blocks):
        q2 = (q_ref[blk * qn:(blk + 1) * qn, :] * HEAD_DIM ** -0.5).astype(BF16)
        k2 = k_ref[key_start * GRID_W:key_start * GRID_W + kn, :]
        v2 = v_ref[key_start * GRID_W:key_start * GRID_W + kn, :]
        o2 = jnp.zeros(q2.shape, F32)
        p_ref[...] = jnp.zeros_like(p_ref)
        for hh, hm in enumerate(masks):
            kh = jnp.where(hm, k2, 0.0).astype(BF16)
            vh = jnp.where(hm, v2, 0.0).astype(BF16)
            kch = jnp.where(hm, kc_ref[...], 0.0).astype(BF16)
            vch = jnp.where(hm, vc_ref[...], 0.0).astype(BF16)
            s_ref[...] = _dot_nt(q2, kh)
            sc_ref[...] = _dot_nt(q2, kch)
            for rl, (band_start, off, drow) in enumerate(bands):
                rsl = slice(rl * GRID_W, (rl + 1) * GRID_W)
                tiles = []
                for j in range(0, BAND_ROWS, 2):
                    ok0, ok1 = off <= j < off + kr, off <= j + 1 < off + kr
                    if not (ok0 or ok1):
                        continue
                    bias = tab_ref[hh, drow - off + j]
                    if not ok0:
                        bias = jnp.where(left, NEG, bias)
                    if not ok1:
                        bias = jnp.where(left, bias, NEG)
                    csl = slice((band_start + j) * GRID_W, (band_start + j + 2) * GRID_W)
                    tiles.append((csl, s_ref[rsl, csl] + bias))
                sc = sc_ref[rsl, :]
                top = functools.reduce(jnp.maximum, _lane_tiles(sc) + [sb for _, sb in tiles])
                mx = jnp.max(top, axis=-1, keepdims=True)
                ec = jnp.exp(sc - mx)
                tot = functools.reduce(jnp.add, _lane_tiles(ec))
                for csl, sb in tiles:
                    eb = jnp.exp(sb - mx)
                    tot = tot + eb
                    p_ref[rsl, csl] = eb.astype(BF16)
                inv_ref[rsl, :] = 1.0 / jnp.sum(tot, axis=-1, keepdims=True)
                pc_ref[rsl, :] = ec.astype(BF16)
            o2 = o2 + (_dot(p_ref[...], vh) + _dot(pc_ref[...], vch)) * inv_ref[...]
        o_ref[blk * qn:(blk + 1) * qn, :] = o2.astype(o_ref.dtype)


def _attn_nbr(qkv, kc, vc, rpb, batch, seq):
    rows = seq // GRID_W
    kr, blocks = _nbr_geometry(rows)
    n_pairs = D_MODEL // LANES
    n_d = 2 * WIN_ROWS - 2
    rp = _nbr_bias_rows(rpb).reshape(n_pairs, 2, n_d, LANES)
    s_ctx = kc.shape[1]
    qspec = lambda g: pl.BlockSpec((None, seq, LANES), lambda b, p: (g, b, p))
    cspec = pl.BlockSpec((None, s_ctx, LANES), lambda b, p: (b, 0, p))
    qn, kn = Q_ROWS * GRID_W, K_ROWS * GRID_W
    return pl.pallas_call(
        functools.partial(_attn_nbr_kernel, blocks=blocks, kr=kr),
        grid=(batch, n_pairs),
        in_specs=[qspec(0), qspec(1), qspec(2), cspec, cspec,
                  pl.BlockSpec((None, 2, n_d, LANES), lambda b, p: (p, 0, 0, 0))],
        out_specs=pl.BlockSpec((seq, LANES), lambda b, p: (b, p)),
        out_shape=jax.ShapeDtypeStruct((batch * seq, D_MODEL), BF16),
        scratch_shapes=[pltpu.VMEM((qn, kn), F32), pltpu.VMEM((qn, s_ctx), F32),
                        pltpu.VMEM((qn, kn), BF16), pltpu.VMEM((qn, s_ctx), BF16),
                        pltpu.VMEM((qn, 1), F32), pltpu.VMEM((2, n_d, GRID_W, LANES), F32)],
        compiler_params=_cparams("arbitrary", "arbitrary"),
        name="attn_nbr",
    )(qkv, qkv, qkv, kc, vc, rp)


def _dft_tables(seq_len, tq, tr):
    n = 2 * seq_len
    f = np.arange(seq_len, dtype=np.float64)[:, None] + 0.5
    s = np.arange(seq_len, dtype=np.float64)[None, :]
    ang = 2.0 * np.pi * f * s / n
    re = np.cos(ang).reshape(seq_len // tq, tq, seq_len)
    im = (-np.sin(ang)).reshape(seq_len // tq, tq, seq_len)
    fwd = np.concatenate([re, im], axis=1).astype(np.float32)
    bwd = fwd.reshape(2 * seq_len, seq_len).T
    return fwd, np.ascontiguousarray(bwd).reshape(seq_len // tr, tr, 2 * seq_len)


def _split_spectrum_table(seq_len, tq):
    half = seq_len // 2
    f = np.arange(half, dtype=np.float64)[:, None] + 0.5
    s = np.arange(seq_len, dtype=np.float64)[None, :]
    lo = np.pi * f * s / seq_len
    hi = np.pi * (seq_len - f) * s / seq_len
    parts = [np.cos(lo), -np.sin(lo), np.cos(hi), np.sin(hi)]
    return np.concatenate([p.reshape(half // tq, tq, seq_len) for p in parts], axis=1).astype(np.float32)


def _twiddles(seq_len):
    th = np.pi * (np.arange(seq_len // 2, dtype=np.float64) + 0.5) / seq_len
    return np.stack([np.cos(th), np.sin(th)])[:, :, None].repeat(LANES, axis=2).astype(np.float32)


def _deinterleave_matrix():
    sel = np.zeros((SPLIT_ROWS, SPLIT_ROWS), np.float32)
    i = np.arange(SPLIT_ROWS // 2)
    sel[i, 2 * i] = 1.0
    sel[SPLIT_ROWS // 2 + i, 2 * i + 1] = 1.0
    return sel


def _filter_features(seq_len):
    pos = np.arange(seq_len, dtype=np.float64)[:, None]
    t = pos / max(seq_len - 1, 1)
    bands = np.linspace(1e-4, HY_BANDS - 1, HY_BANDS)
    ang = 2.0 * np.pi * pos / seq_len * bands
    z = np.concatenate([t, np.cos(ang), -np.sin(ang)], axis=-1)
    zp = np.zeros((seq_len, HY_FILTER_W), np.float64)
    zp[:, :HY_EMB] = z
    deltas = np.abs(np.linspace(math.log(HY_TARGET) / HY_SLOW_DECAY, math.log(HY_TARGET) / HY_FAST_DECAY, D_MODEL))
    return zp.astype(np.float32), t.astype(np.float32), deltas.astype(np.float32)[None, :]


def _filter_kernel(z_ref, t_ref, dl_ref, w1_ref, b1_ref, f1_ref, w2_ref, b2_ref, f2_ref, w3_ref, b3_ref, o_ref,
                   hid_ref, decay_ref):
    hp = lax.Precision.HIGHEST

    @pl.when(pl.program_id(0) == 0)
    def _():
        hid = jnp.sin(f1_ref[...] * (jnp.dot(z_ref[...], w1_ref[...], precision=hp, preferred_element_type=F32) + b1_ref[...]))
        hid_ref[...] = jnp.sin(f2_ref[...] * (jnp.dot(hid, w2_ref[...], precision=hp, preferred_element_type=F32) + b2_ref[...]))
        decay_ref[...] = jnp.exp(-t_ref[...] * dl_ref[...])

    hid, w3 = hid_ref[...], w3_ref[...]
    hid_hi, w3_hi = hid.astype(BF16), w3.astype(BF16)
    hid_lo = (hid - hid_hi.astype(F32)).astype(BF16)
    w3_lo = (w3 - w3_hi.astype(F32)).astype(BF16)
    filt = _dot(hid_hi, w3_hi) + (_dot(hid_lo, w3_hi) + _dot(hid_hi, w3_lo)) + b3_ref[...]
    decay = decay_ref[...]
    hf = filt[:, :D_MODEL] * decay
    hb = filt[:, D_MODEL:] * decay
    row = lax.broadcasted_iota(jnp.int32, (hb.shape[0], 1), 0)
    hb = jnp.where(row == 0, 0.0, hb)
    o_ref[:, :D_MODEL] = (hf + hb).astype(o_ref.dtype)
    o_ref[:, D_MODEL:] = (hf - hb).astype(o_ref.dtype)


def _filters(seq_len, w1, b1, f1, w2, b2, f2, w3, b3):
    z, t, deltas = _filter_features(seq_len)
    w1p = jnp.zeros((HY_FILTER_W, HY_FILTER_W), F32).at[:HY_EMB].set(w1)
    full = lambda shape: pl.BlockSpec(shape, lambda o: (0,) * len(shape))
    row = lambda a: a.reshape(1, -1)
    return pl.pallas_call(
        _filter_kernel,
        grid=(2,),
        in_specs=[full((seq_len, HY_FILTER_W)), full((seq_len, 1)), full((1, D_MODEL)),
                  full((HY_FILTER_W, HY_FILTER_W)), full((1, HY_FILTER_W)), full((1, HY_FILTER_W)),
                  full((HY_FILTER_W, HY_FILTER_W)), full((1, HY_FILTER_W)), full((1, HY_FILTER_W)),
                  pl.BlockSpec((HY_FILTER_W, 2 * D_MODEL), lambda o: (0, o)),
                  pl.BlockSpec((1, 2 * D_MODEL), lambda o: (0, o))],
        out_specs=pl.BlockSpec((None, seq_len, 2 * D_MODEL), lambda o: (o, 0, 0)),
        out_shape=jax.ShapeDtypeStruct((2, seq_len, 2 * D_MODEL), BF16),
        scratch_shapes=[pltpu.VMEM((seq_len, HY_FILTER_W), F32), pltpu.VMEM((seq_len, D_MODEL), F32)],
        compiler_params=_cparams("arbitrary"),
        name="hyena_filters",
    )(jnp.asarray(z),jnp.asarray(t), jnp.asarray(deltas), w1p, row(b1), row(f1), w2, row(b2), row(f2), w3, row(b3))


def _spectrum_kernel(g_ref, hs_ref, hd_ref, o_ref, *, tq, scale):
    for part in range(g_ref.shape[0] // tq):
        rows = slice(part * tq, (part + 1) * tq)
        h_ref = hs_ref if part % 2 == 0 else hd_ref
        o_ref[rows, :] = scale * _dot(g_ref[rows, :], h_ref[...])


def _spectrum(table, filt, seq_len, tq, tc):
    nt, rows, _ = table.shape
    ncb = D_MODEL // tc
    return pl.pallas_call(
        functools.partial(_spectrum_kernel, tq=tq, scale=1.0 / seq_len),
        grid=(2, ncb, nt),
        in_specs=[pl.BlockSpec((None, rows, seq_len), lambda o, c, f: (f, 0, 0)),
                  pl.BlockSpec((None, seq_len, tc), lambda o, c, f: (o, 0, c)),
                  pl.BlockSpec((None, seq_len, tc), lambda o, c, f: (o, 0, c + ncb))],
        out_specs=pl.BlockSpec((None, None, rows, tc), lambda o, c, f: (o, f, 0, c)),
        out_shape=jax.ShapeDtypeStruct((2, nt, rows, D_MODEL), F32),
        compiler_params=_cparams("arbitrary", "arbitrary", "arbitrary"),
        name="hyena_spectrum",
    )(table, filt, filt)


def _hyena_kernel(v_ref, gate_ref, g_ref, gt_ref, k_ref, tw_ref, sel_ref, selt_ref, b_ref, o_ref,
                  ze_ref, zo_ref, pu_ref, pv_ref, *, tq, nf, tr):
    o = pl.program_id(2)
    st = pl.program_id(3)
    half = ze_ref.shape[0]
    tc = ze_ref.shape[1]
    hs = SPLIT_ROWS // 2

    @pl.when((o == 0) & (st == 0))
    def _():
        for j in range(half // hs):
            sp = _dot(sel_ref[...], v_ref[j * SPLIT_ROWS:(j + 1) * SPLIT_ROWS, :])
            ze_ref[j * hs:(j + 1) * hs, :] = sp[:hs].astype(BF16)
            zo_ref[j * hs:(j + 1) * hs, :] = sp[hs:].astype(BF16)

    @pl.when(st < nf)
    def _():
        base = pl.multiple_of(st * 2 * tq, 2 * tq)
        c, s = tw_ref[0], tw_ref[1]
        group = 2 * LANES
        for lt in range(tc // LANES):
            cols = slice(lt * LANES, (lt + 1) * LANES)
            if lt % 2 == 0:
                gcols = slice(lt * LANES, lt * LANES + group)
                fa = _dot(g_ref[...], ze_ref[:, gcols])
                fb = _dot(g_ref[...], zo_ref[:, gcols])
            loc = slice((lt % 2) * LANES, (lt % 2 + 1) * LANES)
            ar, ai, br, bi = fa[:tq, loc], fa[tq:, loc], fb[:tq, loc], fb[tq:, loc]
            tr_, ti = c * br + s * bi, c * bi - s * br
            s1r, s1i, s2r, s2i = ar + tr_, ai + ti, ar - tr_, ai - ti
            k1r, k1i = k_ref[0:tq, cols], k_ref[tq:2 * tq, cols]
            k2r, k2i = k_ref[2 * tq:3 * tq, cols], k_ref[3 * tq:, cols]
            p1r, p1i = s1r * k1r - s1i * k1i, s1r * k1i + s1i * k1r
            p2r, p2i = s2r * k2r - s2i * k2i, s2r * k2i + s2i * k2r
            wr, wi = p1r - p2r, p1i - p2i
            pu_ref[pl.ds(base, tq), cols] = (p1r + p2r).astype(BF16)
            pu_ref[pl.ds(base + tq, tq), cols] = (p1i + p2i).astype(BF16)
            pv_ref[pl.ds(base, tq), cols] = (c * wr - s * wi).astype(BF16)
            pv_ref[pl.ds(base + tq, tq), cols] = (c * wi + s * wr).astype(BF16)

    @pl.when(st >= nf)
    def _():
        r0 = pl.multiple_of((st - nf) * tr, tr)
        rows = pl.ds(r0, tr)
        ye = _dot(gt_ref[...], pu_ref[...]) + ze_ref[rows, :].astype(F32) * b_ref[...]
        yo = _dot(gt_ref[...], pv_ref[...]) + zo_ref[rows, :].astype(F32) * b_ref[...]
        block = lambda j: pl.ds(pl.multiple_of(2 * r0 + j * SPLIT_ROWS, SPLIT_ROWS), SPLIT_ROWS)
        gates = [_dot(sel_ref[...], gate_ref[block(j), :]) for j in range(tr // hs)]
        ge = ye * jnp.concatenate([g[:hs] for g in gates], axis=0)
        go = yo * jnp.concatenate([g[hs:] for g in gates], axis=0)

        @pl.when(o == 0)
        def _():
            ze_ref[rows, :] = ge.astype(BF16)
            zo_ref[rows, :] = go.astype(BF16)

        @pl.when(o == 1)
        def _():
            for j in range(tr // hs):
                blk = jnp.concatenate([ge[j * hs:(j + 1) * hs], go[j * hs:(j + 1) * hs]], axis=0).astype(BF16)
                o_ref[block(j), :] = _dot(selt_ref[...], blk).astype(o_ref.dtype)


def _hyena_core(u, fwd_h, bwd_h, spec, tw, sel, bias, batch, seq_len, tq, tc):
    half = seq_len // 2
    nf = half // tq
    nr, tr, _ = bwd_h.shape
    ncb = D_MODEL // tc
    fidx = lambda st: jnp.minimum(st, nf - 1)
    whole = lambda shape: pl.BlockSpec(shape, lambda b, c, o, st: (0,) * len(shape))
    return pl.pallas_call(
        functools.partial(_hyena_kernel, tq=tq, nf=nf, tr=tr),
        grid=(batch, ncb, 2, nf + nr),
        in_specs=[pl.BlockSpec((None, seq_len, tc), lambda b, c, o, st: (0, b, c)),
                  pl.BlockSpec((None, seq_len, tc), lambda b, c, o, st: (1 + o, b, c)),
                  pl.BlockSpec((None, 2 * tq, half), lambda b, c, o, st: (fidx(st), 0, 0)),
                  pl.BlockSpec((None, tr, seq_len), lambda b, c, o, st: (jnp.maximum(st - nf, 0), 0, 0)),
                  pl.BlockSpec((None, None, 4 * tq, tc), lambda b, c, o, st: (o, fidx(st), 0, c)),
                  pl.BlockSpec((2, tq, LANES), lambda b, c, o, st: (0, fidx(st), 0)),
                  whole((SPLIT_ROWS, SPLIT_ROWS)), whole((SPLIT_ROWS, SPLIT_ROWS)),
                  pl.BlockSpec((None, 1, tc), lambda b, c, o, st: (o, 0, c))],
        out_specs=pl.BlockSpec((seq_len, tc), lambda b, c, o, st: (b, c)),
        out_shape=jax.ShapeDtypeStruct((batch * seq_len, D_MODEL), BF16),
        scratch_shapes=[pltpu.VMEM((half, tc), BF16), pltpu.VMEM((half, tc), BF16),
                        pltpu.VMEM((seq_len, tc), BF16), pltpu.VMEM((seq_len, tc), BF16)],
        compiler_params=_cparams("arbitrary", "arbitrary", "arbitrary", "arbitrary"),
        name="hyena_conv",
    )(u, u, fwd_h, bwd_h, spec, tw, sel, sel.T, bias.reshape(2, 1, D_MODEL))


def _hyena_layer_kernel(x_ref, mod_ref, win_ref, cw_ref, cb_ref, g_ref, gt_ref, k_ref, hb_ref, wout_ref,
                        lg_ref, lb_ref, o_ref, u_scr, *, seq_len, nseq, tn):
    x = x_ref[...]
    h = (x * (1.0 + mod_ref[4:5, :]) + mod_ref[3:4, :]).astype(BF16)
    rows = nseq * seq_len
    pos = lax.broadcasted_iota(jnp.int32, (rows, 1), 0) % seq_len
    first, last = pos == 0, pos == seq_len - 1
    for g in range(3):
        for cc in range(D_MODEL // tn):
            cols = slice(g * D_MODEL + cc * tn, g * D_MODEL + (cc + 1) * tn)
            u = _dot(h, win_ref[:, cols])
            prev = jnp.where(first, 0.0, pltpu.roll(u, 1, 0))
            nxt = jnp.where(last, 0.0, pltpu.roll(u, rows - 1, 0))
            u = prev * cw_ref[0:1, cols] + u * cw_ref[1:2, cols] + nxt * cw_ref[2:3, cols] + cb_ref[:, cols]
            u_scr[g, :, cc * tn:(cc + 1) * tn] = u.astype(BF16)
    for o in range(2):
        kr, ki = k_ref[o, :seq_len, :], k_ref[o, seq_len:, :]
        for s in range(nseq):
            r = slice(s * seq_len, (s + 1) * seq_len)
            z = u_scr[0, r, :]
            zf = _dot(g_ref[...], z)
            zr, zi = zf[:seq_len], zf[seq_len:]
            prod = jnp.concatenate([zr * kr - zi * ki, zr * ki + zi * kr], axis=0).astype(BF16)
            y = _dot(gt_ref[...], prod) + z.astype(F32) * hb_ref[o]
            u_scr[0, r, :] = (y * u_scr[1 + o, r, :].astype(F32)).astype(BF16)
    y = ALPHA * x + mod_ref[5:6, :] * _dot(u_scr[0], wout_ref[...])
    o_ref[...] = _layer_norm(y, lg_ref[...], lb_ref[...])


def _hyena_layer(x, mods, crow, w_in, w_out, widx, conv_w, conv_b, fwd, bwd, spec, bias, ln_g, ln_b, layer,
                 seq_len, nseq, tn=512):
    m = x.shape[0]
    rows = nseq * seq_len
    assert fwd.shape[0] == 1
    return pl.pallas_call(
        functools.partial(_hyena_layer_kernel, seq_len=seq_len, nseq=nseq, tn=tn),
        grid=(m // rows,),
        in_specs=[
            pl.BlockSpec((rows, D_MODEL), lambda i: (i, 0)),
            pl.BlockSpec((None, None, N_MOD, D_MODEL), lambda i: (layer, crow(i * rows), 0, 0)),
            _resident((None, D_MODEL, 3 * D_MODEL), lambda i: (widx, 0, 0)),
            pl.BlockSpec((None, 3, 3 * D_MODEL), lambda i: (widx, 0, 0)),
            pl.BlockSpec((None, 1, 3 * D_MODEL), lambda i: (widx, 0, 0)),
            pl.BlockSpec((None, 2 * seq_len, seq_len), lambda i: (0, 0, 0)),
            pl.BlockSpec((None, seq_len, 2 * seq_len), lambda i: (0, 0, 0)),
            _resident((2, None, 2 * seq_len, D_MODEL), lambda i: (0, 0, 0, 0)),
            pl.BlockSpec((2, 1, D_MODEL), lambda i: (0, 0, 0)),
            _resident((None, D_MODEL, D_MODEL), lambda i: (widx, 0, 0)),
            pl.BlockSpec((None, None, 1, D_MODEL), lambda i: (layer, 1, 0, 0)),
            pl.BlockSpec((None, None, 1, D_MODEL), lambda i: (layer, 1, 0, 0)),
        ],
        out_specs=pl.BlockSpec((rows, D_MODEL), lambda i: (i, 0)),
        out_shape=jax.ShapeDtypeStruct((m, D_MODEL), F32),
        scratch_shapes=[pltpu.VMEM((3, rows, D_MODEL), BF16)],
        compiler_params=_cparams("arbitrary"),
        name="hyena_layer",
    )(x, mods, w_in, conv_w, conv_b.reshape(conv_b.shape[0], 1, conv_b.shape[1]), fwd, bwd, spec,
      bias.reshape(2, 1, D_MODEL), w_out, ln_g, ln_b)


def _hyena_tiles(seq_len, batch):
    tq = tr = min(seq_len, 512)
    tc = D_MODEL if seq_len <= 256 else 512
    nseq = math.gcd(batch, max(1, 1024 // seq_len))
    return tq, tr, tc, nseq


def kernel(x_prompt, x_sample, cache_k, cache_v, c, c_ctx, mod_w, mod_b, ln_g, ln_b, ffn_w_in, ffn_w_out,
           attn_w_qkv, attn_w_o, attn_rpb, hy_w_in, hy_conv_w, hy_conv_b, hy_f_w1, hy_f_b1, hy_f_freq1,
           hy_f_w2, hy_f_b2, hy_f_freq2, hy_f_w3, hy_f_b3, hy_bias, hy_w_out):
    batch, seq, _ = x_prompt.shape
    dec_batch, dec_seq, _ = x_sample.shape
    assert dec_batch + 1 <= COND_ROWS

    conds = jnp.zeros((COND_ROWS, D_MODEL), F32).at[0].set(c_ctx).at[1:1 + dec_batch].set(c)
    mods = _mods(conds, mod_w, mod_b).reshape(DEPTH, COND_ROWS, N_MOD, D_MODEL)
    ln_g4 = ln_g.reshape(DEPTH, 3, 1, D_MODEL)
    ln_b4 = ln_b.reshape(DEPTH, 3, 1, D_MODEL)

    streams = {
        "ctx": dict(x=x_prompt.reshape(batch * seq, D_MODEL), batch=batch, seq=seq, crow=lambda t: 0),
        "lat": dict(x=x_sample.reshape(dec_batch * dec_seq, D_MODEL), batch=dec_batch, seq=dec_seq,
                    crow=lambda t: 1 + t // dec_seq),
    }
    as_bf16 = lambda table: jnp.asarray(table).astype(BF16)
    hyena_consts = {}
    for name, st in streams.items():
        sl = st["seq"]
        tq, tr, tc, nseq = _hyena_tiles(sl, st["batch"])
        if sl == tq:
            fwd, bwd = _dft_tables(sl, tq, tr)
            hyena_consts[name] = dict(tq=tq, tc=tc, nseq=nseq, fwd=as_bf16(fwd), bwd=as_bf16(bwd), ktab=as_bf16(fwd))
        else:
            fwd, bwd = _dft_tables(sl // 2, tq, tr)
            hyena_consts[name] = dict(tq=tq, tc=tc, nseq=nseq, fwd=as_bf16(fwd), bwd=as_bf16(bwd),
                                      ktab=as_bf16(_split_spectrum_table(sl, tq)),
                                      tw=jnp.asarray(_twiddles(sl)), sel=as_bf16(_deinterleave_matrix()))

    ffn_w_in, ffn_w_out, attn_w_qkv, attn_w_o, hy_w_in, hy_w_out = (
        w.astype(BF16) for w in (ffn_w_in, ffn_w_out, attn_w_qkv, attn_w_o, hy_w_in, hy_w_out))

    new_k, new_v = [], []
    outs = {}
    for name, st in streams.items():
        x, nb, sl, crow = st["x"], st["batch"], st["seq"], st["crow"]
        for layer in range(DEPTH):
            j = layer // 2
            x = _ffn(x, mods, crow, ffn_w_in, ffn_w_out, ln_g4, ln_b4, layer, 0)
            if layer % 2 == 0:
                if name == "ctx":
                    x, k_new, v_new = _attn_ctx(x, mods, crow, attn_w_qkv, attn_w_o, j, ln_g4, ln_b4, layer, nb, sl)
                    new_k.append(k_new)
                    new_v.append(v_new)
                else:
                    qkv = _proj(x, mods, crow, attn_w_qkv, layer, j, tm=1024, tn=1024, out_dtype=BF16)
                    kc = cache_k[:, j].reshape(nb, -1, D_MODEL)
                    vc = cache_v[:, j].reshape(nb, -1, D_MODEL)
                    a = _attn_nbr(qkv, kc, vc, attn_rpb[j], nb, sl)
                    x = _outproj(x, a, mods, crow, attn_w_o, j, ln_g4, ln_b4, layer)
            else:
                hc = hyena_consts[name]
                tq, tc = hc["tq"], hc["tc"]
                filt = _filters(sl, hy_f_w1[j], hy_f_b1[j], hy_f_freq1[j], hy_f_w2[j], hy_f_b2[j], hy_f_freq2[j],
                                hy_f_w3[j], hy_f_b3[j])
                spec = _spectrum(hc["ktab"], filt, sl, tq, tc)
                if sl == tq:
                    x = _hyena_layer(x, mods, crow, hy_w_in, hy_w_out, j, hy_conv_w, hy_conv_b, hc["fwd"], hc["bwd"],
                                     spec, hy_bias[j], ln_g4, ln_b4, layer, sl, hc["nseq"])
                else:
                    u = _proj(x, mods, crow, hy_w_in, layer, j, tm=max(sl, 1024), tn=512, out_dtype=BF16,
                              conv=(hy_conv_w, hy_conv_b), seq_len=sl)
                    a = _hyena_core(u, hc["fwd"], hc["bwd"], spec, hc["tw"], hc["sel"], hy_bias[j], nb, sl, tq, tc)
                    x = _outproj(x, a, mods, crow, hy_w_out, j, ln_g4, ln_b4, layer)
            x = _ffn(x, mods, crow, ffn_w_in, ffn_w_out, ln_g4, ln_b4, layer, 1)
        outs[name] = x

    y_prompt = outs["ctx"].reshape(batch, seq, D_MODEL)
    y_sample = outs["lat"].reshape(dec_batch, dec_seq, D_MODEL)
    return (y_prompt, y_sample, jnp.concatenate(new_k, axis=1), jnp.concatenate(new_v, axis=1))
```

```python
import functools
import math

import numpy as np
import jax
import jax.numpy as jnp
from jax import lax
from jax.experimental import pallas as pl
from jax.experimental.pallas import tpu as pltpu

D_MODEL = 1024
DEPTH = 2
GRID_W = 64
N_HEADS = 16
HEAD_DIM = D_MODEL // N_HEADS
WIN_ROWS = 8
WIN_COLS = 16
D_FF = 2816
N_MOD = 9
HY_BANDS = 16
HY_EMB = 1 + 2 * HY_BANDS
HY_FILTER_W = 64
HY_FAST_DECAY = 0.3
HY_SLOW_DECAY = 1.5
HY_TARGET = 1e-2
ALPHA = (2 * DEPTH) ** 0.25
LN_EPS = 1e-5

F32 = jnp.float32
BF16 = jnp.bfloat16
NEG = -1e30
LANES = 128
VMEM_LIMIT = 56 * 1024 * 1024
COND_ROWS = 8

Q_ROWS = 8
K_ROWS = 16
BAND_ROWS = 10
SPLIT_ROWS = 256


def _cparams(*sem):
    return pltpu.CompilerParams(dimension_semantics=sem, vmem_limit_bytes=VMEM_LIMIT)


def _dot(a, b):
    return jnp.dot(a, b, preferred_element_type=F32)


def _dot_nt(a, b):
    return lax.dot_general(a, b, (((1,), (1,)), ((), ())), preferred_element_type=F32)


def _layer_norm(y, g, b):
    mu = jnp.mean(y, axis=-1, keepdims=True)
    yc = y - mu
    var = jnp.mean(yc * yc, axis=-1, keepdims=True)
    return yc * lax.rsqrt(var + LN_EPS) * g + b


def _mods_kernel(c_ref, w_ref, b_ref, o_ref):
    c = c_ref[...]
    s = c * jax.nn.sigmoid(c)
    o_ref[...] = _dot(s.astype(BF16), w_ref[...].astype(BF16)) + b_ref[...]


def _mods(conds, mod_w, mod_b):
    tn = 1024
    nd = N_MOD * D_MODEL
    return pl.pallas_call(
        _mods_kernel,
        grid=(DEPTH, nd // tn),
        in_specs=[
            pl.BlockSpec((COND_ROWS, D_MODEL), lambda l, j: (0, 0)),
            pl.BlockSpec((None, D_MODEL, tn), lambda l, j: (l, 0, j)),
            pl.BlockSpec((None, 1, tn), lambda l, j: (l, 0, j)),
        ],
        out_specs=pl.BlockSpec((None, COND_ROWS, tn), lambda l, j: (l, 0, j)),
        out_shape=jax.ShapeDtypeStruct((DEPTH, COND_ROWS, nd), F32),
        compiler_params=_cparams("arbitrary", "arbitrary"),
        name="mods",
    )(conds, mod_w, mod_b.reshape(DEPTH, 1, nd))


def _ffn_pieces(x_ref, mod_ref, g_ref, b_ref, o_ref, mi):
    sh = mod_ref[3 * mi:3 * mi + 1, :]
    sc = mod_ref[3 * mi + 1:3 * mi + 2, :]
    gt = mod_ref[3 * mi + 2:3 * mi + 3, :]

    def modulated():
        return (x_ref[...] * (1.0 + sc) + sh).astype(BF16)

    def swiglu_chunk(h, wg, wu, wo):
        gate = _dot(h, wg)
        up = _dot(h, wu)
        return _dot((gate * jax.nn.sigmoid(gate) * up).astype(BF16), wo)

    def finish(f):
        y = ALPHA * x_ref[...] + (0.5 * gt) * f
        o_ref[...] = _layer_norm(y, g_ref[...], b_ref[...])

    return modulated, swiglu_chunk, finish


def _ffn_kernel(x_ref, mod_ref, wg_ref, wu_ref, wo_ref, g_ref, b_ref, o_ref, *, mi):
    modulated, swiglu_chunk, finish = _ffn_pieces(x_ref, mod_ref, g_ref, b_ref, o_ref, mi)
    h = modulated()
    f = swiglu_chunk(h, wg_ref[0], wu_ref[0], wo_ref[0])
    for c in range(1, wg_ref.shape[0]):
        f = f + swiglu_chunk(h, wg_ref[c], wu_ref[c], wo_ref[c])
    finish(f)


def _ffn_cast_kernel(x_ref, mod_ref, wg_ref, wu_ref, wo_ref, g_ref, b_ref, o_ref, wgo_ref, wuo_ref, woo_ref,
                     wg_s, wu_s, wo_s, h_s, acc_s, *, mi, nk):
    s = pl.program_id(0)
    modulated, swiglu_chunk, finish = _ffn_pieces(x_ref, mod_ref, g_ref, b_ref, o_ref, mi)

    def cast_chunk():
        for src, keep, out in ((wg_ref, wg_s, wgo_ref), (wu_ref, wu_s, wuo_ref), (wo_ref, wo_s, woo_ref)):
            w = src[...].astype(BF16)
            keep[s] = w
            out[...] = w

    def chunk(h, c):
        return swiglu_chunk(h, wg_s[c], wu_s[c], wo_s[c])

    @pl.when(s == 0)
    def _():
        cast_chunk()
        h_s[...] = modulated()
        acc_s[...] = jnp.zeros_like(acc_s)

    @pl.when((s >= 1) & (s < nk))
    def _():
        cast_chunk()
        acc_s[...] += chunk(h_s[...], s - 1)

    @pl.when(s == nk)
    def _():
        finish(acc_s[...] + chunk(h_s[...], nk - 1))

    @pl.when(s > nk)
    def _():
        h = modulated()
        f = chunk(h, 0)
        for c in range(1, nk):
            f = f + chunk(h, c)
        finish(f)


def _resident(shape, index_map):
    return pl.BlockSpec(shape, index_map, pipeline_mode=pl.Buffered(1))


def _ffn_cast(x, mods, crow, w_in, w_out, ln_g, ln_b, layer, which, tm=1024, tf=256):
    m = x.shape[0]
    mi = 0 if which == 0 else 2
    nk = D_FF // tf
    tile = lambda s: jnp.maximum(s - nk, 0)
    chunk = lambda s: jnp.minimum(s, nk - 1)
    wide = jax.ShapeDtypeStruct((nk, D_MODEL, tf), BF16)
    tall = jax.ShapeDtypeStruct((nk, tf, D_MODEL), BF16)
    y, wg, wu, wo = pl.pallas_call(
        functools.partial(_ffn_cast_kernel, mi=mi, nk=nk),
        grid=(nk + m // tm,),
        in_specs=[
            pl.BlockSpec((tm, D_MODEL), lambda s: (tile(s), 0)),
            pl.BlockSpec((None, None, N_MOD, D_MODEL), lambda s: (layer, crow(tile(s) * tm), 0, 0)),
            pl.BlockSpec((None, None, D_MODEL, tf), lambda s: (layer, which, 0, chunk(s))),
            pl.BlockSpec((None, None, D_MODEL, tf), lambda s: (layer, which, 0, chunk(s) + nk)),
            pl.BlockSpec((None, None, tf, D_MODEL), lambda s: (layer, which, chunk(s), 0)),
            pl.BlockSpec((None, None, 1, D_MODEL), lambda s: (layer, mi, 0, 0)),
            pl.BlockSpec((None, None, 1, D_MODEL), lambda s: (layer, mi, 0, 0)),
        ],
        out_specs=[pl.BlockSpec((tm, D_MODEL), lambda s: (tile(s), 0)),
                   pl.BlockSpec((None, D_MODEL, tf), lambda s: (chunk(s), 0, 0)),
                   pl.BlockSpec((None, D_MODEL, tf), lambda s: (chunk(s), 0, 0)),
                   pl.BlockSpec((None, tf, D_MODEL), lambda s: (chunk(s), 0, 0))],
        out_shape=[jax.ShapeDtypeStruct((m, D_MODEL), F32), wide, wide, tall],
        scratch_shapes=[pltpu.VMEM(wide.shape, BF16), pltpu.VMEM(wide.shape, BF16), pltpu.VMEM(tall.shape, BF16),
                        pltpu.VMEM((tm, D_MODEL), BF16), pltpu.VMEM((tm, D_MODEL), F32)],
        compiler_params=_cparams("arbitrary"),
        name="ffn_cast",
    )(x, mods, w_in, w_in, w_out, ln_g, ln_b)
    return y, (wg, wu, wo)


def _ffn(x, mods, crow, weights, ln_g, ln_b, layer, which, tm=1024):
    m = x.shape[0]
    mi = 0 if which == 0 else 2
    whole = lambda w: _resident(w.shape, lambda i: (0, 0, 0))
    return pl.pallas_call(
        functools.partial(_ffn_kernel, mi=mi),
        grid=(m // tm,),
        in_specs=[
            pl.BlockSpec((tm, D_MODEL), lambda i: (i, 0)),
            pl.BlockSpec((None, None, N_MOD, D_MODEL), lambda i: (layer, crow(i * tm), 0, 0)),
            whole(weights[0]), whole(weights[1]), whole(weights[2]),
            pl.BlockSpec((None, None, 1, D_MODEL), lambda i: (layer, mi, 0, 0)),
            pl.BlockSpec((None, None, 1, D_MODEL), lambda i: (layer, mi, 0, 0)),
        ],
        out_specs=pl.BlockSpec((tm, D_MODEL), lambda i: (i, 0)),
        out_shape=jax.ShapeDtypeStruct((m, D_MODEL), F32),
        compiler_params=_cparams("arbitrary"),
        name="ffn",
    )(x, mods, *weights, ln_g, ln_b)


def _proj_kernel(x_ref, mod_ref, w_ref, *rest, seq_len):
    if seq_len:
        cw_ref, cb_ref, o_ref, h_ref = rest
    else:
        o_ref, h_ref = rest
    j = pl.program_id(1)

    @pl.when(j == 0)
    def _():
        sh = mod_ref[3:4, :]
        sc = mod_ref[4:5, :]
        h_ref[...] = (x_ref[...] * (1.0 + sc) + sh).astype(BF16)

    u = _dot(h_ref[...], w_ref[...])
    if seq_len:
        tm = u.shape[0]
        pos = lax.broadcasted_iota(jnp.int32, (tm, 1), 0) % seq_len
        prev = jnp.where(pos == 0, 0.0, pltpu.roll(u, 1, 0))
        nxt = jnp.where(pos == seq_len - 1, 0.0, pltpu.roll(u, tm - 1, 0))
        u = prev * cw_ref[0:1, :] + u * cw_ref[1:2, :] + nxt * cw_ref[2:3, :] + cb_ref[...]
    o_ref[...] = u.astype(o_ref.dtype)


def _proj(x, mods, crow, w, layer, widx, *, tm, tn, out_dtype, conv=None, seq_len=0):
    m = x.shape[0]
    npg = D_MODEL // tn
    in_specs = [
        pl.BlockSpec((tm, D_MODEL), lambda i, j: (i, 0)),
        pl.BlockSpec((None, None, N_MOD, D_MODEL), lambda i, j: (layer, crow(i * tm), 0, 0)),
        pl.BlockSpec((None, D_MODEL, tn), lambda i, j: (widx, 0, j)),
    ]
    args = [x, mods, w]
    if conv is not None:
        cw, cb = conv
        in_specs += [
            pl.BlockSpec((None, 3, tn), lambda i, j: (widx, 0, j)),
            pl.BlockSpec((None, 1, tn), lambda i, j: (widx, 0, j)),
        ]
        args += [cw, cb.reshape(cb.shape[0], 1, cb.shape[1])]
    return pl.pallas_call(
        functools.partial(_proj_kernel, seq_len=seq_len),
        grid=(m // tm, 3 * npg),
        in_specs=in_specs,
        out_specs=pl.BlockSpec((None, tm, tn), lambda i, j: (j // npg, i, j % npg)),
        out_shape=jax.ShapeDtypeStruct((3, m, D_MODEL), out_dtype),
        scratch_shapes=[pltpu.VMEM((tm, D_MODEL), BF16)],
        compiler_params=_cparams("arbitrary", "arbitrary"),
        name="proj_conv" if seq_len else "proj",
    )(*args)


def _outproj_kernel(x_ref, a_ref, mod_ref, w_ref, g_ref, b_ref, o_ref):
    mix = _dot(a_ref[...], w_ref[...])
    y = ALPHA * x_ref[...] + mod_ref[5:6, :] * mix
    o_ref[...] = _layer_norm(y, g_ref[...], b_ref[...])


def _outproj(x, a, mods, crow, w, widx, ln_g, ln_b, layer, tm=1024):
    m = x.shape[0]
    return pl.pallas_call(
        _outproj_kernel,
        grid=(m // tm,),
        in_specs=[
            pl.BlockSpec((tm, D_MODEL), lambda i: (i, 0)),
            pl.BlockSpec((tm, D_MODEL), lambda i: (i, 0)),
            pl.BlockSpec((None, None, N_MOD, D_MODEL), lambda i: (layer, crow(i * tm), 0, 0)),
            pl.BlockSpec((None, D_MODEL, D_MODEL), lambda i: (widx, 0, 0)),
            pl.BlockSpec((None, None, 1, D_MODEL), lambda i: (layer, 1, 0, 0)),
            pl.BlockSpec((None, None, 1, D_MODEL), lambda i: (layer, 1, 0, 0)),
        ],
        out_specs=pl.BlockSpec((tm, D_MODEL), lambda i: (i, 0)),
        out_shape=jax.ShapeDtypeStruct((m, D_MODEL), F32),
        compiler_params=_cparams("arbitrary"),
        name="outproj",
    )(x, a, mods, w, ln_g, ln_b)


def _lane_tiles(a):
    return [a[:, i:i + LANES] for i in range(0, a.shape[1], LANES)]


def _head_masks():
    lane = lax.broadcasted_iota(jnp.int32, (1, LANES), 1)
    return lane < HEAD_DIM, lane >= HEAD_DIM


def _attn_ctx_kernel(x_ref, mod_ref, wqkv_ref, wo_ref, g_ref, b_ref, o_ref, ko_ref, vo_ref,
                     q_scr, k_scr, v_scr, a_scr, *, seq, nseq):
    x = x_ref[...]
    h = (x * (1.0 + mod_ref[4:5, :]) + mod_ref[3:4, :]).astype(BF16)
    q_scr[...] = (_dot(h, wqkv_ref[:, :D_MODEL]) * HEAD_DIM ** -0.5).astype(BF16)
    k_scr[...] = _dot(h, wqkv_ref[:, D_MODEL:2 * D_MODEL])
    v_scr[...] = _dot(h, wqkv_ref[:, 2 * D_MODEL:])
    masks = _head_masks()
    for s in range(nseq):
        rows = slice(s * seq, (s + 1) * seq)
        ko_ref[s] = k_scr[rows, :].reshape(seq, N_HEADS, HEAD_DIM)
        vo_ref[s] = v_scr[rows, :].reshape(seq, N_HEADS, HEAD_DIM)
        for p in range(D_MODEL // LANES):
            sl = slice(p * LANES, (p + 1) * LANES)
            q2 = q_scr[rows, sl]
            k2 = k_scr[rows, sl]
            v2 = v_scr[rows, sl]
            o2 = jnp.zeros(q2.shape, F32)
            for hm in masks:
                kh = jnp.where(hm, k2, 0.0).astype(BF16)
                vh = jnp.where(hm, v2, 0.0).astype(BF16)
                sc = _dot_nt(q2, kh)
                e = jnp.exp(sc - jnp.max(sc, axis=-1, keepdims=True))
                inv = 1.0 / jnp.sum(e, axis=-1, keepdims=True)
                o2 = o2 + _dot(e.astype(BF16), vh) * inv
            a_scr[rows, sl] = o2.astype(BF16)
    y = ALPHA * x + mod_ref[5:6, :] * _dot(a_scr[...], wo_ref[...])
    o_ref[...] = _layer_norm(y, g_ref[...], b_ref[...])


def _attn_ctx(x, mods, crow, w_qkv, w_o, widx, ln_g, ln_b, layer, batch, seq, nseq=2):
    rows = nseq * seq
    cache_spec = pl.BlockSpec((nseq, None, seq, N_HEADS, HEAD_DIM), lambda i: (i, 0, 0, 0, 0))
    cache_shape = jax.ShapeDtypeStruct((batch, 1, seq, N_HEADS, HEAD_DIM), F32)
    return pl.pallas_call(
        functools.partial(_attn_ctx_kernel, seq=seq, nseq=nseq),
        grid=(batch // nseq,),
        in_specs=[
            pl.BlockSpec((rows, D_MODEL), lambda i: (i, 0)),
            pl.BlockSpec((None, None, N_MOD, D_MODEL), lambda i: (layer, crow(i * rows), 0, 0)),
            _resident((None, D_MODEL, 3 * D_MODEL), lambda i: (widx, 0, 0)),
            _resident((None, D_MODEL, D_MODEL), lambda i: (widx, 0, 0)),
            pl.BlockSpec((None, None, 1, D_MODEL), lambda i: (layer, 1, 0, 0)),
            pl.BlockSpec((None, None, 1, D_MODEL), lambda i: (layer, 1, 0, 0)),
        ],
        out_specs=[pl.BlockSpec((rows, D_MODEL), lambda i: (i, 0)), cache_spec, cache_spec],
        out_shape=[jax.ShapeDtypeStruct((batch * seq, D_MODEL), F32), cache_shape, cache_shape],
        scratch_shapes=[pltpu.VMEM((rows, D_MODEL), BF16), pltpu.VMEM((rows, D_MODEL), F32),
                        pltpu.VMEM((rows, D_MODEL), F32), pltpu.VMEM((rows, D_MODEL), BF16)],
        compiler_params=_cparams("arbitrary"),
        name="attn_ctx",
    )(x, mods, w_qkv, w_o, ln_g, ln_b)


def _nbr_geometry(rows):
    kr = min(WIN_ROWS, rows)
    blocks = []
    for blk in range(rows // Q_ROWS):
        r0 = blk * Q_ROWS
        lo = min(max(r0 - kr // 2, 0), rows - kr)
        key_start = min(lo, rows - K_ROWS)
        bands = []
        for r in range(r0, r0 + Q_ROWS):
            rs = min(max(r - kr // 2, 0), rows - kr)
            a = rs - key_start
            band_start = min(a - a % 2, K_ROWS - BAND_ROWS)
            off = a - band_start
            assert 0 <= band_start and 0 <= off and off + kr <= BAND_ROWS
            bands.append((band_start, off, rs - r + WIN_ROWS - 1))
        blocks.append((key_start, bands))
    return kr, blocks


def _nbr_bias_rows(rpb):
    pad = GRID_W - (2 * WIN_COLS - 1)
    row = jnp.pad(rpb, ((0, 0), (0, 0), (0, pad)))
    return jnp.concatenate([row[:, :-1], row[:, 1:]], axis=-1)


def _attn_nbr_kernel(q_ref, k_ref, v_ref, kc_ref, vc_ref, rp_ref, o_ref,
                     s_ref, sc_ref, p_ref, pc_ref, inv_ref, tab_ref, *, blocks, kr):
    masks = _head_masks()
    left = masks[0]
    qn = Q_ROWS * GRID_W
    kn = K_ROWS * GRID_W
    qcol = lax.broadcasted_iota(jnp.int32, (GRID_W, LANES), 0)
    kcol = lax.broadcasted_iota(jnp.int32, (GRID_W, LANES), 1) % GRID_W
    cstart = jnp.clip(qcol - WIN_COLS // 2, 0, GRID_W - WIN_COLS)
    col_ok = (kcol >= cstart) & (kcol < cstart + WIN_COLS)
    for hh in range(2):
        for d in range(2 * WIN_ROWS - 2):
            row = jnp.broadcast_to(rp_ref[hh, d:d + 1, :], (GRID_W, LANES))
            toep = pltpu.roll(row, LANES - (WIN_COLS - 1), 1, stride=1, stride_axis=0)
            tab_ref[hh, d] = jnp.where(col_ok, toep, NEG)
    for blk, (key_start, bands) in enumerate(blocks):
        q2 = (q_ref[blk * qn:(blk + 1) * qn, :] * HEAD_DIM ** -0.5).astype(BF16)
        k2 = k_ref[key_start * GRID_W:key_start * GRID_W + kn, :]
        v2 = v_ref[key_start * GRID_W:key_start * GRID_W + kn, :]
        o2 = jnp.zeros(q2.shape, F32)
        p_ref[...] = jnp.zeros_like(p_ref)
        for hh, hm in enumerate(masks):
            kh = jnp.where(hm, k2, 0.0).astype(BF16)
            vh = jnp.where(hm, v2, 0.0).astype(BF16)
            kch = jnp.where(hm, kc_ref[...], 0.0).astype(BF16)
            vch = jnp.where(hm, vc_ref[...], 0.0).astype(BF16)
            s_ref[...] = _dot_nt(q2, kh)
            sc_ref[...] = _dot_nt(q2, kch)
            for rl, (band_start, off, drow) in enumerate(bands):
                rsl = slice(rl * GRID_W, (rl + 1) * GRID_W)
                tiles = []
                for j in range(0, BAND_ROWS, 2):
                    ok0, ok1 = off <= j < off + kr, off <= j + 1 < off + kr
                    if not (ok0 or ok1):
                        continue
                    bias = tab_ref[hh, drow - off + j]
                    if not ok0:
                        bias = jnp.where(left, NEG, bias)
                    if not ok1:
                        bias = jnp.where(left, bias, NEG)
                    csl = slice((band_start + j) * GRID_W, (band_start + j + 2) * GRID_W)
                    tiles.append((csl, s_ref[rsl, csl] + bias))
                sc = sc_ref[rsl, :]
                top = functools.reduce(jnp.maximum, _lane_tiles(sc) + [sb for _, sb in tiles])
                mx = jnp.max(top, axis=-1, keepdims=True)
                ec = jnp.exp(sc - mx)
                tot = functools.reduce(jnp.add, _lane_tiles(ec))
                for csl, sb in tiles:
                    eb = jnp.exp(sb - mx)
                    tot = tot + eb
                    p_ref[rsl, csl] = eb.astype(BF16)
                inv_ref[rsl, :] = 1.0 / jnp.sum(tot, axis=-1, keepdims=True)
                pc_ref[rsl, :] = ec.astype(BF16)
            o2 = o2 + (_dot(p_ref[...], vh) + _dot(pc_ref[...], vch)) * inv_ref[...]
        o_ref[blk * qn:(blk + 1) * qn, :] = o2.astype(o_ref.dtype)


def _attn_nbr(qkv, kc, vc, rpb, batch, seq):
    rows = seq // GRID_W
    kr, blocks = _nbr_geometry(rows)
    n_pairs = D_MODEL // LANES
    n_d = 2 * WIN_ROWS - 2
    rp = _nbr_bias_rows(rpb).reshape(n_pairs, 2, n_d, LANES)
    s_ctx = kc.shape[1]
    qspec = lambda g: pl.BlockSpec((None, seq, LANES), lambda b, p: (g, b, p))
    cspec = pl.BlockSpec((None, s_ctx, LANES), lambda b, p: (b, 0, p))
    qn, kn = Q_ROWS * GRID_W, K_ROWS * GRID_W
    return pl.pallas_call(
        functools.partial(_attn_nbr_kernel, blocks=blocks, kr=kr),
        grid=(batch, n_pairs),
        in_specs=[qspec(0), qspec(1), qspec(2), cspec, cspec,
                  pl.BlockSpec((None, 2, n_d, LANES), lambda b, p: (p, 0, 0, 0))],
        out_specs=pl.BlockSpec((seq, LANES), lambda b, p: (b, p)),
        out_shape=jax.ShapeDtypeStruct((batch * seq, D_MODEL), BF16),
        scratch_shapes=[pltpu.VMEM((qn, kn), F32), pltpu.VMEM((qn, s_ctx), F32),
                        pltpu.VMEM((qn, kn), BF16), pltpu.VMEM((qn, s_ctx), BF16),
                        pltpu.VMEM((qn, 1), F32), pltpu.VMEM((2, n_d, GRID_W, LANES), F32)],
        compiler_params=_cparams("arbitrary", "arbitrary"),
        name="attn_nbr",
    )(qkv, qkv, qkv, kc, vc, rp)


def _dft_tables(seq_len, tq, tr):
    n = 2 * seq_len
    f = np.arange(seq_len, dtype=np.float64)[:, None] + 0.5
    s = np.arange(seq_len, dtype=np.float64)[None, :]
    ang = 2.0 * np.pi * f * s / n
    re = np.cos(ang).reshape(seq_len // tq, tq, seq_len)
    im = (-np.sin(ang)).reshape(seq_len // tq, tq, seq_len)
    fwd = np.concatenate([re, im], axis=1).astype(np.float32)
    bwd = fwd.reshape(2 * seq_len, seq_len).T
    return fwd, np.ascontiguousarray(bwd).reshape(seq_len // tr, tr, 2 * seq_len)


def _split_spectrum_table(seq_len, tq):
    half = seq_len // 2
    f = np.arange(half, dtype=np.float64)[:, None] + 0.5
    s = np.arange(seq_len, dtype=np.float64)[None, :]
    lo = np.pi * f * s / seq_len
    hi = np.pi * (seq_len - f) * s / seq_len
    parts = [np.cos(lo), -np.sin(lo), np.cos(hi), np.sin(hi)]
    return np.concatenate([p.reshape(half // tq, tq, seq_len) for p in parts], axis=1).astype(np.float32)


def _twiddles(seq_len):
    th = np.pi * (np.arange(seq_len // 2, dtype=np.float64) + 0.5) / seq_len
    return np.stack([np.cos(th), np.sin(th)])[:, :, None].repeat(LANES, axis=2).astype(np.float32)


def _deinterleave_matrix():
    sel = np.zeros((SPLIT_ROWS, SPLIT_ROWS), np.float32)
    i = np.arange(SPLIT_ROWS // 2)
    sel[i, 2 * i] = 1.0
    sel[SPLIT_ROWS // 2 + i, 2 * i + 1] = 1.0
    return sel


def _filter_features(seq_len):
    pos = np.arange(seq_len, dtype=np.float64)[:, None]
    t = pos / max(seq_len - 1, 1)
    bands = np.linspace(1e-4, HY_BANDS - 1, HY_BANDS)
    ang = 2.0 * np.pi * pos / seq_len * bands
    z = np.concatenate([t, np.cos(ang), -np.sin(ang)], axis=-1)
    zp = np.zeros((seq_len, HY_FILTER_W), np.float64)
    zp[:, :HY_EMB] = z
    deltas = np.abs(np.linspace(math.log(HY_TARGET) / HY_SLOW_DECAY, math.log(HY_TARGET) / HY_FAST_DECAY, D_MODEL))
    return zp.astype(np.float32), t.astype(np.float32), deltas.astype(np.float32)[None, :]


def _filter_kernel(z_ref, t_ref, dl_ref, w1_ref, b1_ref, f1_ref, w2_ref, b2_ref, f2_ref, w3_ref, b3_ref, o_ref,
                   hid_ref, decay_ref):
    hp = lax.Precision.HIGHEST

    @pl.when(pl.program_id(0) == 0)
    def _():
        hid = jnp.sin(f1_ref[...] * (jnp.dot(z_ref[...], w1_ref[...], precision=hp, preferred_element_type=F32) + b1_ref[...]))
        hid_ref[...] = jnp.sin(f2_ref[...] * (jnp.dot(hid, w2_ref[...], precision=hp, preferred_element_type=F32) + b2_ref[...]))
        decay_ref[...] = jnp.exp(-t_ref[...] * dl_ref[...])

    hid, w3 = hid_ref[...], w3_ref[...]
    hid_hi, w3_hi = hid.astype(BF16), w3.astype(BF16)
    hid_lo = (hid - hid_hi.astype(F32)).astype(BF16)
    w3_lo = (w3 - w3_hi.astype(F32)).astype(BF16)
    filt = _dot(hid_hi, w3_hi) + (_dot(hid_lo, w3_hi) + _dot(hid_hi, w3_lo)) + b3_ref[...]
    decay = decay_ref[...]
    hf = filt[:, :D_MODEL] * decay
    hb = filt[:, D_MODEL:] * decay
    row = lax.broadcasted_iota(jnp.int32, (hb.shape[0], 1), 0)
    hb = jnp.where(row == 0, 0.0, hb)
    o_ref[:, :D_MODEL] = (hf + hb).astype(o_ref.dtype)
    o_ref[:, D_MODEL:] = (hf - hb).astype(o_ref.dtype)


def _filters(seq_len, w1, b1, f1, w2, b2, f2, w3, b3):
    z, t, deltas = _filter_features(seq_len)
    w1p = jnp.zeros((HY_FILTER_W, HY_FILTER_W), F32).at[:HY_EMB].set(w1)
    full = lambda shape: pl.BlockSpec(shape, lambda o: (0,) * len(shape))
    row = lambda a: a.reshape(1, -1)
    return pl.pallas_call(
        _filter_kernel,
        grid=(2,),
        in_specs=[full((seq_len, HY_FILTER_W)), full((seq_len, 1)), full((1, D_MODEL)),
                  full((HY_FILTER_W, HY_FILTER_W)), full((1, HY_FILTER_W)), full((1, HY_FILTER_W)),
                  full((HY_FILTER_W, HY_FILTER_W)), full((1, HY_FILTER_W)), full((1, HY_FILTER_W)),
                  pl.BlockSpec((HY_FILTER_W, 2 * D_MODEL), lambda o: (0, o)),
                  pl.BlockSpec((1, 2 * D_MODEL), lambda o: (0, o))],
        out_specs=pl.BlockSpec((None, seq_len, 2 * D_MODEL), lambda o: (o, 0, 0)),
        out_shape=jax.ShapeDtypeStruct((2, seq_len, 2 * D_MODEL), BF16),
        scratch_shapes=[pltpu.VMEM((seq_len, HY_FILTER_W), F32), pltpu.VMEM((seq_len, D_MODEL), F32)],
        compiler_params=_cparams("arbitrary"),
        name="hyena_filters",
    )(jnp.asarray(z),jnp.asarray(t), jnp.asarray(deltas), w1p, row(b1), row(f1), w2, row(b2), row(f2), w3, row(b3))


def _spectrum_kernel(g_ref, hs_ref, hd_ref, o_ref, *, tq, scale):
    for part in range(g_ref.shape[0] // tq):
        rows = slice(part * tq, (part + 1) * tq)
        h_ref = hs_ref if part % 2 == 0 else hd_ref
        o_ref[rows, :] = scale * _dot(g_ref[rows, :], h_ref[...])


def _spectrum(table, filt, seq_len, tq, tc):
    nt, rows, _ = table.shape
    ncb = D_MODEL // tc
    return pl.pallas_call(
        functools.partial(_spectrum_kernel, tq=tq, scale=1.0 / seq_len),
        grid=(2, ncb, nt),
        in_specs=[pl.BlockSpec((None, rows, seq_len), lambda o, c, f: (f, 0, 0)),
                  pl.BlockSpec((None, seq_len, tc), lambda o, c, f: (o, 0, c)),
                  pl.BlockSpec((None, seq_len, tc), lambda o, c, f: (o, 0, c + ncb))],
        out_specs=pl.BlockSpec((None, None, rows, tc), lambda o, c, f: (o, f, 0, c)),
        out_shape=jax.ShapeDtypeStruct((2, nt, rows, D_MODEL), F32),
        compiler_params=_cparams("arbitrary", "arbitrary", "arbitrary"),
        name="hyena_spectrum",
    )(table, filt, filt)


def _hyena_kernel(v_ref, gate_ref, g_ref, gt_ref, k_ref, tw_ref, sel_ref, selt_ref, b_ref, o_ref,
                  ze_ref, zo_ref, pu_ref, pv_ref, *, tq, nf, tr):
    o = pl.program_id(2)
    st = pl.program_id(3)
    half = ze_ref.shape[0]
    tc = ze_ref.shape[1]
    hs = SPLIT_ROWS // 2

    @pl.when((o == 0) & (st == 0))
    def _():
        for j in range(half // hs):
            sp = _dot(sel_ref[...], v_ref[j * SPLIT_ROWS:(j + 1) * SPLIT_ROWS, :])
            ze_ref[j * hs:(j + 1) * hs, :] = sp[:hs].astype(BF16)
            zo_ref[j * hs:(j + 1) * hs, :] = sp[hs:].astype(BF16)

    @pl.when(st < nf)
    def _():
        base = pl.multiple_of(st * 2 * tq, 2 * tq)
        c, s = tw_ref[0], tw_ref[1]
        group = 2 * LANES
        for lt in range(tc // LANES):
            cols = slice(lt * LANES, (lt + 1) * LANES)
            if lt % 2 == 0:
                gcols = slice(lt * LANES, lt * LANES + group)
                fa = _dot(g_ref[...], ze_ref[:, gcols])
                fb = _dot(g_ref[...], zo_ref[:, gcols])
            loc = slice((lt % 2) * LANES, (lt % 2 + 1) * LANES)
            ar, ai, br, bi = fa[:tq, loc], fa[tq:, loc], fb[:tq, loc], fb[tq:, loc]
            tr_, ti = c * br + s * bi, c * bi - s * br
            s1r, s1i, s2r, s2i = ar + tr_, ai + ti, ar - tr_, ai - ti
            k1r, k1i = k_ref[0:tq, cols], k_ref[tq:2 * tq, cols]
            k2r, k2i = k_ref[2 * tq:3 * tq, cols], k_ref[3 * tq:, cols]
            p1r, p1i = s1r * k1r - s1i * k1i, s1r * k1i + s1i * k1r
            p2r, p2i = s2r * k2r - s2i * k2i, s2r * k2i + s2i * k2r
            wr, wi = p1r - p2r, p1i - p2i
            pu_ref[pl.ds(base, tq), cols] = (p1r + p2r).astype(BF16)
            pu_ref[pl.ds(base + tq, tq), cols] = (p1i + p2i).astype(BF16)
            pv_ref[pl.ds(base, tq), cols] = (c * wr - s * wi).astype(BF16)
            pv_ref[pl.ds(base + tq, tq), cols] = (c * wi + s * wr).astype(BF16)

    @pl.when(st >= nf)
    def _():
        r0 = pl.multiple_of((st - nf) * tr, tr)
        rows = pl.ds(r0, tr)
        ye = _dot(gt_ref[...], pu_ref[...]) + ze_ref[rows, :].astype(F32) * b_ref[...]
        yo = _dot(gt_ref[...], pv_ref[...]) + zo_ref[rows, :].astype(F32) * b_ref[...]
        block = lambda j: pl.ds(pl.multiple_of(2 * r0 + j * SPLIT_ROWS, SPLIT_ROWS), SPLIT_ROWS)
        gates = [_dot(sel_ref[...], gate_ref[block(j), :]) for j in range(tr // hs)]
        ge = ye * jnp.concatenate([g[:hs] for g in gates], axis=0)
        go = yo * jnp.concatenate([g[hs:] for g in gates], axis=0)

        @pl.when(o == 0)
        def _():
            ze_ref[rows, :] = ge.astype(BF16)
            zo_ref[rows, :] = go.astype(BF16)

        @pl.when(o == 1)
        def _():
            for j in range(tr // hs):
                blk = jnp.concatenate([ge[j * hs:(j + 1) * hs], go[j * hs:(j + 1) * hs]], axis=0).astype(BF16)
                o_ref[block(j), :] = _dot(selt_ref[...], blk).astype(o_ref.dtype)


def _hyena_core(u, fwd_h, bwd_h, spec, tw, sel, bias, batch, seq_len, tq, tc):
    half = seq_len // 2
    nf = half // tq
    nr, tr, _ = bwd_h.shape
    ncb = D_MODEL // tc
    fidx = lambda st: jnp.minimum(st, nf - 1)
    whole = lambda shape: pl.BlockSpec(shape, lambda b, c, o, st: (0,) * len(shape))
    return pl.pallas_call(
        functools.partial(_hyena_kernel, tq=tq, nf=nf, tr=tr),
        grid=(batch, ncb, 2, nf + nr),
        in_specs=[pl.BlockSpec((None, seq_len, tc), lambda b, c, o, st: (0, b, c)),
                  pl.BlockSpec((None, seq_len, tc), lambda b, c, o, st: (1 + o, b, c)),
                  pl.BlockSpec((None, 2 * tq, half), lambda b, c, o, st: (fidx(st), 0, 0)),
                  pl.BlockSpec((None, tr, seq_len), lambda b, c, o, st: (jnp.maximum(st - nf, 0), 0, 0)),
                  pl.BlockSpec((None, None, 4 * tq, tc), lambda b, c, o, st: (o, fidx(st), 0, c)),
                  pl.BlockSpec((2, tq, LANES), lambda b, c, o, st: (0, fidx(st), 0)),
                  whole((SPLIT_ROWS, SPLIT_ROWS)), whole((SPLIT_ROWS, SPLIT_ROWS)),
                  pl.BlockSpec((None, 1, tc), lambda b, c, o, st: (o, 0, c))],
        out_specs=pl.BlockSpec((seq_len, tc), lambda b, c, o, st: (b, c)),
        out_shape=jax.ShapeDtypeStruct((batch * seq_len, D_MODEL), BF16),
        scratch_shapes=[pltpu.VMEM((half, tc), BF16), pltpu.VMEM((half, tc), BF16),
                        pltpu.VMEM((seq_len, tc), BF16), pltpu.VMEM((seq_len, tc), BF16)],
        compiler_params=_cparams("arbitrary", "arbitrary", "arbitrary", "arbitrary"),
        name="hyena_conv",
    )(u, u, fwd_h, bwd_h, spec, tw, sel, sel.T, bias.reshape(2, 1, D_MODEL))


def _hyena_layer_kernel(x_ref, mod_ref, win_ref, cw_ref, cb_ref, g_ref, gt_ref, k_ref, hb_ref, wout_ref,
                        lg_ref, lb_ref, o_ref, u_scr, *, seq_len, nseq, tn):
    x = x_ref[...]
    h = (x * (1.0 + mod_ref[4:5, :]) + mod_ref[3:4, :]).astype(BF16)
    rows = nseq * seq_len
    pos = lax.broadcasted_iota(jnp.int32, (rows, 1), 0) % seq_len
    first, last = pos == 0, pos == seq_len - 1
    for g in range(3):
        for cc in range(D_MODEL // tn):
            cols = slice(g * D_MODEL + cc * tn, g * D_MODEL + (cc + 1) * tn)
            u = _dot(h, win_ref[:, cols])
            prev = jnp.where(first, 0.0, pltpu.roll(u, 1, 0))
            nxt = jnp.where(last, 0.0, pltpu.roll(u, rows - 1, 0))
            u = prev * cw_ref[0:1, cols] + u * cw_ref[1:2, cols] + nxt * cw_ref[2:3, cols] + cb_ref[:, cols]
            u_scr[g, :, cc * tn:(cc + 1) * tn] = u.astype(BF16)
    for o in range(2):
        kr, ki = k_ref[o, :seq_len, :], k_ref[o, seq_len:, :]
        for s in range(nseq):
            r = slice(s * seq_len, (s + 1) * seq_len)
            z = u_scr[0, r, :]
            zf = _dot(g_ref[...], z)
            zr, zi = zf[:seq_len], zf[seq_len:]
            prod = jnp.concatenate([zr * kr - zi * ki, zr * ki + zi * kr], axis=0).astype(BF16)
            y = _dot(gt_ref[...], prod) + z.astype(F32) * hb_ref[o]
            u_scr[0, r, :] = (y * u_scr[1 + o, r, :].astype(F32)).astype(BF16)
    y = ALPHA * x + mod_ref[5:6, :] * _dot(u_scr[0], wout_ref[...])
    o_ref[...] = _layer_norm(y, lg_ref[...], lb_ref[...])


def _hyena_layer(x, mods, crow, w_in, w_out, widx, conv_w, conv_b, fwd, bwd, spec, bias, ln_g, ln_b, layer,
                 seq_len, nseq, tn=512):
    m = x.shape[0]
    rows = nseq * seq_len
    assert fwd.shape[0] == 1
    return pl.pallas_call(
        functools.partial(_hyena_layer_kernel, seq_len=seq_len, nseq=nseq, tn=tn),
        grid=(m // rows,),
        in_specs=[
            pl.BlockSpec((rows, D_MODEL), lambda i: (i, 0)),
            pl.BlockSpec((None, None, N_MOD, D_MODEL), lambda i: (layer, crow(i * rows), 0, 0)),
            _resident((None, D_MODEL, 3 * D_MODEL), lambda i: (widx, 0, 0)),
            pl.BlockSpec((None, 3, 3 * D_MODEL), lambda i: (widx, 0, 0)),
            pl.BlockSpec((None, 1, 3 * D_MODEL), lambda i: (widx, 0, 0)),
            pl.BlockSpec((None, 2 * seq_len, seq_len), lambda i: (0, 0, 0)),
            pl.BlockSpec((None, seq_len, 2 * seq_len), lambda i: (0, 0, 0)),
            _resident((2, None, 2 * seq_len, D_MODEL), lambda i: (0, 0, 0, 0)),
            pl.BlockSpec((2, 1, D_MODEL), lambda i: (0, 0, 0)),
            _resident((None, D_MODEL, D_MODEL), lambda i: (widx, 0, 0)),
            pl.BlockSpec((None, None, 1, D_MODEL), lambda i: (layer, 1, 0, 0)),
            pl.BlockSpec((None, None, 1, D_MODEL), lambda i: (layer, 1, 0, 0)),
        ],
        out_specs=pl.BlockSpec((rows, D_MODEL), lambda i: (i, 0)),
        out_shape=jax.ShapeDtypeStruct((m, D_MODEL), F32),
        scratch_shapes=[pltpu.VMEM((3, rows, D_MODEL), BF16)],
        compiler_params=_cparams("arbitrary"),
        name="hyena_layer",
    )(x, mods, w_in, conv_w, conv_b.reshape(conv_b.shape[0], 1, conv_b.shape[1]), fwd, bwd, spec,
      bias.reshape(2, 1, D_MODEL), w_out, ln_g, ln_b)


def _hyena_tiles(seq_len, batch):
    tq = tr = min(seq_len, 512)
    tc = D_MODEL if seq_len <= 256 else 512
    nseq = math.gcd(batch, max(1, 1024 // seq_len))
    return tq, tr, tc, nseq


def kernel(x_prompt, x_sample, cache_k, cache_v, c, c_ctx, mod_w, mod_b, ln_g, ln_b, ffn_w_in, ffn_w_out,
           attn_w_qkv, attn_w_o, attn_rpb, hy_w_in, hy_conv_w, hy_conv_b, hy_f_w1, hy_f_b1, hy_f_freq1,
           hy_f_w2, hy_f_b2, hy_f_freq2, hy_f_w3, hy_f_b3, hy_bias, hy_w_out):
    batch, seq, _ = x_prompt.shape
    dec_batch, dec_seq, _ = x_sample.shape
    assert dec_batch + 1 <= COND_ROWS

    conds = jnp.zeros((COND_ROWS, D_MODEL), F32).at[0].set(c_ctx).at[1:1 + dec_batch].set(c)
    mods = _mods(conds, mod_w, mod_b).reshape(DEPTH, COND_ROWS, N_MOD, D_MODEL)
    ln_g4 = ln_g.reshape(DEPTH, 3, 1, D_MODEL)
    ln_b4 = ln_b.reshape(DEPTH, 3, 1, D_MODEL)

    streams = {
        "ctx": dict(x=x_prompt.reshape(batch * seq, D_MODEL), batch=batch, seq=seq, crow=lambda t: 0),
        "lat": dict(x=x_sample.reshape(dec_batch * dec_seq, D_MODEL), batch=dec_batch, seq=dec_seq,
                    crow=lambda t: 1 + t // dec_seq),
    }
    as_bf16 = lambda table: jnp.asarray(table).astype(BF16)
    hyena_consts = {}
    for name, st in streams.items():
        sl = st["seq"]
        tq, tr, tc, nseq = _hyena_tiles(sl, st["batch"])
        if sl == tq:
            fwd, bwd = _dft_tables(sl, tq, tr)
            hyena_consts[name] = dict(tq=tq, tc=tc, nseq=nseq, fwd=as_bf16(fwd), bwd=as_bf16(bwd), ktab=as_bf16(fwd))
        else:
            fwd, bwd = _dft_tables(sl // 2, tq, tr)
            hyena_consts[name] = dict(tq=tq, tc=tc, nseq=nseq, fwd=as_bf16(fwd), bwd=as_bf16(bwd),
                                      ktab=as_bf16(_split_spectrum_table(sl, tq)),
                                      tw=jnp.asarray(_twiddles(sl)), sel=as_bf16(_deinterleave_matrix()))

    attn_w_qkv, attn_w_o, hy_w_in, hy_w_out = (w.astype(BF16) for w in (attn_w_qkv, attn_w_o, hy_w_in, hy_w_out))
    ffn_weights = {}

    def ffn(x, crow, layer, which):
        if (layer, which) in ffn_weights:
            return _ffn(x, mods, crow, ffn_weights[layer, which], ln_g4, ln_b4, layer, which)
        y, ffn_weights[layer, which] = _ffn_cast(x, mods, crow, ffn_w_in, ffn_w_out, ln_g4, ln_b4, layer, which)
        return y

    new_k, new_v = [], []
    outs = {}
    for name, st in streams.items():
        x, nb, sl, crow = st["x"], st["batch"], st["seq"], st["crow"]
        for layer in range(DEPTH):
            j = layer // 2
            x = ffn(x, crow, layer, 0)
            if layer % 2 == 0:
                if name == "ctx":
                    x, k_new, v_new = _attn_ctx(x, mods, crow, attn_w_qkv, attn_w_o, j, ln_g4, ln_b4, layer, nb, sl)
                    new_k.append(k_new)
                    new_v.append(v_new)
                else:
                    qkv = _proj(x, mods, crow, attn_w_qkv, layer, j, tm=1024, tn=1024, out_dtype=BF16)
                    kc = cache_k[:, j].reshape(nb, -1, D_MODEL)
                    vc = cache_v[:, j].reshape(nb, -1, D_MODEL)
                    a = _attn_nbr(qkv, kc, vc, attn_rpb[j], nb, sl)
                    x = _outproj(x, a, mods, crow, attn_w_o, j, ln_g4, ln_b4, layer)
            else:
                hc = hyena_consts[name]
                tq, tc = hc["tq"], hc["tc"]
                filt = _filters(sl, hy_f_w1[j], hy_f_b1[j], hy_f_freq1[j], hy_f_w2[j], hy_f_b2[j], hy_f_freq2[j],
                                hy_f_w3[j], hy_f_b3[j])
                spec = _spectrum(hc["ktab"], filt, sl, tq, tc)
                if sl == tq:
                    x = _hyena_layer(x, mods, crow, hy_w_in, hy_w_out, j, hy_conv_w, hy_conv_b, hc["fwd"], hc["bwd"],
                                     spec, hy_bias[j], ln_g4, ln_b4, layer, sl, hc["nseq"])
                else:
                    u = _proj(x, mods, crow, hy_w_in, layer, j, tm=max(sl, 1024), tn=512, out_dtype=BF16,
                              conv=(hy_conv_w, hy_conv_b), seq_len=sl)
                    a = _hyena_core(u, hc["fwd"], hc["bwd"], spec, hc["tw"], hc["sel"], hy_bias[j], nb, sl, tq, tc)
                    x = _outproj(x, a, mods, crow, hy_w_out, j, ln_g4, ln_b4, layer)
            x = ffn(x, crow, layer, 1)
        outs[name] = x

    y_prompt = outs["ctx"].reshape(batch, seq, D_MODEL)
    y_sample = outs["lat"].reshape(dec_batch, dec_seq, D_MODEL)
    return (y_prompt, y_sample, jnp.concatenate(new_k, axis=1), jnp.concatenate(new_v, axis=1))
```

```python
import functools
import math

import numpy as np
import jax
import jax.numpy as jnp
from jax import lax
from jax.experimental import pallas as pl
from jax.experimental.pallas import tpu as pltpu

D_MODEL = 1024
DEPTH = 2
GRID_W = 64
N_HEADS = 16
HEAD_DIM = D_MODEL // N_HEADS
WIN_ROWS = 8
WIN_COLS = 16
D_FF = 2816
N_MOD = 9
HY_BANDS = 16
HY_EMB = 1 + 2 * HY_BANDS
HY_FILTER_W = 64
HY_FAST_DECAY = 0.3
HY_SLOW_DECAY = 1.5
HY_TARGET = 1e-2
ALPHA = (2 * DEPTH) ** 0.25
LN_EPS = 1e-5

F32 = jnp.float32
BF16 = jnp.bfloat16
NEG = -1e30
LANES = 128
VMEM_LIMIT = 56 * 1024 * 1024
COND_ROWS = 8

Q_ROWS = 8
K_ROWS = 16
BAND_ROWS = 10
SPLIT_ROWS = 256


def _cparams(*sem):
    return pltpu.CompilerParams(dimension_semantics=sem, vmem_limit_bytes=VMEM_LIMIT)


def _dot(a, b):
    return jnp.dot(a, b, preferred_element_type=F32)


def _dot_nt(a, b):
    return lax.dot_general(a, b, (((1,), (1,)), ((), ())), preferred_element_type=F32)


def _layer_norm(y, g, b):
    mu = jnp.mean(y, axis=-1, keepdims=True)
    yc = y - mu
    var = jnp.mean(yc * yc, axis=-1, keepdims=True)
    return yc * lax.rsqrt(var + LN_EPS) * g + b


def _mods_kernel(c_ref, w_ref, b_ref, o_ref):
    c = c_ref[...]
    s = c * jax.nn.sigmoid(c)
    o_ref[...] = _dot(s.astype(BF16), w_ref[...].astype(BF16)) + b_ref[...]


def _mods(conds, mod_w, mod_b):
    tn = 1024
    nd = N_MOD * D_MODEL
    return pl.pallas_call(
        _mods_kernel,
        grid=(DEPTH, nd // tn),
        in_specs=[
            pl.BlockSpec((COND_ROWS, D_MODEL), lambda l, j: (0, 0)),
            pl.BlockSpec((None, D_MODEL, tn), lambda l, j: (l, 0, j)),
            pl.BlockSpec((None, 1, tn), lambda l, j: (l, 0, j)),
        ],
        out_specs=pl.BlockSpec((None, COND_ROWS, tn), lambda l, j: (l, 0, j)),
        out_shape=jax.ShapeDtypeStruct((DEPTH, COND_ROWS, nd), F32),
        compiler_params=_cparams("arbitrary", "arbitrary"),
        name="mods",
    )(conds, mod_w, mod_b.reshape(DEPTH, 1, nd))


def _ffn_kernel(x_ref, mod_ref, wi_ref, wo_ref, g_ref, b_ref, *rest, mi, n_chunks, after_mixer):
    x = x_ref[...]
    if after_mixer:
        a_ref, wmix_ref, gmix_ref, bmix_ref, o_ref = rest
        x = _layer_norm(ALPHA * x + mod_ref[5:6, :] * _dot(a_ref[...], wmix_ref[...]), gmix_ref[...], bmix_ref[...])
    else:
        o_ref, = rest
    sh = mod_ref[3 * mi:3 * mi + 1, :]
    sc = mod_ref[3 * mi + 1:3 * mi + 2, :]
    gt = mod_ref[3 * mi + 2:3 * mi + 3, :]
    h = (x * (1.0 + sc) + sh).astype(BF16)
    tf = D_FF // n_chunks
    f = None
    for c in range(n_chunks):
        gate = _dot(h, wi_ref[:, c * tf:(c + 1) * tf])
        up = _dot(h, wi_ref[:, D_FF + c * tf:D_FF + (c + 1) * tf])
        a = (gate * jax.nn.sigmoid(gate) * up).astype(BF16)
        part = _dot(a, wo_ref[c * tf:(c + 1) * tf, :])
        f = part if f is None else f + part
    y = ALPHA * x + (0.5 * gt) * f
    o_ref[...] = _layer_norm(y, g_ref[...], b_ref[...])


def _resident(shape, index_map):
    return pl.BlockSpec(shape, index_map, pipeline_mode=pl.Buffered(1))


def _ffn(x, mods, crow, w_in, w_out, ln_g, ln_b, layer, which, mixer_out=None, tm=1024, n_chunks=11):
    m = x.shape[0]
    mi = 0 if which == 0 else 2
    ln_spec = lambda idx: pl.BlockSpec((None, None, 1, D_MODEL), lambda i: (layer, idx, 0, 0))
    in_specs = [
        pl.BlockSpec((tm, D_MODEL), lambda i: (i, 0)),
        pl.BlockSpec((None, None, N_MOD, D_MODEL), lambda i: (layer, crow(i * tm), 0, 0)),
        _resident((None, None, D_MODEL, 2 * D_FF), lambda i: (layer, which, 0, 0)),
        _resident((None, None, D_FF, D_MODEL), lambda i: (layer, which, 0, 0)),
        ln_spec(mi), ln_spec(mi),
    ]
    args = [x, mods, w_in, w_out, ln_g, ln_b]
    if mixer_out is not None:
        a, w_mix, widx = mixer_out
        in_specs += [pl.BlockSpec((tm, D_MODEL), lambda i: (i, 0)),
                     _resident((None, D_MODEL, D_MODEL), lambda i: (widx, 0, 0)), ln_spec(1), ln_spec(1)]
        args += [a, w_mix, ln_g, ln_b]
    return pl.pallas_call(
        functools.partial(_ffn_kernel, mi=mi, n_chunks=n_chunks, after_mixer=mixer_out is not None),
        grid=(m // tm,),
        in_specs=in_specs,
        out_specs=pl.BlockSpec((tm, D_MODEL), lambda i: (i, 0)),
        out_shape=jax.ShapeDtypeStruct((m, D_MODEL), F32),
        compiler_params=_cparams("arbitrary"),
        name="ffn_mix" if mixer_out is not None else "ffn",
    )(*args)


def _proj_kernel(x_ref, mod_ref, w_ref, *rest, seq_len, tn):
    if seq_len:
        cw_ref, cb_ref, o_ref, h_ref = rest
    else:
        o_ref, h_ref = rest

    @pl.when(pl.program_id(1) == 0)
    def _():
        sh = mod_ref[3:4, :]
        sc = mod_ref[4:5, :]
        h_ref[...] = (x_ref[...] * (1.0 + sc) + sh).astype(BF16)

    tm = h_ref.shape[0]
    if seq_len:
        pos = lax.broadcasted_iota(jnp.int32, (tm, 1), 0) % seq_len
        first, last = pos == 0, pos == seq_len - 1
    for cc in range(D_MODEL // tn):
        cols = slice(cc * tn, (cc + 1) * tn)
        u = _dot(h_ref[...], w_ref[:, cols])
        if seq_len:
            prev = jnp.where(first, 0.0, pltpu.roll(u, 1, 0))
            nxt = jnp.where(last, 0.0, pltpu.roll(u, tm - 1, 0))
            u = prev * cw_ref[0:1, cols] + u * cw_ref[1:2, cols] + nxt * cw_ref[2:3, cols] + cb_ref[:, cols]
        o_ref[:, cols] = u.astype(o_ref.dtype)


def _proj(x, mods, crow, w, layer, widx, *, tm, tn, out_dtype, conv=None, seq_len=0):
    m = x.shape[0]
    in_specs = [
        pl.BlockSpec((tm, D_MODEL), lambda i, g: (i, 0)),
        pl.BlockSpec((None, None, N_MOD, D_MODEL), lambda i, g: (layer, crow(i * tm), 0, 0)),
        pl.BlockSpec((None, D_MODEL, D_MODEL), lambda i, g: (widx, 0, g)),
    ]
    args = [x, mods, w]
    if conv is not None:
        cw, cb = conv
        in_specs += [
            pl.BlockSpec((None, 3, D_MODEL), lambda i, g: (widx, 0, g)),
            pl.BlockSpec((None, 1, D_MODEL), lambda i, g: (widx, 0, g)),
        ]
        args += [cw, cb.reshape(cb.shape[0], 1, cb.shape[1])]
    return pl.pallas_call(
        functools.partial(_proj_kernel, seq_len=seq_len, tn=tn),
        grid=(m // tm, 3),
        in_specs=in_specs,
        out_specs=pl.BlockSpec((None, tm, D_MODEL), lambda i, g: (g, i, 0)),
        out_shape=jax.ShapeDtypeStruct((3, m, D_MODEL), out_dtype),
        scratch_shapes=[pltpu.VMEM((tm, D_MODEL), BF16)],
        compiler_params=_cparams("arbitrary", "arbitrary"),
        name="proj_conv" if seq_len else "proj",
    )(*args)


def _lane_tiles(a):
    return [a[:, i:i + LANES] for i in range(0, a.shape[1], LANES)]


def _head_masks():
    lane = lax.broadcasted_iota(jnp.int32, (1, LANES), 1)
    return lane < HEAD_DIM, lane >= HEAD_DIM


def _attn_ctx_kernel(x_ref, mod_ref, wqkv_ref, wo_ref, g_ref, b_ref, o_ref, ko_ref, vo_ref,
                     q_scr, k_scr, v_scr, a_scr, *, seq, nseq):
    x = x_ref[...]
    h = (x * (1.0 + mod_ref[4:5, :]) + mod_ref[3:4, :]).astype(BF16)
    q_scr[...] = (_dot(h, wqkv_ref[:, :D_MODEL]) * HEAD_DIM ** -0.5).astype(BF16)
    k_scr[...] = _dot(h, wqkv_ref[:, D_MODEL:2 * D_MODEL])
    v_scr[...] = _dot(h, wqkv_ref[:, 2 * D_MODEL:])
    masks = _head_masks()
    for s in range(nseq):
        rows = slice(s * seq, (s + 1) * seq)
        ko_ref[s] = k_scr[rows, :].reshape(seq, N_HEADS, HEAD_DIM)
        vo_ref[s] = v_scr[rows, :].reshape(seq, N_HEADS, HEAD_DIM)
        for p in range(D_MODEL // LANES):
            sl = slice(p * LANES, (p + 1) * LANES)
            q2 = q_scr[rows, sl]
            k2 = k_scr[rows, sl]
            v2 = v_scr[rows, sl]
            o2 = jnp.zeros(q2.shape, F32)
            for hm in masks:
                kh = jnp.where(hm, k2, 0.0).astype(BF16)
                vh = jnp.where(hm, v2, 0.0).astype(BF16)
                sc = _dot_nt(q2, kh)
                e = jnp.exp(sc - jnp.max(sc, axis=-1, keepdims=True))
                inv = 1.0 / jnp.sum(e, axis=-1, keepdims=True)
                o2 = o2 + _dot(e.astype(BF16), vh) * inv
            a_scr[rows, sl] = o2.astype(BF16)
    y = ALPHA * x + mod_ref[5:6, :] * _dot(a_scr[...], wo_ref[...])
    o_ref[...] = _layer_norm(y, g_ref[...], b_ref[...])


def _attn_ctx(x, mods, crow, w_qkv, w_o, widx, ln_g, ln_b, layer, batch, seq, nseq=2):
    rows = nseq * seq
    cache_spec = pl.BlockSpec((nseq, None, seq, N_HEADS, HEAD_DIM), lambda i: (i, 0, 0, 0, 0))
    cache_shape = jax.ShapeDtypeStruct((batch, 1, seq, N_HEADS, HEAD_DIM), F32)
    return pl.pallas_call(
        functools.partial(_attn_ctx_kernel, seq=seq, nseq=nseq),
        grid=(batch // nseq,),
        in_specs=[
            pl.BlockSpec((rows, D_MODEL), lambda i: (i, 0)),
            pl.BlockSpec((None, None, N_MOD, D_MODEL), lambda i: (layer, crow(i * rows), 0, 0)),
            _resident((None, D_MODEL, 3 * D_MODEL), lambda i: (widx, 0, 0)),
            _resident((None, D_MODEL, D_MODEL), lambda i: (widx, 0, 0)),
            pl.BlockSpec((None, None, 1, D_MODEL), lambda i: (layer, 1, 0, 0)),
            pl.BlockSpec((None, None, 1, D_MODEL), lambda i: (layer, 1, 0, 0)),
        ],
        out_specs=[pl.BlockSpec((rows, D_MODEL), lambda i: (i, 0)), cache_spec, cache_spec],
        out_shape=[jax.ShapeDtypeStruct((batch * seq, D_MODEL), F32), cache_shape, cache_shape],
        scratch_shapes=[pltpu.VMEM((rows, D_MODEL), BF16), pltpu.VMEM((rows, D_MODEL), F32),
                        pltpu.VMEM((rows, D_MODEL), F32), pltpu.VMEM((rows, D_MODEL), BF16)],
        compiler_params=_cparams("arbitrary"),
        name="attn_ctx",
    )(x, mods, w_qkv, w_o, ln_g, ln_b)


def _nbr_geometry(rows):
    kr = min(WIN_ROWS, rows)
    blocks = []
    for blk in range(rows // Q_ROWS):
        r0 = blk * Q_ROWS
        lo = min(max(r0 - kr // 2, 0), rows - kr)
        key_start = min(lo, rows - K_ROWS)
        bands = []
        for r in range(r0, r0 + Q_ROWS):
            rs = min(max(r - kr // 2, 0), rows - kr)
            a = rs - key_start
            band_start = min(a - a % 2, K_ROWS - BAND_ROWS)
            off = a - band_start
            assert 0 <= band_start and 0 <= off and off + kr <= BAND_ROWS
            bands.append((band_start, off, rs - r + WIN_ROWS - 1))
        blocks.append((key_start, bands))
    return kr, blocks


def _nbr_bias_rows(rpb):
    pad = GRID_W - (2 * WIN_COLS - 1)
    row = jnp.pad(rpb, ((0, 0), (0, 0), (0, pad)))
    return jnp.concatenate([row[:, :-1], row[:, 1:]], axis=-1)


def _attn_nbr_kernel(q_ref, k_ref, v_ref, kc_ref, vc_ref, rp_ref, o_ref,
                     s_ref, sc_ref, p_ref, pc_ref, inv_ref, tab_ref, *, blocks, kr):
    masks = _head_masks()
    left = masks[0]
    qn = Q_ROWS * GRID_W
    kn = K_ROWS * GRID_W
    qcol = lax.broadcasted_iota(jnp.int32, (GRID_W, LANES), 0)
    kcol = lax.broadcasted_iota(jnp.int32, (GRID_W, LANES), 1) % GRID_W
    cstart = jnp.clip(qcol - WIN_COLS // 2, 0, GRID_W - WIN_COLS)
    col_ok = (kcol >= cstart) & (kcol < cstart + WIN_COLS)
    for hh in range(2):
        for d in range(2 * WIN_ROWS - 2):
            row = jnp.broadcast_to(rp_ref[hh, d:d + 1, :], (GRID_W, LANES))
            toep = pltpu.roll(row, LANES - (WIN_COLS - 1), 1, stride=1, stride_axis=0)
            tab_ref[hh, d] = jnp.where(col_ok, toep, NEG)
    for blk, (key_start, bands) in enumerate(blocks):
        q2 = (q_ref[blk * qn:(blk + 1) * qn, :] * HEAD_DIM ** -0.5).astype(BF16)
        k2 = k_ref[key_start * GRID_W:key_start * GRID_W + kn, :]
        v2 = v_ref[key_start * GRID_W:key_start * GRID_W + kn, :]
        o2 = jnp.zeros(q2.shape, F32)
        p_ref[...] = jnp.zeros_like(p_ref)
        for hh, hm in enumerate(masks):
            kh = jnp.where(hm, k2, 0.0).astype(BF16)
            vh = jnp.where(hm, v2, 0.0).astype(BF16)
            kch = jnp.where(hm, kc_ref[...], 0.0).astype(BF16)
            vch = jnp.where(hm, vc_ref[...], 0.0).astype(BF16)
            s_ref[...] = _dot_nt(q2, kh)
            sc_ref[...] = _dot_nt(q2, kch)
            for rl, (band_start, off, drow) in enumerate(bands):
                rsl = slice(rl * GRID_W, (rl + 1) * GRID_W)
                tiles = []
                for j in range(0, BAND_ROWS, 2):
                    ok0, ok1 = off <= j < off + kr, off <= j + 1 < off + kr
                    if not (ok0 or ok1):
                        continue
                    bias = tab_ref[hh, drow - off + j]
                    if not ok0:
                        bias = jnp.where(left, NEG, bias)
                    if not ok1:
                        bias = jnp.where(left, bias, NEG)
                    csl = slice((band_start + j) * GRID_W, (band_start + j + 2) * GRID_W)
                    tiles.append((csl, s_ref[rsl, csl] + bias))
                sc = sc_ref[rsl, :]
                top = functools.reduce(jnp.maximum, _lane_tiles(sc) + [sb for _, sb in tiles])
                mx = jnp.max(top, axis=-1, keepdims=True)
                ec = jnp.exp(sc - mx)
                tot = functools.reduce(jnp.add, _lane_tiles(ec))
                for csl, sb in tiles:
                    eb = jnp.exp(sb - mx)
                    tot = tot + eb
                    p_ref[rsl, csl] = eb.astype(BF16)
                inv_ref[rsl, :] = 1.0 / jnp.sum(tot, axis=-1, keepdims=True)
                pc_ref[rsl, :] = ec.astype(BF16)
            o2 = o2 + (_dot(p_ref[...], vh) + _dot(pc_ref[...], vch)) * inv_ref[...]
        o_ref[blk * qn:(blk + 1) * qn, :] = o2.astype(o_ref.dtype)


def _attn_nbr(qkv, kc, vc, rpb, batch, seq):
    rows = seq // GRID_W
    kr, blocks = _nbr_geometry(rows)
    n_pairs = D_MODEL // LANES
    n_d = 2 * WIN_ROWS - 2
    rp = _nbr_bias_rows(rpb).reshape(n_pairs, 2, n_d, LANES)
    s_ctx = kc.shape[1]
    qspec = lambda g: pl.BlockSpec((None, seq, LANES), lambda b, p: (g, b, p))
    cspec = pl.BlockSpec((None, s_ctx, LANES), lambda b, p: (b, 0, p))
    qn, kn = Q_ROWS * GRID_W, K_ROWS * GRID_W
    return pl.pallas_call(
        functools.partial(_attn_nbr_kernel, blocks=blocks, kr=kr),
        grid=(batch, n_pairs),
        in_specs=[qspec(0), qspec(1), qspec(2), cspec, cspec,
                  pl.BlockSpec((None, 2, n_d, LANES), lambda b, p: (p, 0, 0, 0))],
        out_specs=pl.BlockSpec((seq, LANES), lambda b, p: (b, p)),
        out_shape=jax.ShapeDtypeStruct((batch * seq, D_MODEL), BF16),
        scratch_shapes=[pltpu.VMEM((qn, kn), F32), pltpu.VMEM((qn, s_ctx), F32),
                        pltpu.VMEM((qn, kn), BF16), pltpu.VMEM((qn, s_ctx), BF16),
                        pltpu.VMEM((qn, 1), F32), pltpu.VMEM((2, n_d, GRID_W, LANES), F32)],
        compiler_params=_cparams("arbitrary", "arbitrary"),
        name="attn_nbr",
    )(qkv, qkv, qkv, kc, vc, rp)


def _dft_tables(seq_len, tq, tr):
    n = 2 * seq_len
    f = np.arange(seq_len, dtype=np.float64)[:, None] + 0.5
    s = np.arange(seq_len, dtype=np.float64)[None, :]
    ang = 2.0 * np.pi * f * s / n
    re = np.cos(ang).reshape(seq_len // tq, tq, seq_len)
    im = (-np.sin(ang)).reshape(seq_len // tq, tq, seq_len)
    fwd = np.concatenate([re, im], axis=1).astype(np.float32)
    bwd = fwd.reshape(2 * seq_len, seq_len).T
    return fwd, np.ascontiguousarray(bwd).reshape(seq_len // tr, tr, 2 * seq_len)


def _split_spectrum_table(seq_len, tq):
    half = seq_len // 2
    f = np.arange(half, dtype=np.float64)[:, None] + 0.5
    s = np.arange(seq_len, dtype=np.float64)[None, :]
    lo = np.pi * f * s / seq_len
    hi = np.pi * (seq_len - f) * s / seq_len
    parts = [np.cos(lo), -np.sin(lo), np.cos(hi), np.sin(hi)]
    return np.concatenate([p.reshape(half // tq, tq, seq_len) for p in parts], axis=1).astype(np.float32)


def _twiddles(seq_len):
    th = np.pi * (np.arange(seq_len // 2, dtype=np.float64) + 0.5) / seq_len
    return np.stack([np.cos(th), np.sin(th)])[:, :, None].repeat(LANES, axis=2).astype(np.float32)


def _deinterleave_matrix():
    sel = np.zeros((SPLIT_ROWS, SPLIT_ROWS), np.float32)
    i = np.arange(SPLIT_ROWS // 2)
    sel[i, 2 * i] = 1.0
    sel[SPLIT_ROWS // 2 + i, 2 * i + 1] = 1.0
    return sel


def _filter_features(seq_len):
    pos = np.arange(seq_len, dtype=np.float64)[:, None]
    t = pos / max(seq_len - 1, 1)
    bands = np.linspace(1e-4, HY_BANDS - 1, HY_BANDS)
    ang = 2.0 * np.pi * pos / seq_len * bands
    z = np.concatenate([t, np.cos(ang), -np.sin(ang)], axis=-1)
    zp = np.zeros((seq_len, HY_FILTER_W), np.float64)
    zp[:, :HY_EMB] = z
    deltas = np.abs(np.linspace(math.log(HY_TARGET) / HY_SLOW_DECAY, math.log(HY_TARGET) / HY_FAST_DECAY, D_MODEL))
    return zp.astype(np.float32), t.astype(np.float32), deltas.astype(np.float32)[None, :]


def _filter_kernel(z_ref, t_ref, dl_ref, w1_ref, b1_ref, f1_ref, w2_ref, b2_ref, f2_ref, w3_ref, b3_ref, o_ref,
                   hid_ref, decay_ref):
    hp = lax.Precision.HIGHEST

    @pl.when(pl.program_id(0) == 0)
    def _():
        hid = jnp.sin(f1_ref[...] * (jnp.dot(z_ref[...], w1_ref[...], precision=hp, preferred_element_type=F32) + b1_ref[...]))
        hid_ref[...] = jnp.sin(f2_ref[...] * (jnp.dot(hid, w2_ref[...], precision=hp, preferred_element_type=F32) + b2_ref[...]))
        decay_ref[...] = jnp.exp(-t_ref[...] * dl_ref[...])

    hid, w3 = hid_ref[...], w3_ref[...]
    hid_hi, w3_hi = hid.astype(BF16), w3.astype(BF16)
    hid_lo = (hid - hid_hi.astype(F32)).astype(BF16)
    w3_lo = (w3 - w3_hi.astype(F32)).astype(BF16)
    filt = _dot(hid_hi, w3_hi) + (_dot(hid_lo, w3_hi) + _dot(hid_hi, w3_lo)) + b3_ref[...]
    decay = decay_ref[...]
    hf = filt[:, :D_MODEL] * decay
    hb = filt[:, D_MODEL:] * decay
    row = lax.broadcasted_iota(jnp.int32, (hb.shape[0], 1), 0)
    hb = jnp.where(row == 0, 0.0, hb)
    o_ref[:, :D_MODEL] = (hf + hb).astype(o_ref.dtype)
    o_ref[:, D_MODEL:] = (hf - hb).astype(o_ref.dtype)


def _filters(seq_len, w1, b1, f1, w2, b2, f2, w3, b3):
    z, t, deltas = _filter_features(seq_len)
    w1p = jnp.zeros((HY_FILTER_W, HY_FILTER_W), F32).at[:HY_EMB].set(w1)
    full = lambda shape: pl.BlockSpec(shape, lambda o: (0,) * len(shape))
    row = lambda a: a.reshape(1, -1)
    return pl.pallas_call(
        _filter_kernel,
        grid=(2,),
        in_specs=[full((seq_len, HY_FILTER_W)), full((seq_len, 1)), full((1, D_MODEL)),
                  full((HY_FILTER_W, HY_FILTER_W)), full((1, HY_FILTER_W)), full((1, HY_FILTER_W)),
                  full((HY_FILTER_W, HY_FILTER_W)), full((1, HY_FILTER_W)), full((1, HY_FILTER_W)),
                  pl.BlockSpec((HY_FILTER_W, 2 * D_MODEL), lambda o: (0, o)),
                  pl.BlockSpec((1, 2 * D_MODEL), lambda o: (0, o))],
        out_specs=pl.BlockSpec((None, seq_len, 2 * D_MODEL), lambda o: (o, 0, 0)),
        out_shape=jax.ShapeDtypeStruct((2, seq_len, 2 * D_MODEL), BF16),
        scratch_shapes=[pltpu.VMEM((seq_len, HY_FILTER_W), F32), pltpu.VMEM((seq_len, D_MODEL), F32)],
        compiler_params=_cparams("arbitrary"),
        name="hyena_filters",
    )(jnp.asarray(z),jnp.asarray(t), jnp.asarray(deltas), w1p, row(b1), row(f1), w2, row(b2), row(f2), w3, row(b3))


def _spectrum_kernel(g_ref, hs_ref, hd_ref, o_ref, *, tq, scale):
    for part in range(g_ref.shape[0] // tq):
        rows = slice(part * tq, (part + 1) * tq)
        h_ref = hs_ref if part % 2 == 0 else hd_ref
        o_ref[rows, :] = scale * _dot(g_ref[rows, :], h_ref[...])


def _spectrum(table, filt, seq_len, tq, tc):
    nt, rows, _ = table.shape
    ncb = D_MODEL // tc
    return pl.pallas_call(
        functools.partial(_spectrum_kernel, tq=tq, scale=1.0 / seq_len),
        grid=(2, ncb, nt),
        in_specs=[pl.BlockSpec((None, rows, seq_len), lambda o, c, f: (f, 0, 0)),
                  pl.BlockSpec((None, seq_len, tc), lambda o, c, f: (o, 0, c)),
                  pl.BlockSpec((None, seq_len, tc), lambda o, c, f: (o, 0, c + ncb))],
        out_specs=pl.BlockSpec((None, None, rows, tc), lambda o, c, f: (o, f, 0, c)),
        out_shape=jax.ShapeDtypeStruct((2, nt, rows, D_MODEL), F32),
        compiler_params=_cparams("arbitrary", "arbitrary", "arbitrary"),
        name="hyena_spectrum",
    )(table, filt, filt)


def _hyena_kernel(v_ref, gate_ref, g_ref, gt_ref, k_ref, tw_ref, sel_ref, selt_ref, b_ref, o_ref,
                  ze_ref, zo_ref, pu_ref, pv_ref, *, tq, nf, tr):
    o = pl.program_id(2)
    st = pl.program_id(3)
    half = ze_ref.shape[0]
    tc = ze_ref.shape[1]
    hs = SPLIT_ROWS // 2

    @pl.when((o == 0) & (st == 0))
    def _():
        for j in range(half // hs):
            sp = _dot(sel_ref[...], v_ref[j * SPLIT_ROWS:(j + 1) * SPLIT_ROWS, :])
            ze_ref[j * hs:(j + 1) * hs, :] = sp[:hs].astype(BF16)
            zo_ref[j * hs:(j + 1) * hs, :] = sp[hs:].astype(BF16)

    @pl.when(st < nf)
    def _():
        base = pl.multiple_of(st * 2 * tq, 2 * tq)
        c, s = tw_ref[0], tw_ref[1]
        group = 2 * LANES
        for lt in range(tc // LANES):
            cols = slice(lt * LANES, (lt + 1) * LANES)
            if lt % 2 == 0:
                gcols = slice(lt * LANES, lt * LANES + group)
                fa = _dot(g_ref[...], ze_ref[:, gcols])
                fb = _dot(g_ref[...], zo_ref[:, gcols])
            loc = slice((lt % 2) * LANES, (lt % 2 + 1) * LANES)
            ar, ai, br, bi = fa[:tq, loc], fa[tq:, loc], fb[:tq, loc], fb[tq:, loc]
            tr_, ti = c * br + s * bi, c * bi - s * br
            s1r, s1i, s2r, s2i = ar + tr_, ai + ti, ar - tr_, ai - ti
            k1r, k1i = k_ref[0:tq, cols], k_ref[tq:2 * tq, cols]
            k2r, k2i = k_ref[2 * tq:3 * tq, cols], k_ref[3 * tq:, cols]
            p1r, p1i = s1r * k1r - s1i * k1i, s1r * k1i + s1i * k1r
            p2r, p2i = s2r * k2r - s2i * k2i, s2r * k2i + s2i * k2r
            wr, wi = p1r - p2r, p1i - p2i
            pu_ref[pl.ds(base, tq), cols] = (p1r + p2r).astype(BF16)
            pu_ref[pl.ds(base + tq, tq), cols] = (p1i + p2i).astype(BF16)
            pv_ref[pl.ds(base, tq), cols] = (c * wr - s * wi).astype(BF16)
            pv_ref[pl.ds(base + tq, tq), cols] = (c * wi + s * wr).astype(BF16)

    @pl.when(st >= nf)
    def _():
        r0 = pl.multiple_of((st - nf) * tr, tr)
        rows = pl.ds(r0, tr)
        ye = _dot(gt_ref[...], pu_ref[...]) + ze_ref[rows, :].astype(F32) * b_ref[...]
        yo = _dot(gt_ref[...], pv_ref[...]) + zo_ref[rows, :].astype(F32) * b_ref[...]
        block = lambda j: pl.ds(pl.multiple_of(2 * r0 + j * SPLIT_ROWS, SPLIT_ROWS), SPLIT_ROWS)
        gates = [_dot(sel_ref[...], gate_ref[block(j), :]) for j in range(tr // hs)]
        ge = ye * jnp.concatenate([g[:hs] for g in gates], axis=0)
        go = yo * jnp.concatenate([g[hs:] for g in gates], axis=0)

        @pl.when(o == 0)
        def _():
            ze_ref[rows, :] = ge.astype(BF16)
            zo_ref[rows, :] = go.astype(BF16)

        @pl.when(o == 1)
        def _():
            for j in range(tr // hs):
                blk = jnp.concatenate([ge[j * hs:(j + 1) * hs], go[j * hs:(j + 1) * hs]], axis=0).astype(BF16)
                o_ref[block(j), :] = _dot(selt_ref[...], blk).astype(o_ref.dtype)


def _hyena_core(u, fwd_h, bwd_h, spec, tw, sel, bias, batch, seq_len, tq, tc):
    half = seq_len // 2
    nf = half // tq
    nr, tr, _ = bwd_h.shape
    ncb = D_MODEL // tc
    fidx = lambda st: jnp.minimum(st, nf - 1)
    whole = lambda shape: pl.BlockSpec(shape, lambda b, c, o, st: (0,) * len(shape))
    return pl.pallas_call(
        functools.partial(_hyena_kernel, tq=tq, nf=nf, tr=tr),
        grid=(batch, ncb, 2, nf + nr),
        in_specs=[pl.BlockSpec((None, seq_len, tc), lambda b, c, o, st: (0, b, c)),
                  pl.BlockSpec((None, seq_len, tc), lambda b, c, o, st: (1 + o, b, c)),
                  pl.BlockSpec((None, 2 * tq, half), lambda b, c, o, st: (fidx(st), 0, 0)),
                  pl.BlockSpec((None, tr, seq_len), lambda b, c, o, st: (jnp.maximum(st - nf, 0), 0, 0)),
                  pl.BlockSpec((None, None, 4 * tq, tc), lambda b, c, o, st: (o, fidx(st), 0, c)),
                  pl.BlockSpec((2, tq, LANES), lambda b, c, o, st: (0, fidx(st), 0)),
                  whole((SPLIT_ROWS, SPLIT_ROWS)), whole((SPLIT_ROWS, SPLIT_ROWS)),
                  pl.BlockSpec((None, 1, tc), lambda b, c, o, st: (o, 0, c))],
        out_specs=pl.BlockSpec((seq_len, tc), lambda b, c, o, st: (b, c)),
        out_shape=jax.ShapeDtypeStruct((batch * seq_len, D_MODEL), BF16),
        scratch_shapes=[pltpu.VMEM((half, tc), BF16), pltpu.VMEM((half, tc), BF16),
                        pltpu.VMEM((seq_len, tc), BF16), pltpu.VMEM((seq_len, tc), BF16)],
        compiler_params=_cparams("arbitrary", "arbitrary", "arbitrary", "arbitrary"),
        name="hyena_conv",
    )(u, u, fwd_h, bwd_h, spec, tw, sel, sel.T, bias.reshape(2, 1, D_MODEL))


def _hyena_layer_kernel(x_ref, mod_ref, win_ref, cw_ref, cb_ref, g_ref, gt_ref, k_ref, hb_ref, wout_ref,
                        lg_ref, lb_ref, o_ref, u_scr, *, seq_len, nseq, tn):
    x = x_ref[...]
    h = (x * (1.0 + mod_ref[4:5, :]) + mod_ref[3:4, :]).astype(BF16)
    rows = nseq * seq_len
    pos = lax.broadcasted_iota(jnp.int32, (rows, 1), 0) % seq_len
    first, last = pos == 0, pos == seq_len - 1
    for g in range(3):
        for cc in range(D_MODEL // tn):
            cols = slice(g * D_MODEL + cc * tn, g * D_MODEL + (cc + 1) * tn)
            u = _dot(h, win_ref[:, cols])
            prev = jnp.where(first, 0.0, pltpu.roll(u, 1, 0))
            nxt = jnp.where(last, 0.0, pltpu.roll(u, rows - 1, 0))
            u = prev * cw_ref[0:1, cols] + u * cw_ref[1:2, cols] + nxt * cw_ref[2:3, cols] + cb_ref[:, cols]
            u_scr[g, :, cc * tn:(cc + 1) * tn] = u.astype(BF16)
    for o in range(2):
        kr, ki = k_ref[o, :seq_len, :], k_ref[o, seq_len:, :]
        for s in range(nseq):
            r = slice(s * seq_len, (s + 1) * seq_len)
            z = u_scr[0, r, :]
            zf = _dot(g_ref[...], z)
            zr, zi = zf[:seq_len], zf[seq_len:]
            prod = jnp.concatenate([zr * kr - zi * ki, zr * ki + zi * kr], axis=0).astype(BF16)
            y = _dot(gt_ref[...], prod) + z.astype(F32) * hb_ref[o]
            u_scr[0, r, :] = (y * u_scr[1 + o, r, :].astype(F32)).astype(BF16)
    y = ALPHA * x + mod_ref[5:6, :] * _dot(u_scr[0], wout_ref[...])
    o_ref[...] = _layer_norm(y, lg_ref[...], lb_ref[...])


def _hyena_layer(x, mods, crow, w_in, w_out, widx, conv_w, conv_b, fwd, bwd, spec, bias, ln_g, ln_b, layer,
                 seq_len, nseq, tn=512):
    m = x.shape[0]
    rows = nseq * seq_len
    assert fwd.shape[0] == 1
    return pl.pallas_call(
        functools.partial(_hyena_layer_kernel, seq_len=seq_len, nseq=nseq, tn=tn),
        grid=(m // rows,),
        in_specs=[
            pl.BlockSpec((rows, D_MODEL), lambda i: (i, 0)),
            pl.BlockSpec((None, None, N_MOD, D_MODEL), lambda i: (layer, crow(i * rows), 0, 0)),
            _resident((None, D_MODEL, 3 * D_MODEL), lambda i: (widx, 0, 0)),
            pl.BlockSpec((None, 3, 3 * D_MODEL), lambda i: (widx, 0, 0)),
            pl.BlockSpec((None, 1, 3 * D_MODEL), lambda i: (widx, 0, 0)),
            pl.BlockSpec((None, 2 * seq_len, seq_len), lambda i: (0, 0, 0)),
            pl.BlockSpec((None, seq_len, 2 * seq_len), lambda i: (0, 0, 0)),
            _resident((2, None, 2 * seq_len, D_MODEL), lambda i: (0, 0, 0, 0)),
            pl.BlockSpec((2, 1, D_MODEL), lambda i: (0, 0, 0)),
            _resident((None, D_MODEL, D_MODEL), lambda i: (widx, 0, 0)),
            pl.BlockSpec((None, None, 1, D_MODEL), lambda i: (layer, 1, 0, 0)),
            pl.BlockSpec((None, None, 1, D_MODEL), lambda i: (layer, 1, 0, 0)),
        ],
        out_specs=pl.BlockSpec((rows, D_MODEL), lambda i: (i, 0)),
        out_shape=jax.ShapeDtypeStruct((m, D_MODEL), F32),
        scratch_shapes=[pltpu.VMEM((3, rows, D_MODEL), BF16)],
        compiler_params=_cparams("arbitrary"),
        name="hyena_layer",
    )(x, mods, w_in, conv_w, conv_b.reshape(conv_b.shape[0], 1, conv_b.shape[1]), fwd, bwd, spec,
      bias.reshape(2, 1, D_MODEL), w_out, ln_g, ln_b)


def _hyena_tiles(seq_len, batch):
    tq = tr = min(seq_len, 512)
    tc = D_MODEL if seq_len <= 256 else 512
    nseq = math.gcd(batch, max(1, 1024 // seq_len))
    return tq, tr, tc, nseq


def kernel(x_prompt, x_sample, cache_k, cache_v, c, c_ctx, mod_w, mod_b, ln_g, ln_b, ffn_w_in, ffn_w_out,
           attn_w_qkv, attn_w_o, attn_rpb, hy_w_in, hy_conv_w, hy_conv_b, hy_f_w1, hy_f_b1, hy_f_freq1,
           hy_f_w2, hy_f_b2, hy_f_freq2, hy_f_w3, hy_f_b3, hy_bias, hy_w_out):
    batch, seq, _ = x_prompt.shape
    dec_batch, dec_seq, _ = x_sample.shape
    assert dec_batch + 1 <= COND_ROWS

    conds = jnp.zeros((COND_ROWS, D_MODEL), F32).at[0].set(c_ctx).at[1:1 + dec_batch].set(c)
    mods = _mods(conds, mod_w, mod_b).reshape(DEPTH, COND_ROWS, N_MOD, D_MODEL)
    ln_g4 = ln_g.reshape(DEPTH, 3, 1, D_MODEL)
    ln_b4 = ln_b.reshape(DEPTH, 3, 1, D_MODEL)

    streams = {
        "ctx": dict(x=x_prompt.reshape(batch * seq, D_MODEL), batch=batch, seq=seq, crow=lambda t: 0),
        "lat": dict(x=x_sample.reshape(dec_batch * dec_seq, D_MODEL), batch=dec_batch, seq=dec_seq,
                    crow=lambda t: 1 + t // dec_seq),
    }
    as_bf16 = lambda table: jnp.asarray(table).astype(BF16)
    hyena_consts = {}
    for name, st in streams.items():
        sl = st["seq"]
        tq, tr, tc, nseq = _hyena_tiles(sl, st["batch"])
        if sl == tq:
            fwd, bwd = _dft_tables(sl, tq, tr)
            hyena_consts[name] = dict(tq=tq, tc=tc, nseq=nseq, fwd=as_bf16(fwd), bwd=as_bf16(bwd), ktab=as_bf16(fwd))
        else:
            fwd, bwd = _dft_tables(sl // 2, tq, tr)
            hyena_consts[name] = dict(tq=tq, tc=tc, nseq=nseq, fwd=as_bf16(fwd), bwd=as_bf16(bwd),
                                      ktab=as_bf16(_split_spectrum_table(sl, tq)),
                                      tw=jnp.asarray(_twiddles(sl)), sel=as_bf16(_deinterleave_matrix()))

    ffn_w_in, ffn_w_out, attn_w_qkv, attn_w_o, hy_w_in, hy_w_out = (
        w.astype(BF16) for w in (ffn_w_in, ffn_w_out, attn_w_qkv, attn_w_o, hy_w_in, hy_w_out))

    new_k, new_v = [], []
    outs = {}
    for name, st in streams.items():
        x, nb, sl, crow = st["x"], st["batch"], st["seq"], st["crow"]
        for layer in range(DEPTH):
            j = layer // 2
            x = _ffn(x, mods, crow, ffn_w_in, ffn_w_out, ln_g4, ln_b4, layer, 0)
            mixer_out = None
            if layer % 2 == 0:
                if name == "ctx":
                    x, k_new, v_new = _attn_ctx(x, mods, crow, attn_w_qkv, attn_w_o, j, ln_g4, ln_b4, layer, nb, sl)
                    new_k.append(k_new)
                    new_v.append(v_new)
                else:
                    qkv = _proj(x, mods, crow, attn_w_qkv, layer, j, tm=1024, tn=1024, out_dtype=BF16)
                    kc = cache_k[:, j].reshape(nb, -1, D_MODEL)
                    vc = cache_v[:, j].reshape(nb, -1, D_MODEL)
                    mixer_out = (_attn_nbr(qkv, kc, vc, attn_rpb[j], nb, sl), attn_w_o, j)
            else:
                hc = hyena_consts[name]
                tq, tc = hc["tq"], hc["tc"]
                filt = _filters(sl, hy_f_w1[j], hy_f_b1[j], hy_f_freq1[j], hy_f_w2[j], hy_f_b2[j], hy_f_freq2[j],
                                hy_f_w3[j], hy_f_b3[j])
                spec = _spectrum(hc["ktab"], filt, sl, tq, tc)
                if sl == tq:
                    x = _hyena_layer(x, mods, crow, hy_w_in, hy_w_out, j, hy_conv_w, hy_conv_b, hc["fwd"], hc["bwd"],
                                     spec, hy_bias[j], ln_g4, ln_b4, layer, sl, hc["nseq"])
                else:
                    u = _proj(x, mods, crow, hy_w_in, layer, j, tm=max(sl, 1024), tn=512, out_dtype=BF16,
                              conv=(hy_conv_w, hy_conv_b), seq_len=sl)
                    a = _hyena_core(u, hc["fwd"], hc["bwd"], spec, hc["tw"], hc["sel"], hy_bias[j], nb, sl, tq, tc)
                    mixer_out = (a, hy_w_out, j)
            x = _ffn(x, mods, crow, ffn_w_in, ffn_w_out, ln_g4, ln_b4, layer, 1, mixer_out=mixer_out)
        outs[name] = x

    y_prompt = outs["ctx"].reshape(batch, seq, D_MODEL)
    y_sample = outs["lat"].reshape(dec_batch, dec_seq, D_MODEL)
    return (y_prompt, y_sample, jnp.concatenate(new_k, axis=1), jnp.concatenate(new_v, axis=1))
```

```python
import functools
import math

import numpy as np
import jax
import jax.numpy as jnp
from jax import lax
from jax.experimental import pallas as pl
from jax.experimental.pallas import tpu as pltpu

D_MODEL = 1024
DEPTH = 2
GRID_W = 64
N_HEADS = 16
HEAD_DIM = D_MODEL // N_HEADS
WIN_ROWS = 8
WIN_COLS = 16
D_FF = 2816
N_MOD = 9
HY_BANDS = 16
HY_EMB = 1 + 2 * HY_BANDS
HY_FILTER_W = 64
HY_FAST_DECAY = 0.3
HY_SLOW_DECAY = 1.5
HY_TARGET = 1e-2
ALPHA = (2 * DEPTH) ** 0.25
LN_EPS = 1e-5

F32 = jnp.float32
BF16 = jnp.bfloat16
NEG = -1e30
LANES = 128
VMEM_LIMIT = 56 * 1024 * 1024
COND_ROWS = 8

Q_ROWS = 8
K_ROWS = 16
BAND_ROWS = 10
SPLIT_ROWS = 256


def _cparams(*sem):
    return pltpu.CompilerParams(dimension_semantics=sem, vmem_limit_bytes=VMEM_LIMIT)


def _dot(a, b):
    return jnp.dot(a, b, preferred_element_type=F32)


def _dot_nt(a, b):
    return lax.dot_general(a, b, (((1,), (1,)), ((), ())), preferred_element_type=F32)


def _layer_norm(y, g, b):
    mu = jnp.mean(y, axis=-1, keepdims=True)
    yc = y - mu
    var = jnp.mean(yc * yc, axis=-1, keepdims=True)
    return yc * lax.rsqrt(var + LN_EPS) * g + b


def _mods_kernel(c_ref, w_ref, b_ref, o_ref):
    c = c_ref[...]
    s = c * jax.nn.sigmoid(c)
    o_ref[...] = _dot(s.astype(BF16), w_ref[...].astype(BF16)) + b_ref[...]


def _mods(conds, mod_w, mod_b):
    tn = 1024
    nd = N_MOD * D_MODEL
    return pl.pallas_call(
        _mods_kernel,
        grid=(DEPTH, nd // tn),
        in_specs=[
            pl.BlockSpec((COND_ROWS, D_MODEL), lambda l, j: (0, 0)),
            pl.BlockSpec((None, D_MODEL, tn), lambda l, j: (l, 0, j)),
            pl.BlockSpec((None, 1, tn), lambda l, j: (l, 0, j)),
        ],
        out_specs=pl.BlockSpec((None, COND_ROWS, tn), lambda l, j: (l, 0, j)),
        out_shape=jax.ShapeDtypeStruct((DEPTH, COND_ROWS, nd), F32),
        compiler_params=_cparams("arbitrary", "arbitrary"),
        name="mods",
    )(conds, mod_w, mod_b.reshape(DEPTH, 1, nd))


def _ffn_kernel(x_ref, mod_ref, wi_ref, wo_ref, g_ref, b_ref, *rest, mi, n_chunks, after_mixer, cast_next):
    rest = list(rest)
    x = x_ref[...]
    if after_mixer:
        a_ref, wmix_ref, gmix_ref, bmix_ref = rest[:4]
        del rest[:4]
        x = _layer_norm(ALPHA * x + mod_ref[5:6, :] * _dot(a_ref[...], wmix_ref[...]), gmix_ref[...], bmix_ref[...])
    if cast_next:
        wi_next_ref, wo_next_ref, o_ref, wi_cast_ref, wo_cast_ref = rest
        wi_cast_ref[...] = wi_next_ref[...].astype(BF16)
        wo_cast_ref[...] = wo_next_ref[...].astype(BF16)
    else:
        o_ref, = rest
    sh = mod_ref[3 * mi:3 * mi + 1, :]
    sc = mod_ref[3 * mi + 1:3 * mi + 2, :]
    gt = mod_ref[3 * mi + 2:3 * mi + 3, :]
    h = (x * (1.0 + sc) + sh).astype(BF16)
    tf = D_FF // n_chunks
    f = None
    for c in range(n_chunks):
        gate = _dot(h, wi_ref[:, c * tf:(c + 1) * tf])
        up = _dot(h, wi_ref[:, D_FF + c * tf:D_FF + (c + 1) * tf])
        a = (gate * jax.nn.sigmoid(gate) * up).astype(BF16)
        part = _dot(a, wo_ref[c * tf:(c + 1) * tf, :])
        f = part if f is None else f + part
    y = ALPHA * x + (0.5 * gt) * f
    o_ref[...] = _layer_norm(y, g_ref[...], b_ref[...])


def _resident(shape, index_map):
    return pl.BlockSpec(shape, index_map, pipeline_mode=pl.Buffered(1))


BF16_ROWS = 16


def _ffn_can_cast(m, tm=1024):
    steps = m // tm
    return D_MODEL % (steps * BF16_ROWS) == 0 and D_FF % (steps * BF16_ROWS) == 0


def _ffn(x, mods, crow, w_in, w_out, ln_g, ln_b, layer, which, mixer_out=None, cast_next=None, tm=1024, n_chunks=11):
    m = x.shape[0]
    steps = m // tm
    mi = 0 if which == 0 else 2
    ln_spec = lambda idx: pl.BlockSpec((None, None, 1, D_MODEL), lambda i: (layer, idx, 0, 0))
    in_specs = [
        pl.BlockSpec((tm, D_MODEL), lambda i: (i, 0)),
        pl.BlockSpec((None, None, N_MOD, D_MODEL), lambda i: (layer, crow(i * tm), 0, 0)),
        _resident((D_MODEL, 2 * D_FF), lambda i: (0, 0)),
        _resident((D_FF, D_MODEL), lambda i: (0, 0)),
        ln_spec(mi), ln_spec(mi),
    ]
    args = [x, mods, w_in, w_out, ln_g, ln_b]
    out_specs = [pl.BlockSpec((tm, D_MODEL), lambda i: (i, 0))]
    out_shape = [jax.ShapeDtypeStruct((m, D_MODEL), F32)]
    if mixer_out is not None:
        a, w_mix, widx = mixer_out
        in_specs += [pl.BlockSpec((tm, D_MODEL), lambda i: (i, 0)),
                     _resident((None, D_MODEL, D_MODEL), lambda i: (widx, 0, 0)), ln_spec(1), ln_spec(1)]
        args += [a, w_mix, ln_g, ln_b]
    if cast_next is not None:
        w_in_all, w_out_all, layer2, which2 = cast_next
        slab_in, slab_out = D_MODEL // steps, D_FF // steps
        in_specs += [pl.BlockSpec((None, None, slab_in, 2 * D_FF), lambda i: (layer2, which2, i, 0)),
                     pl.BlockSpec((None, None, slab_out, D_MODEL), lambda i: (layer2, which2, i, 0))]
        args += [w_in_all, w_out_all]
        out_specs += [pl.BlockSpec((slab_in, 2 * D_FF), lambda i: (i, 0)),
                      pl.BlockSpec((slab_out, D_MODEL), lambda i: (i, 0))]
        out_shape += [jax.ShapeDtypeStruct((D_MODEL, 2 * D_FF), BF16), jax.ShapeDtypeStruct((D_FF, D_MODEL), BF16)]
    outs = pl.pallas_call(
        functools.partial(_ffn_kernel, mi=mi, n_chunks=n_chunks, after_mixer=mixer_out is not None,
                          cast_next=cast_next is not None),
        grid=(steps,),
        in_specs=in_specs,
        out_specs=out_specs,
        out_shape=out_shape,
        compiler_params=_cparams("arbitrary"),
        name="ffn" + ("_mix" if mixer_out is not None else "") + ("_cast" if cast_next is not None else ""),
    )(*args)
    return outs[0], tuple(outs[1:])


def _proj_kernel(x_ref, mod_ref, w_ref, *rest, seq_len, tn):
    if seq_len:
        cw_ref, cb_ref, o_ref, h_ref = rest
    else:
        o_ref, h_ref = rest

    @pl.when(pl.program_id(1) == 0)
    def _():
        sh = mod_ref[3:4, :]
        sc = mod_ref[4:5, :]
        h_ref[...] = (x_ref[...] * (1.0 + sc) + sh).astype(BF16)

    tm = h_ref.shape[0]
    if seq_len:
        pos = lax.broadcasted_iota(jnp.int32, (tm, 1), 0) % seq_len
        first, last = pos == 0, pos == seq_len - 1
    for cc in range(D_MODEL // tn):
        cols = slice(cc * tn, (cc + 1) * tn)
        u = _dot(h_ref[...], w_ref[:, cols])
        if seq_len:
            prev = jnp.where(first, 0.0, pltpu.roll(u, 1, 0))
            nxt = jnp.where(last, 0.0, pltpu.roll(u, tm - 1, 0))
            u = prev * cw_ref[0:1, cols] + u * cw_ref[1:2, cols] + nxt * cw_ref[2:3, cols] + cb_ref[:, cols]
        o_ref[:, cols] = u.astype(o_ref.dtype)


def _proj(x, mods, crow, w, layer, widx, *, tm, tn, out_dtype, conv=None, seq_len=0):
    m = x.shape[0]
    in_specs = [
        pl.BlockSpec((tm, D_MODEL), lambda i, g: (i, 0)),
        pl.BlockSpec((None, None, N_MOD, D_MODEL), lambda i, g: (layer, crow(i * tm), 0, 0)),
        pl.BlockSpec((None, D_MODEL, D_MODEL), lambda i, g: (widx, 0, g)),
    ]
    args = [x, mods, w]
    if conv is not None:
        cw, cb = conv
        in_specs += [
            pl.BlockSpec((None, 3, D_MODEL), lambda i, g: (widx, 0, g)),
            pl.BlockSpec((None, 1, D_MODEL), lambda i, g: (widx, 0, g)),
        ]
        args += [cw, cb.reshape(cb.shape[0], 1, cb.shape[1])]
    return pl.pallas_call(
        functools.partial(_proj_kernel, seq_len=seq_len, tn=tn),
        grid=(m // tm, 3),
        in_specs=in_specs,
        out_specs=pl.BlockSpec((None, tm, D_MODEL), lambda i, g: (g, i, 0)),
        out_shape=jax.ShapeDtypeStruct((3, m, D_MODEL), out_dtype),
        scratch_shapes=[pltpu.VMEM((tm, D_MODEL), BF16)],
        compiler_params=_cparams("arbitrary", "arbitrary"),
        name="proj_conv" if seq_len else "proj",
    )(*args)


def _lane_tiles(a):
    return [a[:, i:i + LANES] for i in range(0, a.shape[1], LANES)]


def _head_masks():
    lane = lax.broadcasted_iota(jnp.int32, (1, LANES), 1)
    return lane < HEAD_DIM, lane >= HEAD_DIM


def _attn_ctx_kernel(x_ref, mod_ref, wqkv_ref, wo_ref, g_ref, b_ref, o_ref, ko_ref, vo_ref,
                     q_scr, k_scr, v_scr, a_scr, *, seq, nseq):
    x = x_ref[...]
    h = (x * (1.0 + mod_ref[4:5, :]) + mod_ref[3:4, :]).astype(BF16)
    q_scr[...] = (_dot(h, wqkv_ref[:, :D_MODEL]) * HEAD_DIM ** -0.5).astype(BF16)
    k_scr[...] = _dot(h, wqkv_ref[:, D_MODEL:2 * D_MODEL])
    v_scr[...] = _dot(h, wqkv_ref[:, 2 * D_MODEL:])
    masks = _head_masks()
    for s in range(nseq):
        rows = slice(s * seq, (s + 1) * seq)
        ko_ref[s] = k_scr[rows, :].reshape(seq, N_HEADS, HEAD_DIM)
        vo_ref[s] = v_scr[rows, :].reshape(seq, N_HEADS, HEAD_DIM)
        for p in range(D_MODEL // LANES):
            sl = slice(p * LANES, (p + 1) * LANES)
            q2 = q_scr[rows, sl]
            k2 = k_scr[rows, sl]
            v2 = v_scr[rows, sl]
            o2 = jnp.zeros(q2.shape, F32)
            for hm in masks:
                kh = jnp.where(hm, k2, 0.0).astype(BF16)
                vh = jnp.where(hm, v2, 0.0).astype(BF16)
                sc = _dot_nt(q2, kh)
                e = jnp.exp(sc - jnp.max(sc, axis=-1, keepdims=True))
                inv = 1.0 / jnp.sum(e, axis=-1, keepdims=True)
                o2 = o2 + _dot(e.astype(BF16), vh) * inv
            a_scr[rows, sl] = o2.astype(BF16)
    y = ALPHA * x + mod_ref[5:6, :] * _dot(a_scr[...], wo_ref[...])
    o_ref[...] = _layer_norm(y, g_ref[...], b_ref[...])


def _attn_ctx(x, mods, crow, w_qkv, w_o, widx, ln_g, ln_b, layer, batch, seq, nseq=2):
    rows = nseq * seq
    cache_spec = pl.BlockSpec((nseq, None, seq, N_HEADS, HEAD_DIM), lambda i: (i, 0, 0, 0, 0))
    cache_shape = jax.ShapeDtypeStruct((batch, 1, seq, N_HEADS, HEAD_DIM), F32)
    return pl.pallas_call(
        functools.partial(_attn_ctx_kernel, seq=seq, nseq=nseq),
        grid=(batch // nseq,),
        in_specs=[
            pl.BlockSpec((rows, D_MODEL), lambda i: (i, 0)),
            pl.BlockSpec((None, None, N_MOD, D_MODEL), lambda i: (layer, crow(i * rows), 0, 0)),
            _resident((None, D_MODEL, 3 * D_MODEL), lambda i: (widx, 0, 0)),
            _resident((None, D_MODEL, D_MODEL), lambda i: (widx, 0, 0)),
            pl.BlockSpec((None, None, 1, D_MODEL), lambda i: (layer, 1, 0, 0)),
            pl.BlockSpec((None, None, 1, D_MODEL), lambda i: (layer, 1, 0, 0)),
        ],
        out_specs=[pl.BlockSpec((rows, D_MODEL), lambda i: (i, 0)), cache_spec, cache_spec],
        out_shape=[jax.ShapeDtypeStruct((batch * seq, D_MODEL), F32), cache_shape, cache_shape],
        scratch_shapes=[pltpu.VMEM((rows, D_MODEL), BF16), pltpu.VMEM((rows, D_MODEL), F32),
                        pltpu.VMEM((rows, D_MODEL), F32), pltpu.VMEM((rows, D_MODEL), BF16)],
        compiler_params=_cparams("arbitrary"),
        name="attn_ctx",
    )(x, mods, w_qkv, w_o, ln_g, ln_b)


def _nbr_geometry(rows):
    kr = min(WIN_ROWS, rows)
    blocks = []
    for blk in range(rows // Q_ROWS):
        r0 = blk * Q_ROWS
        lo = min(max(r0 - kr // 2, 0), rows - kr)
        key_start = min(lo, rows - K_ROWS)
        bands = []
        for r in range(r0, r0 + Q_ROWS):
            rs = min(max(r - kr // 2, 0), rows - kr)
            a = rs - key_start
            band_start = min(a - a % 2, K_ROWS - BAND_ROWS)
            off = a - band_start
            assert 0 <= band_start and 0 <= off and off + kr <= BAND_ROWS
            bands.append((band_start, off, rs - r + WIN_ROWS - 1))
        blocks.append((key_start, bands))
    return kr, blocks


def _nbr_bias_rows(rpb):
    pad = GRID_W - (2 * WIN_COLS - 1)
    row = jnp.pad(rpb, ((0, 0), (0, 0), (0, pad)))
    return jnp.concatenate([row[:, :-1], row[:, 1:]], axis=-1)


def _attn_nbr_kernel(q_ref, k_ref, v_ref, kc_ref, vc_ref, rp_ref, o_ref,
                     s_ref, sc_ref, p_ref, pc_ref, inv_ref, tab_ref, *, blocks, kr):
    masks = _head_masks()
    left = masks[0]
    qn = Q_ROWS * GRID_W
    kn = K_ROWS * GRID_W
    qcol = lax.broadcasted_iota(jnp.int32, (GRID_W, LANES), 0)
    kcol = lax.broadcasted_iota(jnp.int32, (GRID_W, LANES), 1) % GRID_W
    cstart = jnp.clip(qcol - WIN_COLS // 2, 0, GRID_W - WIN_COLS)
    col_ok = (kcol >= cstart) & (kcol < cstart + WIN_COLS)
    for hh in range(2):
        for d in range(2 * WIN_ROWS - 2):
            row = jnp.broadcast_to(rp_ref[hh, d:d + 1, :], (GRID_W, LANES))
            toep = pltpu.roll(row, LANES - (WIN_COLS - 1), 1, stride=1, stride_axis=0)
            tab_ref[hh, d] = jnp.where(col_ok, toep, NEG)
    for blk, (key_start, bands) in enumerate(blocks):
        q2 = (q_ref[blk * qn:(blk + 1) * qn, :] * HEAD_DIM ** -0.5).astype(BF16)
        k2 = k_ref[key_start * GRID_W:key_start * GRID_W + kn, :]
        v2 = v_ref[key_start * GRID_W:key_start * GRID_W + kn, :]
        o2 = jnp.zeros(q2.shape, F32)
        p_ref[...] = jnp.zeros_like(p_ref)
        for hh, hm in enumerate(masks):
            kh = jnp.where(hm, k2, 0.0).astype(BF16)
            vh = jnp.where(hm, v2, 0.0).astype(BF16)
            kch = jnp.where(hm, kc_ref[...], 0.0).astype(BF16)
            vch = jnp.where(hm, vc_ref[...], 0.0).astype(BF16)
            s_ref[...] = _dot_nt(q2, kh)
            sc_ref[...] = _dot_nt(q2, kch)
            for rl, (band_start, off, drow) in enumerate(bands):
                rsl = slice(rl * GRID_W, (rl + 1) * GRID_W)
                tiles = []
                for j in range(0, BAND_ROWS, 2):
                    ok0, ok1 = off <= j < off + kr, off <= j + 1 < off + kr
                    if not (ok0 or ok1):
                        continue
                    bias = tab_ref[hh, drow - off + j]
                    if not ok0:
                        bias = jnp.where(left, NEG, bias)
                    if not ok1:
                        bias = jnp.where(left, bias, NEG)
                    csl = slice((band_start + j) * GRID_W, (band_start + j + 2) * GRID_W)
                    tiles.append((csl, s_ref[rsl, csl] + bias))
                sc = sc_ref[rsl, :]
                top = functools.reduce(jnp.maximum, _lane_tiles(sc) + [sb for _, sb in tiles])
                mx = jnp.max(top, axis=-1, keepdims=True)
                ec = jnp.exp(sc - mx)
                tot = functools.reduce(jnp.add, _lane_tiles(ec))
                for csl, sb in tiles:
                    eb = jnp.exp(sb - mx)
                    tot = tot + eb
                    p_ref[rsl, csl] = eb.astype(BF16)
                inv_ref[rsl, :] = 1.0 / jnp.sum(tot, axis=-1, keepdims=True)
                pc_ref[rsl, :] = ec.astype(BF16)
            o2 = o2 + (_dot(p_ref[...], vh) + _dot(pc_ref[...], vch)) * inv_ref[...]
        o_ref[blk * qn:(blk + 1) * qn, :] = o2.astype(o_ref.dtype)


def _attn_nbr(qkv, kc, vc, rpb, batch, seq):
    rows = seq // GRID_W
    kr, blocks = _nbr_geometry(rows)
    n_pairs = D_MODEL // LANES
    n_d = 2 * WIN_ROWS - 2
    rp = _nbr_bias_rows(rpb).reshape(n_pairs, 2, n_d, LANES)
    s_ctx = kc.shape[1]
    qspec = lambda g: pl.BlockSpec((None, seq, LANES), lambda b, p: (g, b, p))
    cspec = pl.BlockSpec((None, s_ctx, LANES), lambda b, p: (b, 0, p))
    qn, kn = Q_ROWS * GRID_W, K_ROWS * GRID_W
    return pl.pallas_call(
        functools.partial(_attn_nbr_kernel, blocks=blocks, kr=kr),
        grid=(batch, n_pairs),
        in_specs=[qspec(0), qspec(1), qspec(2), cspec, cspec,
                  pl.BlockSpec((None, 2, n_d, LANES), lambda b, p: (p, 0, 0, 0))],
        out_specs=pl.BlockSpec((seq, LANES), lambda b, p: (b, p)),
        out_shape=jax.ShapeDtypeStruct((batch * seq, D_MODEL), BF16),
        scratch_shapes=[pltpu.VMEM((qn, kn), F32), pltpu.VMEM((qn, s_ctx), F32),
                        pltpu.VMEM((qn, kn), BF16), pltpu.VMEM((qn, s_ctx), BF16),
                        pltpu.VMEM((qn, 1), F32), pltpu.VMEM((2, n_d, GRID_W, LANES), F32)],
        compiler_params=_cparams("arbitrary", "arbitrary"),
        name="attn_nbr",
    )(qkv, qkv, qkv, kc, vc, rp)


def _dft_tables(seq_len, tq, tr):
    n = 2 * seq_len
    f = np.arange(seq_len, dtype=np.float64)[:, None] + 0.5
    s = np.arange(seq_len, dtype=np.float64)[None, :]
    ang = 2.0 * np.pi * f * s / n
    re = np.cos(ang).reshape(seq_len // tq, tq, seq_len)
    im = (-np.sin(ang)).reshape(seq_len // tq, tq, seq_len)
    fwd = np.concatenate([re, im], axis=1).astype(np.float32)
    bwd = fwd.reshape(2 * seq_len, seq_len).T
    return fwd, np.ascontiguousarray(bwd).reshape(seq_len // tr, tr, 2 * seq_len)


def _split_spectrum_table(seq_len, tq):
    half = seq_len // 2
    f = np.arange(half, dtype=np.float64)[:, None] + 0.5
    s = np.arange(seq_len, dtype=np.float64)[None, :]
    lo = np.pi * f * s / seq_len
    hi = np.pi * (seq_len - f) * s / seq_len
    parts = [np.cos(lo), -np.sin(lo), np.cos(hi), np.sin(hi)]
    return np.concatenate([p.reshape(half // tq, tq, seq_len) for p in parts], axis=1).astype(np.float32)


def _twiddles(seq_len):
    th = np.pi * (np.arange(seq_len // 2, dtype=np.float64) + 0.5) / seq_len
    return np.stack([np.cos(th), np.sin(th)])[:, :, None].repeat(LANES, axis=2).astype(np.float32)


def _deinterleave_matrix():
    sel = np.zeros((SPLIT_ROWS, SPLIT_ROWS), np.float32)
    i = np.arange(SPLIT_ROWS // 2)
    sel[i, 2 * i] = 1.0
    sel[SPLIT_ROWS // 2 + i, 2 * i + 1] = 1.0
    return sel


def _filter_features(seq_len):
    pos = np.arange(seq_len, dtype=np.float64)[:, None]
    t = pos / max(seq_len - 1, 1)
    bands = np.linspace(1e-4, HY_BANDS - 1, HY_BANDS)
    ang = 2.0 * np.pi * pos / seq_len * bands
    z = np.concatenate([t, np.cos(ang), -np.sin(ang)], axis=-1)
    zp = np.zeros((seq_len, HY_FILTER_W), np.float64)
    zp[:, :HY_EMB] = z
    deltas = np.abs(np.linspace(math.log(HY_TARGET) / HY_SLOW_DECAY, math.log(HY_TARGET) / HY_FAST_DECAY, D_MODEL))
    return zp.astype(np.float32), t.astype(np.float32), deltas.astype(np.float32)[None, :]


def _filter_kernel(z_ref, t_ref, dl_ref, w1_ref, b1_ref, f1_ref, w2_ref, b2_ref, f2_ref, w3_ref, b3_ref, o_ref,
                   hid_ref, decay_ref):
    hp = lax.Precision.HIGHEST

    @pl.when(pl.program_id(0) == 0)
    def _():
        hid = jnp.sin(f1_ref[...] * (jnp.dot(z_ref[...], w1_ref[...], precision=hp, preferred_element_type=F32) + b1_ref[...]))
        hid_ref[...] = jnp.sin(f2_ref[...] * (jnp.dot(hid, w2_ref[...], precision=hp, preferred_element_type=F32) + b2_ref[...]))
        decay_ref[...] = jnp.exp(-t_ref[...] * dl_ref[...])

    hid, w3 = hid_ref[...], w3_ref[...]
    hid_hi, w3_hi = hid.astype(BF16), w3.astype(BF16)
    hid_lo = (hid - hid_hi.astype(F32)).astype(BF16)
    w3_lo = (w3 - w3_hi.astype(F32)).astype(BF16)
    filt = _dot(hid_hi, w3_hi) + (_dot(hid_lo, w3_hi) + _dot(hid_hi, w3_lo)) + b3_ref[...]
    decay = decay_ref[...]
    hf = filt[:, :D_MODEL] * decay
    hb = filt[:, D_MODEL:] * decay
    row = lax.broadcasted_iota(jnp.int32, (hb.shape[0], 1), 0)
    hb = jnp.where(row == 0, 0.0, hb)
    o_ref[:, :D_MODEL] = (hf + hb).astype(o_ref.dtype)
    o_ref[:, D_MODEL:] = (hf - hb).astype(o_ref.dtype)


def _filters(seq_len, w1, b1, f1, w2, b2, f2, w3, b3):
    z, t, deltas = _filter_features(seq_len)
    w1p = jnp.zeros((HY_FILTER_W, HY_FILTER_W), F32).at[:HY_EMB].set(w1)
    full = lambda shape: pl.BlockSpec(shape, lambda o: (0,) * len(shape))
    row = lambda a: a.reshape(1, -1)
    return pl.pallas_call(
        _filter_kernel,
        grid=(2,),
        in_specs=[full((seq_len, HY_FILTER_W)), full((seq_len, 1)), full((1, D_MODEL)),
                  full((HY_FILTER_W, HY_FILTER_W)), full((1, HY_FILTER_W)), full((1, HY_FILTER_W)),
                  full((HY_FILTER_W, HY_FILTER_W)), full((1, HY_FILTER_W)), full((1, HY_FILTER_W)),
                  pl.BlockSpec((HY_FILTER_W, 2 * D_MODEL), lambda o: (0, o)),
                  pl.BlockSpec((1, 2 * D_MODEL), lambda o: (0, o))],
        out_specs=pl.BlockSpec((None, seq_len, 2 * D_MODEL), lambda o: (o, 0, 0)),
        out_shape=jax.ShapeDtypeStruct((2, seq_len, 2 * D_MODEL), BF16),
        scratch_shapes=[pltpu.VMEM((seq_len, HY_FILTER_W), F32), pltpu.VMEM((seq_len, D_MODEL), F32)],
        compiler_params=_cparams("arbitrary"),
        name="hyena_filters",
    )(jnp.asarray(z),jnp.asarray(t), jnp.asarray(deltas), w1p, row(b1), row(f1), w2, row(b2), row(f2), w3, row(b3))


def _spectrum_kernel(g_ref, hs_ref, hd_ref, o_ref, *, tq, scale):
    for part in range(g_ref.shape[0] // tq):
        rows = slice(part * tq, (part + 1) * tq)
        h_ref = hs_ref if part % 2 == 0 else hd_ref
        o_ref[rows, :] = scale * _dot(g_ref[rows, :], h_ref[...])


def _spectrum(table, filt, seq_len, tq, tc):
    nt, rows, _ = table.shape
    ncb = D_MODEL // tc
    return pl.pallas_call(
        functools.partial(_spectrum_kernel, tq=tq, scale=1.0 / seq_len),
        grid=(2, ncb, nt),
        in_specs=[pl.BlockSpec((None, rows, seq_len), lambda o, c, f: (f, 0, 0)),
                  pl.BlockSpec((None, seq_len, tc), lambda o, c, f: (o, 0, c)),
                  pl.BlockSpec((None, seq_len, tc), lambda o, c, f: (o, 0, c + ncb))],
        out_specs=pl.BlockSpec((None, None, rows, tc), lambda o, c, f: (o, f, 0, c)),
        out_shape=jax.ShapeDtypeStruct((2, nt, rows, D_MODEL), F32),
        compiler_params=_cparams("arbitrary", "arbitrary", "arbitrary"),
        name="hyena_spectrum",
    )(table, filt, filt)


def _hyena_kernel(v_ref, gate_ref, g_ref, gt_ref, k_ref, tw_ref, sel_ref, selt_ref, b_ref, o_ref,
                  ze_ref, zo_ref, pu_ref, pv_ref, *, tq, nf, tr):
    o = pl.program_id(2)
    st = pl.program_id(3)
    half = ze_ref.shape[0]
    tc = ze_ref.shape[1]
    hs = SPLIT_ROWS // 2

    @pl.when((o == 0) & (st == 0))
    def _():
        for j in range(half // hs):
            sp = _dot(sel_ref[...], v_ref[j * SPLIT_ROWS:(j + 1) * SPLIT_ROWS, :])
            ze_ref[j * hs:(j + 1) * hs, :] = sp[:hs].astype(BF16)
            zo_ref[j * hs:(j + 1) * hs, :] = sp[hs:].astype(BF16)

    @pl.when(st < nf)
    def _():
        base = pl.multiple_of(st * 2 * tq, 2 * tq)
        c, s = tw_ref[0], tw_ref[1]
        group = 2 * LANES
        for lt in range(tc // LANES):
            cols = slice(lt * LANES, (lt + 1) * LANES)
            if lt % 2 == 0:
                gcols = slice(lt * LANES, lt * LANES + group)
                fa = _dot(g_ref[...], ze_ref[:, gcols])
                fb = _dot(g_ref[...], zo_ref[:, gcols])
            loc = slice((lt % 2) * LANES, (lt % 2 + 1) * LANES)
            ar, ai, br, bi = fa[:tq, loc], fa[tq:, loc], fb[:tq, loc], fb[tq:, loc]
            tr_, ti = c * br + s * bi, c * bi - s * br
            s1r, s1i, s2r, s2i = ar + tr_, ai + ti, ar - tr_, ai - ti
            k1r, k1i = k_ref[0:tq, cols], k_ref[tq:2 * tq, cols]
            k2r, k2i = k_ref[2 * tq:3 * tq, cols], k_ref[3 * tq:, cols]
            p1r, p1i = s1r * k1r - s1i * k1i, s1r * k1i + s1i * k1r
            p2r, p2i = s2r * k2r - s2i * k2i, s2r * k2i + s2i * k2r
            wr, wi = p1r - p2r, p1i - p2i
            pu_ref[pl.ds(base, tq), cols] = (p1r + p2r).astype(BF16)
            pu_ref[pl.ds(base + tq, tq), cols] = (p1i + p2i).astype(BF16)
            pv_ref[pl.ds(base, tq), cols] = (c * wr - s * wi).astype(BF16)
            pv_ref[pl.ds(base + tq, tq), cols] = (c * wi + s * wr).astype(BF16)

    @pl.when(st >= nf)
    def _():
        r0 = pl.multiple_of((st - nf) * tr, tr)
        rows = pl.ds(r0, tr)
        ye = _dot(gt_ref[...], pu_ref[...]) + ze_ref[rows, :].astype(F32) * b_ref[...]
        yo = _dot(gt_ref[...], pv_ref[...]) + zo_ref[rows, :].astype(F32) * b_ref[...]
        block = lambda j: pl.ds(pl.multiple_of(2 * r0 + j * SPLIT_ROWS, SPLIT_ROWS), SPLIT_ROWS)
        gates = [_dot(sel_ref[...], gate_ref[block(j), :]) for j in range(tr // hs)]
        ge = ye * jnp.concatenate([g[:hs] for g in gates], axis=0)
        go = yo * jnp.concatenate([g[hs:] for g in gates], axis=0)

        @pl.when(o == 0)
        def _():
            ze_ref[rows, :] = ge.astype(BF16)
            zo_ref[rows, :] = go.astype(BF16)

        @pl.when(o == 1)
        def _():
            for j in range(tr // hs):
                blk = jnp.concatenate([ge[j * hs:(j + 1) * hs], go[j * hs:(j + 1) * hs]], axis=0).astype(BF16)
                o_ref[block(j), :] = _dot(selt_ref[...], blk).astype(o_ref.dtype)


def _hyena_core(u, fwd_h, bwd_h, spec, tw, sel, bias, batch, seq_len, tq, tc):
    half = seq_len // 2
    nf = half // tq
    nr, tr, _ = bwd_h.shape
    ncb = D_MODEL // tc
    fidx = lambda st: jnp.minimum(st, nf - 1)
    whole = lambda shape: pl.BlockSpec(shape, lambda b, c, o, st: (0,) * len(shape))
    return pl.pallas_call(
        functools.partial(_hyena_kernel, tq=tq, nf=nf, tr=tr),
        grid=(batch, ncb, 2, nf + nr),
        in_specs=[pl.BlockSpec((None, seq_len, tc), lambda b, c, o, st: (0, b, c)),
                  pl.BlockSpec((None, seq_len, tc), lambda b, c, o, st: (1 + o, b, c)),
                  pl.BlockSpec((None, 2 * tq, half), lambda b, c, o, st: (fidx(st), 0, 0)),
                  pl.BlockSpec((None, tr, seq_len), lambda b, c, o, st: (jnp.maximum(st - nf, 0), 0, 0)),
                  pl.BlockSpec((None, None, 4 * tq, tc), lambda b, c, o, st: (o, fidx(st), 0, c)),
                  pl.BlockSpec((2, tq, LANES), lambda b, c, o, st: (0, fidx(st), 0)),
                  whole((SPLIT_ROWS, SPLIT_ROWS)), whole((SPLIT_ROWS, SPLIT_ROWS)),
                  pl.BlockSpec((None, 1, tc), lambda b, c, o, st: (o, 0, c))],
        out_specs=pl.BlockSpec((seq_len, tc), lambda b, c, o, st: (b, c)),
        out_shape=jax.ShapeDtypeStruct((batch * seq_len, D_MODEL), BF16),
        scratch_shapes=[pltpu.VMEM((half, tc), BF16), pltpu.VMEM((half, tc), BF16),
                        pltpu.VMEM((seq_len, tc), BF16), pltpu.VMEM((seq_len, tc), BF16)],
        compiler_params=_cparams("arbitrary", "arbitrary", "arbitrary", "arbitrary"),
        name="hyena_conv",
    )(u, u, fwd_h, bwd_h, spec, tw, sel, sel.T, bias.reshape(2, 1, D_MODEL))


def _hyena_layer_kernel(x_ref, mod_ref, win_ref, cw_ref, cb_ref, g_ref, gt_ref, k_ref, hb_ref, wout_ref,
                        lg_ref, lb_ref, o_ref, u_scr, *, seq_len, nseq, tn):
    x = x_ref[...]
    h = (x * (1.0 + mod_ref[4:5, :]) + mod_ref[3:4, :]).astype(BF16)
    rows = nseq * seq_len
    pos = lax.broadcasted_iota(jnp.int32, (rows, 1), 0) % seq_len
    first, last = pos == 0, pos == seq_len - 1
    for g in range(3):
        for cc in range(D_MODEL // tn):
            cols = slice(g * D_MODEL + cc * tn, g * D_MODEL + (cc + 1) * tn)
            u = _dot(h, win_ref[:, cols])
            prev = jnp.where(first, 0.0, pltpu.roll(u, 1, 0))
            nxt = jnp.where(last, 0.0, pltpu.roll(u, rows - 1, 0))
            u = prev * cw_ref[0:1, cols] + u * cw_ref[1:2, cols] + nxt * cw_ref[2:3, cols] + cb_ref[:, cols]
            u_scr[g, :, cc * tn:(cc + 1) * tn] = u.astype(BF16)
    for o in range(2):
        kr, ki = k_ref[o, :seq_len, :], k_ref[o, seq_len:, :]
        for s in range(nseq):
            r = slice(s * seq_len, (s + 1) * seq_len)
            z = u_scr[0, r, :]
            zf = _dot(g_ref[...], z)
            zr, zi = zf[:seq_len], zf[seq_len:]
            prod = jnp.concatenate([zr * kr - zi * ki, zr * ki + zi * kr], axis=0).astype(BF16)
            y = _dot(gt_ref[...], prod) + z.astype(F32) * hb_ref[o]
            u_scr[0, r, :] = (y * u_scr[1 + o, r, :].astype(F32)).astype(BF16)
    y = ALPHA * x + mod_ref[5:6, :] * _dot(u_scr[0], wout_ref[...])
    o_ref[...] = _layer_norm(y, lg_ref[...], lb_ref[...])


def _hyena_layer(x, mods, crow, w_in, w_out, widx, conv_w, conv_b, fwd, bwd, spec, bias, ln_g, ln_b, layer,
                 seq_len, nseq, tn=512):
    m = x.shape[0]
    rows = nseq * seq_len
    assert fwd.shape[0] == 1
    return pl.pallas_call(
        functools.partial(_hyena_layer_kernel, seq_len=seq_len, nseq=nseq, tn=tn),
        grid=(m // rows,),
        in_specs=[
            pl.BlockSpec((rows, D_MODEL), lambda i: (i, 0)),
            pl.BlockSpec((None, None, N_MOD, D_MODEL), lambda i: (layer, crow(i * rows), 0, 0)),
            _resident((None, D_MODEL, 3 * D_MODEL), lambda i: (widx, 0, 0)),
            pl.BlockSpec((None, 3, 3 * D_MODEL), lambda i: (widx, 0, 0)),
            pl.BlockSpec((None, 1, 3 * D_MODEL), lambda i: (widx, 0, 0)),
            pl.BlockSpec((None, 2 * seq_len, seq_len), lambda i: (0, 0, 0)),
            pl.BlockSpec((None, seq_len, 2 * seq_len), lambda i: (0, 0, 0)),
            _resident((2, None, 2 * seq_len, D_MODEL), lambda i: (0, 0, 0, 0)),
            pl.BlockSpec((2, 1, D_MODEL), lambda i: (0, 0, 0)),
            _resident((None, D_MODEL, D_MODEL), lambda i: (widx, 0, 0)),
            pl.BlockSpec((None, None, 1, D_MODEL), lambda i: (layer, 1, 0, 0)),
            pl.BlockSpec((None, None, 1, D_MODEL), lambda i: (layer, 1, 0, 0)),
        ],
        out_specs=pl.BlockSpec((rows, D_MODEL), lambda i: (i, 0)),
        out_shape=jax.ShapeDtypeStruct((m, D_MODEL), F32),
        scratch_shapes=[pltpu.VMEM((3, rows, D_MODEL), BF16)],
        compiler_params=_cparams("arbitrary"),
        name="hyena_layer",
    )(x, mods, w_in, conv_w, conv_b.reshape(conv_b.shape[0], 1, conv_b.shape[1]), fwd, bwd, spec,
      bias.reshape(2, 1, D_MODEL), w_out, ln_g, ln_b)


def _hyena_tiles(seq_len, batch):
    tq = tr = min(seq_len, 512)
    tc = D_MODEL if seq_len <= 256 else 512
    nseq = math.gcd(batch, max(1, 1024 // seq_len))
    return tq, tr, tc, nseq


def kernel(x_prompt, x_sample, cache_k, cache_v, c, c_ctx, mod_w, mod_b, ln_g, ln_b, ffn_w_in, ffn_w_out,
           attn_w_qkv, attn_w_o, attn_rpb, hy_w_in, hy_conv_w, hy_conv_b, hy_f_w1, hy_f_b1, hy_f_freq1,
           hy_f_w2, hy_f_b2, hy_f_freq2, hy_f_w3, hy_f_b3, hy_bias, hy_w_out):
    batch, seq, _ = x_prompt.shape
    dec_batch, dec_seq, _ = x_sample.shape
    assert dec_batch + 1 <= COND_ROWS

    conds = jnp.zeros((COND_ROWS, D_MODEL), F32).at[0].set(c_ctx).at[1:1 + dec_batch].set(c)
    mods = _mods(conds, mod_w, mod_b).reshape(DEPTH, COND_ROWS, N_MOD, D_MODEL)
    ln_g4 = ln_g.reshape(DEPTH, 3, 1, D_MODEL)
    ln_b4 = ln_b.reshape(DEPTH, 3, 1, D_MODEL)

    streams = {
        "ctx": dict(x=x_prompt.reshape(batch * seq, D_MODEL), batch=batch, seq=seq, crow=lambda t: 0),
        "lat": dict(x=x_sample.reshape(dec_batch * dec_seq, D_MODEL), batch=dec_batch, seq=dec_seq,
                    crow=lambda t: 1 + t // dec_seq),
    }
    as_bf16 = lambda table: jnp.asarray(table).astype(BF16)
    hyena_consts = {}
    for name, st in streams.items():
        sl = st["seq"]
        tq, tr, tc, nseq = _hyena_tiles(sl, st["batch"])
        if sl == tq:
            fwd, bwd = _dft_tables(sl, tq, tr)
            hyena_consts[name] = dict(tq=tq, tc=tc, nseq=nseq, fwd=as_bf16(fwd), bwd=as_bf16(bwd), ktab=as_bf16(fwd))
        else:
            fwd, bwd = _dft_tables(sl // 2, tq, tr)
            hyena_consts[name] = dict(tq=tq, tc=tc, nseq=nseq, fwd=as_bf16(fwd), bwd=as_bf16(bwd),
                                      ktab=as_bf16(_split_spectrum_table(sl, tq)),
                                      tw=jnp.asarray(_twiddles(sl)), sel=as_bf16(_deinterleave_matrix()))

    attn_w_qkv, attn_w_o, hy_w_in, hy_w_out = (w.astype(BF16) for w in (attn_w_qkv, attn_w_o, hy_w_in, hy_w_out))
    ffn_order = [(layer, which) for layer in range(DEPTH) for which in (0, 1)]
    ffn_bf16 = {}

    def ffn(x, crow, layer, which, mixer_out=None):
        key = (layer, which)
        if key not in ffn_bf16:
            ffn_bf16[key] = (ffn_w_in[layer, which].astype(BF16), ffn_w_out[layer, which].astype(BF16))
        later = ffn_order[ffn_order.index(key) + 1:]
        cast_next = None
        if later and later[0] not in ffn_bf16 and _ffn_can_cast(x.shape[0]):
            cast_next = (ffn_w_in, ffn_w_out, *later[0])
        y, cast = _ffn(x, mods, crow, *ffn_bf16[key], ln_g4, ln_b4, layer, which, mixer_out=mixer_out,
                       cast_next=cast_next)
        if cast_next is not None:
            ffn_bf16[later[0]] = cast
        return y

    new_k, new_v = [], []
    outs = {}
    for name, st in streams.items():
        x, nb, sl, crow = st["x"], st["batch"], st["seq"], st["crow"]
        for layer in range(DEPTH):
            j = layer // 2
            x = ffn(x, crow, layer, 0)
            mixer_out = None
            if layer % 2 == 0:
                if name == "ctx":
                    x, k_new, v_new = _attn_ctx(x, mods, crow, attn_w_qkv, attn_w_o, j, ln_g4, ln_b4, layer, nb, sl)
                    new_k.append(k_new)
                    new_v.append(v_new)
                else:
                    qkv = _proj(x, mods, crow, attn_w_qkv, layer, j, tm=1024, tn=1024, out_dtype=BF16)
                    kc = cache_k[:, j].reshape(nb, -1, D_MODEL)
                    vc = cache_v[:, j].reshape(nb, -1, D_MODEL)
                    mixer_out = (_attn_nbr(qkv, kc, vc, attn_rpb[j], nb, sl), attn_w_o, j)
            else:
                hc = hyena_consts[name]
                tq, tc = hc["tq"], hc["tc"]
                filt = _filters(sl, hy_f_w1[j], hy_f_b1[j], hy_f_freq1[j], hy_f_w2[j], hy_f_b2[j], hy_f_freq2[j],
                                hy_f_w3[j], hy_f_b3[j])
                spec = _spectrum(hc["ktab"], filt, sl, tq, tc)
                if sl == tq:
                    x = _hyena_layer(x, mods, crow, hy_w_in, hy_w_out, j, hy_conv_w, hy_conv_b, hc["fwd"], hc["bwd"],
                                     spec, hy_bias[j], ln_g4, ln_b4, layer, sl, hc["nseq"])
                else:
                    u = _proj(x, mods, crow, hy_w_in, layer, j, tm=max(sl, 1024), tn=512, out_dtype=BF16,
                              conv=(hy_conv_w, hy_conv_b), seq_len=sl)
                    a = _hyena_core(u, hc["fwd"], hc["bwd"], spec, hc["tw"], hc["sel"], hy_bias[j], nb, sl, tq, tc)
                    mixer_out = (a, hy_w_out, j)
            x = ffn(x, crow, layer, 1, mixer_out=mixer_out)
        outs[name] = x

    y_prompt = outs["ctx"].reshape(batch, seq, D_MODEL)
    y_sample = outs["lat"].reshape(dec_batch, dec_seq, D_MODEL)
    return (y_prompt, y_sample, jnp.concatenate(new_k, axis=1), jnp.concatenate(new_v, axis=1))
```

```python
import functools
import math

import numpy as np
import jax
import jax.numpy as jnp
from jax import lax
from jax.experimental import pallas as pl
from jax.experimental.pallas import tpu as pltpu

D_MODEL = 1024
DEPTH = 2
GRID_W = 64
N_HEADS = 16
HEAD_DIM = D_MODEL // N_HEADS
WIN_ROWS = 8
WIN_COLS = 16
D_FF = 2816
N_MOD = 9
HY_BANDS = 16
HY_EMB = 1 + 2 * HY_BANDS
HY_FILTER_W = 64
HY_FAST_DECAY = 0.3
HY_SLOW_DECAY = 1.5
HY_TARGET = 1e-2
ALPHA = (2 * DEPTH) ** 0.25
LN_EPS = 1e-5

F32 = jnp.float32
BF16 = jnp.bfloat16
NEG = -1e30
LANES = 128
VMEM_LIMIT = 56 * 1024 * 1024
COND_ROWS = 8

Q_ROWS = 8
K_ROWS = 16
BAND_ROWS = 10
SPLIT_ROWS = 256


def _cparams(*sem):
    return pltpu.CompilerParams(dimension_semantics=sem, vmem_limit_bytes=VMEM_LIMIT)


def _dot(a, b):
    return jnp.dot(a, b, preferred_element_type=F32)


def _dot_nt(a, b):
    return lax.dot_general(a, b, (((1,), (1,)), ((), ())), preferred_element_type=F32)


def _layer_norm(y, g, b):
    mu = jnp.mean(y, axis=-1, keepdims=True)
    yc = y - mu
    var = jnp.mean(yc * yc, axis=-1, keepdims=True)
    return yc * lax.rsqrt(var + LN_EPS) * g + b


def _mods_kernel(c_ref, w_ref, b_ref, o_ref):
    c = c_ref[...]
    s = c * jax.nn.sigmoid(c)
    o_ref[...] = _dot(s.astype(BF16), w_ref[...].astype(BF16)) + b_ref[...]


def _mods(conds, mod_w, mod_b):
    nd = N_MOD * D_MODEL
    tn = nd // 4
    return pl.pallas_call(
        _mods_kernel,
        grid=(DEPTH, nd // tn),
        in_specs=[
            pl.BlockSpec((COND_ROWS, D_MODEL), lambda l, j: (0, 0)),
            pl.BlockSpec((None, D_MODEL, tn), lambda l, j: (l, 0, j)),
            pl.BlockSpec((None, 1, tn), lambda l, j: (l, 0, j)),
        ],
        out_specs=pl.BlockSpec((None, COND_ROWS, tn), lambda l, j: (l, 0, j)),
        out_shape=jax.ShapeDtypeStruct((DEPTH, COND_ROWS, nd), F32),
        compiler_params=_cparams("arbitrary", "arbitrary"),
        name="mods",
    )(conds, mod_w, mod_b.reshape(DEPTH, 1, nd))


def _ffn_kernel(x_ref, mod_ref, wi_ref, wo_ref, g_ref, b_ref, *rest, mi, n_chunks, after_mixer, cast_next):
    rest = list(rest)
    x = x_ref[...]
    if after_mixer:
        a_ref, wmix_ref, gmix_ref, bmix_ref = rest[:4]
        del rest[:4]
        x = _layer_norm(ALPHA * x + mod_ref[5:6, :] * _dot(a_ref[...], wmix_ref[...]), gmix_ref[...], bmix_ref[...])
    if cast_next:
        wi_next_ref, wo_next_ref, o_ref, wi_cast_ref, wo_cast_ref = rest
        wi_cast_ref[...] = wi_next_ref[...].astype(BF16)
        wo_cast_ref[...] = wo_next_ref[...].astype(BF16)
    else:
        o_ref, = rest
    sh = mod_ref[3 * mi:3 * mi + 1, :]
    sc = mod_ref[3 * mi + 1:3 * mi + 2, :]
    gt = mod_ref[3 * mi + 2:3 * mi + 3, :]
    h = (x * (1.0 + sc) + sh).astype(BF16)
    tf = D_FF // n_chunks
    f = None
    for c in range(n_chunks):
        gate = _dot(h, wi_ref[:, c * tf:(c + 1) * tf])
        up = _dot(h, wi_ref[:, D_FF + c * tf:D_FF + (c + 1) * tf])
        a = (gate * jax.nn.sigmoid(gate) * up).astype(BF16)
        part = _dot(a, wo_ref[c * tf:(c + 1) * tf, :])
        f = part if f is None else f + part
    y = ALPHA * x + (0.5 * gt) * f
    o_ref[...] = _layer_norm(y, g_ref[...], b_ref[...])


def _resident(shape, index_map):
    return pl.BlockSpec(shape, index_map, pipeline_mode=pl.Buffered(1))


BF16_ROWS = 16


def _ffn_can_cast(m, tm=1024):
    steps = m // tm
    return D_MODEL % (steps * BF16_ROWS) == 0 and D_FF % (steps * BF16_ROWS) == 0


def _ffn(x, mods, crow, w_in, w_out, ln_g, ln_b, layer, which, mixer_out=None, cast_next=None, tm=1024, n_chunks=11):
    m = x.shape[0]
    steps = m // tm
    mi = 0 if which == 0 else 2
    ln_spec = lambda idx: pl.BlockSpec((None, None, 1, D_MODEL), lambda i: (layer, idx, 0, 0))
    in_specs = [
        pl.BlockSpec((tm, D_MODEL), lambda i: (i, 0)),
        pl.BlockSpec((None, None, N_MOD, D_MODEL), lambda i: (layer, crow(i * tm), 0, 0)),
        _resident((D_MODEL, 2 * D_FF), lambda i: (0, 0)),
        _resident((D_FF, D_MODEL), lambda i: (0, 0)),
        ln_spec(mi), ln_spec(mi),
    ]
    args = [x, mods, w_in, w_out, ln_g, ln_b]
    out_specs = [pl.BlockSpec((tm, D_MODEL), lambda i: (i, 0))]
    out_shape = [jax.ShapeDtypeStruct((m, D_MODEL), F32)]
    if mixer_out is not None:
        a, w_mix, widx = mixer_out
        in_specs += [pl.BlockSpec((tm, D_MODEL), lambda i: (i, 0)),
                     _resident((None, D_MODEL, D_MODEL), lambda i: (widx, 0, 0)), ln_spec(1), ln_spec(1)]
        args += [a, w_mix, ln_g, ln_b]
    if cast_next is not None:
        w_in_all, w_out_all, layer2, which2 = cast_next
        slab_in, slab_out = D_MODEL // steps, D_FF // steps
        in_specs += [pl.BlockSpec((None, None, slab_in, 2 * D_FF), lambda i: (layer2, which2, i, 0)),
                     pl.BlockSpec((None, None, slab_out, D_MODEL), lambda i: (layer2, which2, i, 0))]
        args += [w_in_all, w_out_all]
        out_specs += [pl.BlockSpec((slab_in, 2 * D_FF), lambda i: (i, 0)),
                      pl.BlockSpec((slab_out, D_MODEL), lambda i: (i, 0))]
        out_shape += [jax.ShapeDtypeStruct((D_MODEL, 2 * D_FF), BF16), jax.ShapeDtypeStruct((D_FF, D_MODEL), BF16)]
    outs = pl.pallas_call(
        functools.partial(_ffn_kernel, mi=mi, n_chunks=n_chunks, after_mixer=mixer_out is not None,
                          cast_next=cast_next is not None),
        grid=(steps,),
        in_specs=in_specs,
        out_specs=out_specs,
        out_shape=out_shape,
        compiler_params=_cparams("arbitrary"),
        name="ffn" + ("_mix" if mixer_out is not None else "") + ("_cast" if cast_next is not None else ""),
    )(*args)
    return outs[0], tuple(outs[1:])


def _proj_kernel(x_ref, mod_ref, w_ref, *rest, seq_len, tn):
    if seq_len:
        cw_ref, cb_ref, o_ref, h_ref = rest
    else:
        o_ref, h_ref = rest

    @pl.when(pl.program_id(1) == 0)
    def _():
        sh = mod_ref[3:4, :]
        sc = mod_ref[4:5, :]
        h_ref[...] = (x_ref[...] * (1.0 + sc) + sh).astype(BF16)

    tm = h_ref.shape[0]
    if seq_len:
        pos = lax.broadcasted_iota(jnp.int32, (tm, 1), 0) % seq_len
        first, last = pos == 0, pos == seq_len - 1
    for cc in range(D_MODEL // tn):
        cols = slice(cc * tn, (cc + 1) * tn)
        u = _dot(h_ref[...], w_ref[:, cols])
        if seq_len:
            prev = jnp.where(first, 0.0, pltpu.roll(u, 1, 0))
            nxt = jnp.where(last, 0.0, pltpu.roll(u, tm - 1, 0))
            u = prev * cw_ref[0:1, cols] + u * cw_ref[1:2, cols] + nxt * cw_ref[2:3, cols] + cb_ref[:, cols]
        o_ref[:, cols] = u.astype(o_ref.dtype)


def _proj(x, mods, crow, w, layer, widx, *, tm, tn, out_dtype, conv=None, seq_len=0):
    m = x.shape[0]
    in_specs = [
        pl.BlockSpec((tm, D_MODEL), lambda i, g: (i, 0)),
        pl.BlockSpec((None, None, N_MOD, D_MODEL), lambda i, g: (layer, crow(i * tm), 0, 0)),
        pl.BlockSpec((None, D_MODEL, D_MODEL), lambda i, g: (widx, 0, g)),
    ]
    args = [x, mods, w]
    if conv is not None:
        cw, cb = conv
        in_specs += [
            pl.BlockSpec((None, 3, D_MODEL), lambda i, g: (widx, 0, g)),
            pl.BlockSpec((None, 1, D_MODEL), lambda i, g: (widx, 0, g)),
        ]
        args += [cw, cb.reshape(cb.shape[0], 1, cb.shape[1])]
    return pl.pallas_call(
        functools.partial(_proj_kernel, seq_len=seq_len, tn=tn),
        grid=(m // tm, 3),
        in_specs=in_specs,
        out_specs=pl.BlockSpec((None, tm, D_MODEL), lambda i, g: (g, i, 0)),
        out_shape=jax.ShapeDtypeStruct((3, m, D_MODEL), out_dtype),
        scratch_shapes=[pltpu.VMEM((tm, D_MODEL), BF16)],
        compiler_params=_cparams("arbitrary", "arbitrary"),
        name="proj_conv" if seq_len else "proj",
    )(*args)


def _lane_tiles(a):
    return [a[:, i:i + LANES] for i in range(0, a.shape[1], LANES)]


def _head_masks():
    lane = lax.broadcasted_iota(jnp.int32, (1, LANES), 1)
    return lane < HEAD_DIM, lane >= HEAD_DIM


def _attn_ctx_kernel(x_ref, mod_ref, wqkv_ref, wo_ref, g_ref, b_ref, o_ref, ko_ref, vo_ref,
                     q_scr, k_scr, v_scr, a_scr, *, seq, nseq):
    x = x_ref[...]
    h = (x * (1.0 + mod_ref[4:5, :]) + mod_ref[3:4, :]).astype(BF16)
    q_scr[...] = (_dot(h, wqkv_ref[:, :D_MODEL]) * HEAD_DIM ** -0.5).astype(BF16)
    k_scr[...] = _dot(h, wqkv_ref[:, D_MODEL:2 * D_MODEL])
    v_scr[...] = _dot(h, wqkv_ref[:, 2 * D_MODEL:])
    masks = _head_masks()
    for s in range(nseq):
        rows = slice(s * seq, (s + 1) * seq)
        ko_ref[s] = k_scr[rows, :].reshape(seq, N_HEADS, HEAD_DIM)
        vo_ref[s] = v_scr[rows, :].reshape(seq, N_HEADS, HEAD_DIM)
        for p in range(D_MODEL // LANES):
            sl = slice(p * LANES, (p + 1) * LANES)
            q2 = q_scr[rows, sl]
            k2 = k_scr[rows, sl].astype(BF16)
            v2 = v_scr[rows, sl].astype(BF16)
            o2 = jnp.zeros(q2.shape, F32)
            for hm in masks:
                kh = jnp.where(hm, k2, 0.0)
                vh = jnp.where(hm, v2, 0.0)
                sc = _dot_nt(q2, kh)
                e = jnp.exp(sc - jnp.max(sc, axis=-1, keepdims=True))
                inv = 1.0 / jnp.sum(e, axis=-1, keepdims=True)
                o2 = o2 + _dot(e.astype(BF16), vh) * inv
            a_scr[rows, sl] = o2.astype(BF16)
    y = ALPHA * x + mod_ref[5:6, :] * _dot(a_scr[...], wo_ref[...])
    o_ref[...] = _layer_norm(y, g_ref[...], b_ref[...])


def _attn_ctx(x, mods, crow, w_qkv, w_o, widx, ln_g, ln_b, layer, batch, seq, nseq=2):
    rows = nseq * seq
    cache_spec = pl.BlockSpec((nseq, None, seq, N_HEADS, HEAD_DIM), lambda i: (i, 0, 0, 0, 0))
    cache_shape = jax.ShapeDtypeStruct((batch, 1, seq, N_HEADS, HEAD_DIM), F32)
    return pl.pallas_call(
        functools.partial(_attn_ctx_kernel, seq=seq, nseq=nseq),
        grid=(batch // nseq,),
        in_specs=[
            pl.BlockSpec((rows, D_MODEL), lambda i: (i, 0)),
            pl.BlockSpec((None, None, N_MOD, D_MODEL), lambda i: (layer, crow(i * rows), 0, 0)),
            _resident((None, D_MODEL, 3 * D_MODEL), lambda i: (widx, 0, 0)),
            _resident((None, D_MODEL, D_MODEL), lambda i: (widx, 0, 0)),
            pl.BlockSpec((None, None, 1, D_MODEL), lambda i: (layer, 1, 0, 0)),
            pl.BlockSpec((None, None, 1, D_MODEL), lambda i: (layer, 1, 0, 0)),
        ],
        out_specs=[pl.BlockSpec((rows, D_MODEL), lambda i: (i, 0)), cache_spec, cache_spec],
        out_shape=[jax.ShapeDtypeStruct((batch * seq, D_MODEL), F32), cache_shape, cache_shape],
        scratch_shapes=[pltpu.VMEM((rows, D_MODEL), BF16), pltpu.VMEM((rows, D_MODEL), F32),
                        pltpu.VMEM((rows, D_MODEL), F32), pltpu.VMEM((rows, D_MODEL), BF16)],
        compiler_params=_cparams("arbitrary"),
        name="attn_ctx",
    )(x, mods, w_qkv, w_o, ln_g, ln_b)


def _nbr_geometry(rows):
    kr = min(WIN_ROWS, rows)
    blocks = []
    for blk in range(rows // Q_ROWS):
        r0 = blk * Q_ROWS
        lo = min(max(r0 - kr // 2, 0), rows - kr)
        key_start = min(lo, rows - K_ROWS)
        bands = []
        for r in range(r0, r0 + Q_ROWS):
            rs = min(max(r - kr // 2, 0), rows - kr)
            a = rs - key_start
            band_start = min(a - a % 2, K_ROWS - BAND_ROWS)
            off = a - band_start
            assert 0 <= band_start and 0 <= off and off + kr <= BAND_ROWS
            bands.append((band_start, off, rs - r + WIN_ROWS - 1))
        blocks.append((key_start, bands))
    return kr, blocks


def _nbr_bias_rows(rpb):
    pad = GRID_W - (2 * WIN_COLS - 1)
    row = jnp.pad(rpb, ((0, 0), (0, 0), (0, pad)))
    return jnp.concatenate([row[:, :-1], row[:, 1:]], axis=-1)


def _attn_nbr_kernel(q_ref, k_ref, v_ref, kc_ref, vc_ref, rp_ref, o_ref,
                     s_ref, sc_ref, p_ref, pc_ref, inv_ref, tab_ref, *, blocks, kr):
    masks = _head_masks()
    left = masks[0]
    qn = Q_ROWS * GRID_W
    kn = K_ROWS * GRID_W
    qcol = lax.broadcasted_iota(jnp.int32, (GRID_W, LANES), 0)
    kcol = lax.broadcasted_iota(jnp.int32, (GRID_W, LANES), 1) % GRID_W
    cstart = jnp.clip(qcol - WIN_COLS // 2, 0, GRID_W - WIN_COLS)
    col_ok = (kcol >= cstart) & (kcol < cstart + WIN_COLS)
    for hh in range(2):
        for d in range(2 * WIN_ROWS - 2):
            row = jnp.broadcast_to(rp_ref[hh, d:d + 1, :], (GRID_W, LANES))
            toep = pltpu.roll(row, LANES - (WIN_COLS - 1), 1, stride=1, stride_axis=0)
            tab_ref[hh, d] = jnp.where(col_ok, toep, NEG)
    kc, vc = kc_ref[...].astype(BF16), vc_ref[...].astype(BF16)
    ctx_kv = [(jnp.where(hm, kc, 0.0), jnp.where(hm, vc, 0.0)) for hm in masks]
    for blk, (key_start, bands) in enumerate(blocks):
        q2 = (q_ref[blk * qn:(blk + 1) * qn, :] * HEAD_DIM ** -0.5).astype(BF16)
        k2 = k_ref[key_start * GRID_W:key_start * GRID_W + kn, :]
        v2 = v_ref[key_start * GRID_W:key_start * GRID_W + kn, :]
        o2 = jnp.zeros(q2.shape, F32)
        p_ref[...] = jnp.zeros_like(p_ref)
        for hh, hm in enumerate(masks):
            kh = jnp.where(hm, k2, 0.0)
            vh = jnp.where(hm, v2, 0.0)
            kch, vch = ctx_kv[hh]
            s_ref[...] = _dot_nt(q2, kh)
            sc_ref[...] = _dot_nt(q2, kch)
            for rl, (band_start, off, drow) in enumerate(bands):
                rsl = slice(rl * GRID_W, (rl + 1) * GRID_W)
                tiles = []
                for j in range(0, BAND_ROWS, 2):
                    ok0, ok1 = off <= j < off + kr, off <= j + 1 < off + kr
                    if not (ok0 or ok1):
                        continue
                    bias = tab_ref[hh, drow - off + j]
                    if not ok0:
                        bias = jnp.where(left, NEG, bias)
                    if not ok1:
                        bias = jnp.where(left, bias, NEG)
                    csl = slice((band_start + j) * GRID_W, (band_start + j + 2) * GRID_W)
                    tiles.append((csl, s_ref[rsl, csl] + bias))
                sc = sc_ref[rsl, :]
                top = functools.reduce(jnp.maximum, _lane_tiles(sc) + [sb for _, sb in tiles])
                mx = jnp.max(top, axis=-1, keepdims=True)
                ec = jnp.exp(sc - mx)
                tot = functools.reduce(jnp.add, _lane_tiles(ec))
                for csl, sb in tiles:
                    eb = jnp.exp(sb - mx)
                    tot = tot + eb
                    p_ref[rsl, csl] = eb.astype(BF16)
                inv_ref[rsl, :] = 1.0 / jnp.sum(tot, axis=-1, keepdims=True)
                pc_ref[rsl, :] = ec.astype(BF16)
            o2 = o2 + (_dot(p_ref[...], vh) + _dot(pc_ref[...], vch)) * inv_ref[...]
        o_ref[blk * qn:(blk + 1) * qn, :] = o2.astype(o_ref.dtype)


def _attn_nbr(qkv, kc, vc, rpb, batch, seq):
    rows = seq // GRID_W
    kr, blocks = _nbr_geometry(rows)
    n_pairs = D_MODEL // LANES
    n_d = 2 * WIN_ROWS - 2
    rp = _nbr_bias_rows(rpb).reshape(n_pairs, 2, n_d, LANES)
    s_ctx = kc.shape[1]
    qspec = lambda g: pl.BlockSpec((None, seq, LANES), lambda b, p: (g, b, p))
    cspec = pl.BlockSpec((None, s_ctx, LANES), lambda b, p: (b, 0, p))
    qn, kn = Q_ROWS * GRID_W, K_ROWS * GRID_W
    return pl.pallas_call(
        functools.partial(_attn_nbr_kernel, blocks=blocks, kr=kr),
        grid=(batch, n_pairs),
        in_specs=[qspec(0), qspec(1), qspec(2), cspec, cspec,
                  pl.BlockSpec((None, 2, n_d, LANES), lambda b, p: (p, 0, 0, 0))],
        out_specs=pl.BlockSpec((seq, LANES), lambda b, p: (b, p)),
        out_shape=jax.ShapeDtypeStruct((batch * seq, D_MODEL), BF16),
        scratch_shapes=[pltpu.VMEM((qn, kn), F32), pltpu.VMEM((qn, s_ctx), F32),
                        pltpu.VMEM((qn, kn), BF16), pltpu.VMEM((qn, s_ctx), BF16),
                        pltpu.VMEM((qn, 1), F32), pltpu.VMEM((2, n_d, GRID_W, LANES), F32)],
        compiler_params=_cparams("arbitrary", "arbitrary"),
        name="attn_nbr",
    )(qkv, qkv, qkv, kc, vc, rp)


def _dft_tables(seq_len, tq, tr):
    n = 2 * seq_len
    f = np.arange(seq_len, dtype=np.float64)[:, None] + 0.5
    s = np.arange(seq_len, dtype=np.float64)[None, :]
    ang = 2.0 * np.pi * f * s / n
    re = np.cos(ang).reshape(seq_len // tq, tq, seq_len)
    im = (-np.sin(ang)).reshape(seq_len // tq, tq, seq_len)
    fwd = np.concatenate([re, im], axis=1).astype(np.float32)
    bwd = fwd.reshape(2 * seq_len, seq_len).T
    return fwd, np.ascontiguousarray(bwd).reshape(seq_len // tr, tr, 2 * seq_len)


def _split_spectrum_table(seq_len, tq):
    half = seq_len // 2
    f = np.arange(half, dtype=np.float64)[:, None] + 0.5
    s = np.arange(seq_len, dtype=np.float64)[None, :]
    lo = np.pi * f * s / seq_len
    hi = np.pi * (seq_len - f) * s / seq_len
    parts = [np.cos(lo), -np.sin(lo), np.cos(hi), np.sin(hi)]
    return np.concatenate([p.reshape(half // tq, tq, seq_len) for p in parts], axis=1).astype(np.float32)


def _twiddles(seq_len):
    th = np.pi * (np.arange(seq_len // 2, dtype=np.float64) + 0.5) / seq_len
    return np.stack([np.cos(th), np.sin(th)])[:, :, None].repeat(LANES, axis=2).astype(np.float32)


def _deinterleave_matrix():
    sel = np.zeros((SPLIT_ROWS, SPLIT_ROWS), np.float32)
    i = np.arange(SPLIT_ROWS // 2)
    sel[i, 2 * i] = 1.0
    sel[SPLIT_ROWS // 2 + i, 2 * i + 1] = 1.0
    return sel


def _filter_features(seq_len):
    pos = np.arange(seq_len, dtype=np.float64)[:, None]
    t = pos / max(seq_len - 1, 1)
    bands = np.linspace(1e-4, HY_BANDS - 1, HY_BANDS)
    ang = 2.0 * np.pi * pos / seq_len * bands
    z = np.concatenate([t, np.cos(ang), -np.sin(ang)], axis=-1)
    zp = np.zeros((seq_len, HY_FILTER_W), np.float64)
    zp[:, :HY_EMB] = z
    deltas = np.abs(np.linspace(math.log(HY_TARGET) / HY_SLOW_DECAY, math.log(HY_TARGET) / HY_FAST_DECAY, D_MODEL))
    return zp.astype(np.float32), t.astype(np.float32), deltas.astype(np.float32)[None, :]


def _filter_kernel(z_ref, t_ref, dl_ref, w1_ref, b1_ref, f1_ref, w2_ref, b2_ref, f2_ref, w3_ref, b3_ref, o_ref,
                   hid_ref, decay_ref):
    hp = lax.Precision.HIGHEST

    @pl.when(pl.program_id(0) == 0)
    def _():
        hid = jnp.sin(f1_ref[...] * (jnp.dot(z_ref[...], w1_ref[...], precision=hp, preferred_element_type=F32) + b1_ref[...]))
        hid_ref[...] = jnp.sin(f2_ref[...] * (jnp.dot(hid, w2_ref[...], precision=hp, preferred_element_type=F32) + b2_ref[...]))
        decay_ref[...] = jnp.exp(-t_ref[...] * dl_ref[...])

    hid, w3 = hid_ref[...], w3_ref[...]
    hid_hi, w3_hi = hid.astype(BF16), w3.astype(BF16)
    hid_lo = (hid - hid_hi.astype(F32)).astype(BF16)
    w3_lo = (w3 - w3_hi.astype(F32)).astype(BF16)
    filt = _dot(hid_hi, w3_hi) + (_dot(hid_lo, w3_hi) + _dot(hid_hi, w3_lo)) + b3_ref[...]
    decay = decay_ref[...]
    hf = filt[:, :D_MODEL] * decay
    hb = filt[:, D_MODEL:] * decay
    row = lax.broadcasted_iota(jnp.int32, (hb.shape[0], 1), 0)
    hb = jnp.where(row == 0, 0.0, hb)
    o_ref[:, :D_MODEL] = (hf + hb).astype(o_ref.dtype)
    o_ref[:, D_MODEL:] = (hf - hb).astype(o_ref.dtype)


def _filters(seq_len, w1, b1, f1, w2, b2, f2, w3, b3):
    z, t, deltas = _filter_features(seq_len)
    w1p = jnp.zeros((HY_FILTER_W, HY_FILTER_W), F32).at[:HY_EMB].set(w1)
    full = lambda shape: pl.BlockSpec(shape, lambda o: (0,) * len(shape))
    row = lambda a: a.reshape(1, -1)
    return pl.pallas_call(
        _filter_kernel,
        grid=(2,),
        in_specs=[full((seq_len, HY_FILTER_W)), full((seq_len, 1)), full((1, D_MODEL)),
                  full((HY_FILTER_W, HY_FILTER_W)), full((1, HY_FILTER_W)), full((1, HY_FILTER_W)),
                  full((HY_FILTER_W, HY_FILTER_W)), full((1, HY_FILTER_W)), full((1, HY_FILTER_W)),
                  pl.BlockSpec((HY_FILTER_W, 2 * D_MODEL), lambda o: (0, o)),
                  pl.BlockSpec((1, 2 * D_MODEL), lambda o: (0, o))],
        out_specs=pl.BlockSpec((None, seq_len, 2 * D_MODEL), lambda o: (o, 0, 0)),
        out_shape=jax.ShapeDtypeStruct((2, seq_len, 2 * D_MODEL), BF16),
        scratch_shapes=[pltpu.VMEM((seq_len, HY_FILTER_W), F32), pltpu.VMEM((seq_len, D_MODEL), F32)],
        compiler_params=_cparams("arbitrary"),
        name="hyena_filters",
    )(jnp.asarray(z),jnp.asarray(t), jnp.asarray(deltas), w1p, row(b1), row(f1), w2, row(b2), row(f2), w3, row(b3))


def _spectrum_kernel(g_ref, hs_ref, hd_ref, o_ref, *, tq, scale):
    for part in range(g_ref.shape[0] // tq):
        rows = slice(part * tq, (part + 1) * tq)
        h_ref = hs_ref if part % 2 == 0 else hd_ref
        o_ref[rows, :] = scale * _dot(g_ref[rows, :], h_ref[...])


def _spectrum(table, filt, seq_len, tq, tc):
    nt, rows, _ = table.shape
    ncb = D_MODEL // tc
    return pl.pallas_call(
        functools.partial(_spectrum_kernel, tq=tq, scale=1.0 / seq_len),
        grid=(nt, 2, ncb),
        in_specs=[pl.BlockSpec((None, rows, seq_len), lambda f, o, c: (f, 0, 0)),
                  pl.BlockSpec((None, seq_len, tc), lambda f, o, c: (o, 0, c)),
                  pl.BlockSpec((None, seq_len, tc), lambda f, o, c: (o, 0, c + ncb))],
        out_specs=pl.BlockSpec((None, None, rows, tc), lambda f, o, c: (o, f, 0, c)),
        out_shape=jax.ShapeDtypeStruct((2, nt, rows, D_MODEL), F32),
        compiler_params=_cparams("arbitrary", "arbitrary", "arbitrary"),
        name="hyena_spectrum",
    )(table, filt, filt)


def _hyena_kernel(v_ref, gate_ref, g_ref, gt_ref, k_ref, tw_ref, sel_ref, selt_ref, b_ref, o_ref,
                  ze_ref, zo_ref, pu_ref, pv_ref, *, tq, nf, tr):
    o = pl.program_id(2)
    st = pl.program_id(3)
    half = ze_ref.shape[0]
    tc = ze_ref.shape[1]
    hs = SPLIT_ROWS // 2

    @pl.when((o == 0) & (st == 0))
    def _():
        for j in range(half // hs):
            sp = _dot(sel_ref[...], v_ref[j * SPLIT_ROWS:(j + 1) * SPLIT_ROWS, :])
            ze_ref[j * hs:(j + 1) * hs, :] = sp[:hs].astype(BF16)
            zo_ref[j * hs:(j + 1) * hs, :] = sp[hs:].astype(BF16)

    @pl.when(st < nf)
    def _():
        base = pl.multiple_of(st * 2 * tq, 2 * tq)
        c, s = tw_ref[0], tw_ref[1]
        group = 2 * LANES
        for lt in range(tc // LANES):
            cols = slice(lt * LANES, (lt + 1) * LANES)
            if lt % 2 == 0:
                gcols = slice(lt * LANES, lt * LANES + group)
                fa = _dot(g_ref[...], ze_ref[:, gcols])
                fb = _dot(g_ref[...], zo_ref[:, gcols])
            loc = slice((lt % 2) * LANES, (lt % 2 + 1) * LANES)
            ar, ai, br, bi = fa[:tq, loc], fa[tq:, loc], fb[:tq, loc], fb[tq:, loc]
            tr_, ti = c * br + s * bi, c * bi - s * br
            s1r, s1i, s2r, s2i = ar + tr_, ai + ti, ar - tr_, ai - ti
            k1r, k1i = k_ref[0:tq, cols], k_ref[tq:2 * tq, cols]
            k2r, k2i = k_ref[2 * tq:3 * tq, cols], k_ref[3 * tq:, cols]
            p1r, p1i = s1r * k1r - s1i * k1i, s1r * k1i + s1i * k1r
            p2r, p2i = s2r * k2r - s2i * k2i, s2r * k2i + s2i * k2r
            wr, wi = p1r - p2r, p1i - p2i
            pu_ref[pl.ds(base, tq), cols] = (p1r + p2r).astype(BF16)
            pu_ref[pl.ds(base + tq, tq), cols] = (p1i + p2i).astype(BF16)
            pv_ref[pl.ds(base, tq), cols] = (c * wr - s * wi).astype(BF16)
            pv_ref[pl.ds(base + tq, tq), cols] = (c * wi + s * wr).astype(BF16)

    @pl.when(st >= nf)
    def _():
        r0 = pl.multiple_of((st - nf) * tr, tr)
        rows = pl.ds(r0, tr)
        ye = _dot(gt_ref[...], pu_ref[...]) + ze_ref[rows, :].astype(F32) * b_ref[...]
        yo = _dot(gt_ref[...], pv_ref[...]) + zo_ref[rows, :].astype(F32) * b_ref[...]
        block = lambda j: pl.ds(pl.multiple_of(2 * r0 + j * SPLIT_ROWS, SPLIT_ROWS), SPLIT_ROWS)
        gates = [_dot(sel_ref[...], gate_ref[block(j), :]) for j in range(tr // hs)]
        ge = ye * jnp.concatenate([g[:hs] for g in gates], axis=0)
        go = yo * jnp.concatenate([g[hs:] for g in gates], axis=0)

        @pl.when(o == 0)
        def _():
            ze_ref[rows, :] = ge.astype(BF16)
            zo_ref[rows, :] = go.astype(BF16)

        @pl.when(o == 1)
        def _():
            for j in range(tr // hs):
                blk = jnp.concatenate([ge[j * hs:(j + 1) * hs], go[j * hs:(j + 1) * hs]], axis=0).astype(BF16)
                o_ref[block(j), :] = _dot(selt_ref[...], blk).astype(o_ref.dtype)


def _hyena_core(u, fwd_h, bwd_h, spec, tw, sel, bias, batch, seq_len, tq, tc):
    half = seq_len // 2
    nf = half // tq
    nr, tr, _ = bwd_h.shape
    ncb = D_MODEL // tc
    fidx = lambda st: jnp.minimum(st, nf - 1)
    whole = lambda shape: pl.BlockSpec(shape, lambda b, c, o, st: (0,) * len(shape))
    return pl.pallas_call(
        functools.partial(_hyena_kernel, tq=tq, nf=nf, tr=tr),
        grid=(batch, ncb, 2, nf + nr),
        in_specs=[pl.BlockSpec((None, seq_len, tc), lambda b, c, o, st: (0, b, c)),
                  pl.BlockSpec((None, seq_len, tc), lambda b, c, o, st: (1 + o, b, c)),
                  pl.BlockSpec((None, 2 * tq, half), lambda b, c, o, st: (fidx(st), 0, 0)),
                  pl.BlockSpec((None, tr, seq_len), lambda b, c, o, st: (jnp.maximum(st - nf, 0), 0, 0)),
                  pl.BlockSpec((None, None, 4 * tq, tc), lambda b, c, o, st: (o, fidx(st), 0, c)),
                  pl.BlockSpec((2, tq, LANES), lambda b, c, o, st: (0, fidx(st), 0)),
                  whole((SPLIT_ROWS, SPLIT_ROWS)), whole((SPLIT_ROWS, SPLIT_ROWS)),
                  pl.BlockSpec((None, 1, tc), lambda b, c, o, st: (o, 0, c))],
        out_specs=pl.BlockSpec((seq_len, tc), lambda b, c, o, st: (b, c)),
        out_shape=jax.ShapeDtypeStruct((batch * seq_len, D_MODEL), BF16),
        scratch_shapes=[pltpu.VMEM((half, tc), BF16), pltpu.VMEM((half, tc), BF16),
                        pltpu.VMEM((seq_len, tc), BF16), pltpu.VMEM((seq_len, tc), BF16)],
        compiler_params=_cparams("arbitrary", "arbitrary", "arbitrary", "arbitrary"),
        name="hyena_conv",
    )(u, u, fwd_h, bwd_h, spec, tw, sel, sel.T, bias.reshape(2, 1, D_MODEL))


def _hyena_layer_kernel(x_ref, mod_ref, win_ref, cw_ref, cb_ref, g_ref, gt_ref, k_ref, hb_ref, wout_ref,
                        lg_ref, lb_ref, o_ref, u_scr, *, seq_len, nseq, tn):
    x = x_ref[...]
    h = (x * (1.0 + mod_ref[4:5, :]) + mod_ref[3:4, :]).astype(BF16)
    rows = nseq * seq_len
    pos = lax.broadcasted_iota(jnp.int32, (rows, 1), 0) % seq_len
    first, last = pos == 0, pos == seq_len - 1
    for g in range(3):
        for cc in range(D_MODEL // tn):
            cols = slice(g * D_MODEL + cc * tn, g * D_MODEL + (cc + 1) * tn)
            u = _dot(h, win_ref[:, cols])
            prev = jnp.where(first, 0.0, pltpu.roll(u, 1, 0))
            nxt = jnp.where(last, 0.0, pltpu.roll(u, rows - 1, 0))
            u = prev * cw_ref[0:1, cols] + u * cw_ref[1:2, cols] + nxt * cw_ref[2:3, cols] + cb_ref[:, cols]
            u_scr[g, :, cc * tn:(cc + 1) * tn] = u.astype(BF16)
    for o in range(2):
        kr, ki = k_ref[o, :seq_len, :], k_ref[o, seq_len:, :]
        for s in range(nseq):
            r = slice(s * seq_len, (s + 1) * seq_len)
            z = u_scr[0, r, :]
            zf = _dot(g_ref[...], z)
            zr, zi = zf[:seq_len], zf[seq_len:]
            prod = jnp.concatenate([zr * kr - zi * ki, zr * ki + zi * kr], axis=0).astype(BF16)
            y = _dot(gt_ref[...], prod) + z.astype(F32) * hb_ref[o]
            u_scr[0, r, :] = (y * u_scr[1 + o, r, :].astype(F32)).astype(BF16)
    y = ALPHA * x + mod_ref[5:6, :] * _dot(u_scr[0], wout_ref[...])
    o_ref[...] = _layer_norm(y, lg_ref[...], lb_ref[...])


def _hyena_layer(x, mods, crow, w_in, w_out, widx, conv_w, conv_b, fwd, bwd, spec, bias, ln_g, ln_b, layer,
                 seq_len, nseq, tn=512):
    m = x.shape[0]
    rows = nseq * seq_len
    assert fwd.shape[0] == 1
    return pl.pallas_call(
        functools.partial(_hyena_layer_kernel, seq_len=seq_len, nseq=nseq, tn=tn),
        grid=(m // rows,),
        in_specs=[
            pl.BlockSpec((rows, D_MODEL), lambda i: (i, 0)),
            pl.BlockSpec((None, None, N_MOD, D_MODEL), lambda i: (layer, crow(i * rows), 0, 0)),
            _resident((None, D_MODEL, 3 * D_MODEL), lambda i: (widx, 0, 0)),
            pl.BlockSpec((None, 3, 3 * D_MODEL), lambda i: (widx, 0, 0)),
            pl.BlockSpec((None, 1, 3 * D_MODEL), lambda i: (widx, 0, 0)),
            pl.BlockSpec((None, 2 * seq_len, seq_len), lambda i: (0, 0, 0)),
            pl.BlockSpec((None, seq_len, 2 * seq_len), lambda i: (0, 0, 0)),
            _resident((2, None, 2 * seq_len, D_MODEL), lambda i: (0, 0, 0, 0)),
            pl.BlockSpec((2, 1, D_MODEL), lambda i: (0, 0, 0)),
            _resident((None, D_MODEL, D_MODEL), lambda i: (widx, 0, 0)),
            pl.BlockSpec((None, None, 1, D_MODEL), lambda i: (layer, 1, 0, 0)),
            pl.BlockSpec((None, None, 1, D_MODEL), lambda i: (layer, 1, 0, 0)),
        ],
        out_specs=pl.BlockSpec((rows, D_MODEL), lambda i: (i, 0)),
        out_shape=jax.ShapeDtypeStruct((m, D_MODEL), F32),
        scratch_shapes=[pltpu.VMEM((3, rows, D_MODEL), BF16)],
        compiler_params=_cparams("arbitrary"),
        name="hyena_layer",
    )(x, mods, w_in, conv_w, conv_b.reshape(conv_b.shape[0], 1, conv_b.shape[1]), fwd, bwd, spec,
      bias.reshape(2, 1, D_MODEL), w_out, ln_g, ln_b)


def _hyena_tiles(seq_len, batch):
    tq = tr = min(seq_len, 512)
    tc = D_MODEL if seq_len <= 256 else 512
    nseq = math.gcd(batch, max(1, 1024 // seq_len))
    return tq, tr, tc, nseq


def kernel(x_prompt, x_sample, cache_k, cache_v, c, c_ctx, mod_w, mod_b, ln_g, ln_b, ffn_w_in, ffn_w_out,
           attn_w_qkv, attn_w_o, attn_rpb, hy_w_in, hy_conv_w, hy_conv_b, hy_f_w1, hy_f_b1, hy_f_freq1,
           hy_f_w2, hy_f_b2, hy_f_freq2, hy_f_w3, hy_f_b3, hy_bias, hy_w_out):
    batch, seq, _ = x_prompt.shape
    dec_batch, dec_seq, _ = x_sample.shape
    assert dec_batch + 1 <= COND_ROWS

    conds = jnp.zeros((COND_ROWS, D_MODEL), F32).at[0].set(c_ctx).at[1:1 + dec_batch].set(c)
    mods = _mods(conds, mod_w, mod_b).reshape(DEPTH, COND_ROWS, N_MOD, D_MODEL)
    ln_g4 = ln_g.reshape(DEPTH, 3, 1, D_MODEL)
    ln_b4 = ln_b.reshape(DEPTH, 3, 1, D_MODEL)

    streams = {
        "ctx": dict(x=x_prompt.reshape(batch * seq, D_MODEL), batch=batch, seq=seq, crow=lambda t: 0),
        "lat": dict(x=x_sample.reshape(dec_batch * dec_seq, D_MODEL), batch=dec_batch, seq=dec_seq,
                    crow=lambda t: 1 + t // dec_seq),
    }
    as_bf16 = lambda table: jnp.asarray(table).astype(BF16)
    hyena_consts = {}
    for name, st in streams.items():
        sl = st["seq"]
        tq, tr, tc, nseq = _hyena_tiles(sl, st["batch"])
        if sl == tq:
            fwd, bwd = _dft_tables(sl, tq, tr)
            hyena_consts[name] = dict(tq=tq, tc=tc, nseq=nseq, fwd=as_bf16(fwd), bwd=as_bf16(bwd), ktab=as_bf16(fwd))
        else:
            fwd, bwd = _dft_tables(sl // 2, tq, tr)
            hyena_consts[name] = dict(tq=tq, tc=tc, nseq=nseq, fwd=as_bf16(fwd), bwd=as_bf16(bwd),
                                      ktab=as_bf16(_split_spectrum_table(sl, tq)),
                                      tw=jnp.asarray(_twiddles(sl)), sel=as_bf16(_deinterleave_matrix()))

    attn_w_qkv, attn_w_o, hy_w_in, hy_w_out = (w.astype(BF16) for w in (attn_w_qkv, attn_w_o, hy_w_in, hy_w_out))
    ffn_order = [(layer, which) for layer in range(DEPTH) for which in (0, 1)]
    ffn_bf16 = {}

    def ffn(x, crow, layer, which, mixer_out=None):
        key = (layer, which)
        if key not in ffn_bf16:
            ffn_bf16[key] = (ffn_w_in[layer, which].astype(BF16), ffn_w_out[layer, which].astype(BF16))
        later = ffn_order[ffn_order.index(key) + 1:]
        cast_next = None
        if later and later[0] not in ffn_bf16 and _ffn_can_cast(x.shape[0]):
            cast_next = (ffn_w_in, ffn_w_out, *later[0])
        y, cast = _ffn(x, mods, crow, *ffn_bf16[key], ln_g4, ln_b4, layer, which, mixer_out=mixer_out,
                       cast_next=cast_next)
        if cast_next is not None:
            ffn_bf16[later[0]] = cast
        return y

    new_k, new_v = [], []
    outs = {}
    for name, st in streams.items():
        x, nb, sl, crow = st["x"], st["batch"], st["seq"], st["crow"]
        for layer in range(DEPTH):
            j = layer // 2
            x = ffn(x, crow, layer, 0)
            mixer_out = None
            if layer % 2 == 0:
                if name == "ctx":
                    x, k_new, v_new = _attn_ctx(x, mods, crow, attn_w_qkv, attn_w_o, j, ln_g4, ln_b4, layer, nb, sl)
                    new_k.append(k_new)
                    new_v.append(v_new)
                else:
                    qkv = _proj(x, mods, crow, attn_w_qkv, layer, j, tm=sl, tn=1024, out_dtype=BF16)
                    kc = cache_k[:, j].reshape(nb, -1, D_MODEL)
                    vc = cache_v[:, j].reshape(nb, -1, D_MODEL)
                    mixer_out = (_attn_nbr(qkv, kc, vc, attn_rpb[j], nb, sl), attn_w_o, j)
            else:
                hc = hyena_consts[name]
                tq, tc = hc["tq"], hc["tc"]
                filt = _filters(sl, hy_f_w1[j], hy_f_b1[j], hy_f_freq1[j], hy_f_w2[j], hy_f_b2[j], hy_f_freq2[j],
                                hy_f_w3[j], hy_f_b3[j])
                spec = _spectrum(hc["ktab"], filt, sl, tq, tc)
                if sl == tq:
                    x = _hyena_layer(x, mods, crow, hy_w_in, hy_w_out, j, hy_conv_w, hy_conv_b, hc["fwd"], hc["bwd"],
                                     spec, hy_bias[j], ln_g4, ln_b4, layer, sl, hc["nseq"])
                else:
                    u = _proj(x, mods, crow, hy_w_in, layer, j, tm=max(sl, 1024), tn=512, out_dtype=BF16,
                              conv=(hy_conv_w, hy_conv_b), seq_len=sl)
                    a = _hyena_core(u, hc["fwd"], hc["bwd"], spec, hc["tw"], hc["sel"], hy_bias[j], nb, sl, tq, tc)
                    mixer_out = (a, hy_w_out, j)
            x = ffn(x, crow, layer, 1, mixer_out=mixer_out)
        outs[name] = x

    y_prompt = outs["ctx"].reshape(batch, seq, D_MODEL)
    y_sample = outs["lat"].reshape(dec_batch, dec_seq, D_MODEL)
    return (y_prompt, y_sample, jnp.concatenate(new_k, axis=1), jnp.concatenate(new_v, axis=1))
```

```python
import functools
import math

import numpy as np
import jax
import jax.numpy as jnp
from jax import lax
from jax.experimental import pallas as pl
from jax.experimental.pallas import tpu as pltpu

D_MODEL = 1024
DEPTH = 2
GRID_W = 64
N_HEADS = 16
HEAD_DIM = D_MODEL // N_HEADS
WIN_ROWS = 8
WIN_COLS = 16
D_FF = 2816
N_MOD = 9
HY_BANDS = 16
HY_EMB = 1 + 2 * HY_BANDS
HY_FILTER_W = 64
HY_FAST_DECAY = 0.3
HY_SLOW_DECAY = 1.5
HY_TARGET = 1e-2
ALPHA = (2 * DEPTH) ** 0.25
LN_EPS = 1e-5

F32 = jnp.float32
BF16 = jnp.bfloat16
NEG = -1e30
LANES = 128
VMEM_LIMIT = 56 * 1024 * 1024
COND_ROWS = 8

Q_ROWS = 8
K_ROWS = 16
BAND_ROWS = 10
SPLIT_ROWS = 256


def _cparams(*sem):
    return pltpu.CompilerParams(dimension_semantics=sem, vmem_limit_bytes=VMEM_LIMIT)


def _dot(a, b):
    return jnp.dot(a, b, preferred_element_type=F32)


def _dot_nt(a, b):
    return lax.dot_general(a, b, (((1,), (1,)), ((), ())), preferred_element_type=F32)


def _layer_norm(y, g, b):
    mu = jnp.mean(y, axis=-1, keepdims=True)
    yc = y - mu
    var = jnp.mean(yc * yc, axis=-1, keepdims=True)
    return yc * lax.rsqrt(var + LN_EPS) * g + b


def _mods_kernel(c_ref, w_ref, b_ref, o_ref):
    c = c_ref[...]
    s = c * jax.nn.sigmoid(c)
    o_ref[...] = _dot(s.astype(BF16), w_ref[...].astype(BF16)) + b_ref[...]


def _mods(conds, mod_w, mod_b):
    nd = N_MOD * D_MODEL
    tn = nd // 4
    return pl.pallas_call(
        _mods_kernel,
        grid=(DEPTH, nd // tn),
        in_specs=[
            pl.BlockSpec((COND_ROWS, D_MODEL), lambda l, j: (0, 0)),
            pl.BlockSpec((None, D_MODEL, tn), lambda l, j: (l, 0, j)),
            pl.BlockSpec((None, 1, tn), lambda l, j: (l, 0, j)),
        ],
        out_specs=pl.BlockSpec((None, COND_ROWS, tn), lambda l, j: (l, 0, j)),
        out_shape=jax.ShapeDtypeStruct((DEPTH, COND_ROWS, nd), F32),
        compiler_params=_cparams("arbitrary", "arbitrary"),
        name="mods",
    )(conds, mod_w, mod_b.reshape(DEPTH, 1, nd))


def _ffn_kernel(*refs, mi, n_chunks, starts, mixers, cast_next):
    refs = list(refs)
    take = lambda n: [refs.pop(0) for _ in range(n)]
    stream_refs = [take(3 if has_mixer else 2) for has_mixer in mixers]
    wi_ref, wo_ref, g_ref, b_ref = take(4)
    if any(mixers):
        wmix_ref, gmix_ref, bmix_ref = take(3)
    if cast_next:
        wi_next_ref, wo_next_ref = take(2)
    out_refs = take(len(mixers))
    s = pl.program_id(0)

    def half_step(x_ref, mod_ref, a_ref, o_ref):
        x = x_ref[...]
        if a_ref is not None:
            x = _layer_norm(ALPHA * x + mod_ref[5:6, :] * _dot(a_ref[...], wmix_ref[...]), gmix_ref[...], bmix_ref[...])
        sh = mod_ref[3 * mi:3 * mi + 1, :]
        sc = mod_ref[3 * mi + 1:3 * mi + 2, :]
        gt = mod_ref[3 * mi + 2:3 * mi + 3, :]
        h = (x * (1.0 + sc) + sh).astype(BF16)
        tf = D_FF // n_chunks
        f = None
        for c in range(n_chunks):
            gate = _dot(h, wi_ref[:, c * tf:(c + 1) * tf])
            up = _dot(h, wi_ref[:, D_FF + c * tf:D_FF + (c + 1) * tf])
            a = (gate * jax.nn.sigmoid(gate) * up).astype(BF16)
            part = _dot(a, wo_ref[c * tf:(c + 1) * tf, :])
            f = part if f is None else f + part
        y = ALPHA * x + (0.5 * gt) * f
        o_ref[...] = _layer_norm(y, g_ref[...], b_ref[...])

    for k, (x_ref, mod_ref, *a_ref) in enumerate(stream_refs):
        @pl.when((s >= starts[k]) & (s < starts[k + 1]))
        def _(k=k, x_ref=x_ref, mod_ref=mod_ref, a_ref=a_ref):
            half_step(x_ref, mod_ref, a_ref[0] if a_ref else None, out_refs[k])
            if cast_next and k == 0:
                wi_cast_ref, wo_cast_ref = refs
                wi_cast_ref[...] = wi_next_ref[...].astype(BF16)
                wo_cast_ref[...] = wo_next_ref[...].astype(BF16)


def _resident(shape, index_map):
    return pl.BlockSpec(shape, index_map, pipeline_mode=pl.Buffered(1))


BF16_ROWS = 16


FFN_ROWS = 512


def _ffn_can_cast(m, tm=FFN_ROWS):
    steps = m // tm
    return D_MODEL % (steps * BF16_ROWS) == 0 and D_FF % (steps * BF16_ROWS) == 0


def _ffn(xs, mods, crows, w_in, w_out, ln_g, ln_b, layer, which, mixer_outs, w_mix=None, cast_next=None,
         tm=FFN_ROWS, n_chunks=11):
    mi = 0 if which == 0 else 2
    steps = [x.shape[0] // tm for x in xs]
    starts = [sum(steps[:k]) for k in range(len(xs) + 1)]
    mixers = tuple(a is not None for a in mixer_outs)
    ln_spec = lambda idx: pl.BlockSpec((None, None, 1, D_MODEL), lambda s: (layer, idx, 0, 0))

    def tile_of(k):
        return lambda s: jnp.clip(s - starts[k], 0, steps[k] - 1)

    in_specs, args, out_specs, out_shape = [], [], [], []
    for k, (x, crow, a) in enumerate(zip(xs, crows, mixer_outs)):
        tile = tile_of(k)
        row_spec = pl.BlockSpec((tm, D_MODEL), lambda s, tile=tile: (tile(s), 0))
        in_specs += [row_spec, pl.BlockSpec((None, None, N_MOD, D_MODEL),
                                            lambda s, tile=tile, crow=crow: (layer, crow(tile(s) * tm), 0, 0))]
        args += [x, mods]
        if a is not None:
            in_specs.append(row_spec)
            args.append(a)
        out_specs.append(row_spec)
        out_shape.append(jax.ShapeDtypeStruct(x.shape, F32))
    in_specs += [_resident((D_MODEL, 2 * D_FF), lambda s: (0, 0)), _resident((D_FF, D_MODEL), lambda s: (0, 0)),
                 ln_spec(mi), ln_spec(mi)]
    args += [w_in, w_out, ln_g, ln_b]
    if any(mixers):
        w, widx = w_mix
        in_specs += [_resident((None, D_MODEL, D_MODEL), lambda s: (widx, 0, 0)), ln_spec(1), ln_spec(1)]
        args += [w, ln_g, ln_b]
    if cast_next is not None:
        w_in_all, w_out_all, layer2, which2 = cast_next
        slab = tile_of(0)
        slab_in, slab_out = D_MODEL // steps[0], D_FF // steps[0]
        in_specs += [pl.BlockSpec((None, None, slab_in, 2 * D_FF), lambda s: (layer2, which2, slab(s), 0)),
                     pl.BlockSpec((None, None, slab_out, D_MODEL), lambda s: (layer2, which2, slab(s), 0))]
        args += [w_in_all, w_out_all]
        out_specs += [pl.BlockSpec((slab_in, 2 * D_FF), lambda s: (slab(s), 0)),
                      pl.BlockSpec((slab_out, D_MODEL), lambda s: (slab(s), 0))]
        out_shape += [jax.ShapeDtypeStruct((D_MODEL, 2 * D_FF), BF16), jax.ShapeDtypeStruct((D_FF, D_MODEL), BF16)]
    outs = pl.pallas_call(
        functools.partial(_ffn_kernel, mi=mi, n_chunks=n_chunks, starts=tuple(starts), mixers=mixers,
                          cast_next=cast_next is not None),
        grid=(starts[-1],),
        in_specs=in_specs,
        out_specs=out_specs,
        out_shape=out_shape,
        compiler_params=_cparams("arbitrary"),
        name="ffn" + ("_mix" if any(mixers) else "") + ("_cast" if cast_next is not None else ""),
    )(*args)
    return list(outs[:len(xs)]), tuple(outs[len(xs):])


def _proj_kernel(x_ref, mod_ref, w_ref, *rest, seq_len, tn):
    if seq_len:
        cw_ref, cb_ref, o_ref, h_ref = rest
    else:
        o_ref, h_ref = rest

    @pl.when(pl.program_id(1) == 0)
    def _():
        sh = mod_ref[3:4, :]
        sc = mod_ref[4:5, :]
        h_ref[...] = (x_ref[...] * (1.0 + sc) + sh).astype(BF16)

    tm = h_ref.shape[0]
    if seq_len:
        pos = lax.broadcasted_iota(jnp.int32, (tm, 1), 0) % seq_len
        first, last = pos == 0, pos == seq_len - 1
    for cc in range(D_MODEL // tn):
        cols = slice(cc * tn, (cc + 1) * tn)
        u = _dot(h_ref[...], w_ref[:, cols])
        if seq_len:
            prev = jnp.where(first, 0.0, pltpu.roll(u, 1, 0))
            nxt = jnp.where(last, 0.0, pltpu.roll(u, tm - 1, 0))
            u = prev * cw_ref[0:1, cols] + u * cw_ref[1:2, cols] + nxt * cw_ref[2:3, cols] + cb_ref[:, cols]
        o_ref[:, cols] = u.astype(o_ref.dtype)


def _proj(x, mods, crow, w, layer, widx, *, tm, tn, out_dtype, conv=None, seq_len=0):
    m = x.shape[0]
    in_specs = [
        pl.BlockSpec((tm, D_MODEL), lambda i, g: (i, 0)),
        pl.BlockSpec((None, None, N_MOD, D_MODEL), lambda i, g: (layer, crow(i * tm), 0, 0)),
        pl.BlockSpec((None, D_MODEL, D_MODEL), lambda i, g: (widx, 0, g)),
    ]
    args = [x, mods, w]
    if conv is not None:
        cw, cb = conv
        in_specs += [
            pl.BlockSpec((None, 3, D_MODEL), lambda i, g: (widx, 0, g)),
            pl.BlockSpec((None, 1, D_MODEL), lambda i, g: (widx, 0, g)),
        ]
        args += [cw, cb.reshape(cb.shape[0], 1, cb.shape[1])]
    return pl.pallas_call(
        functools.partial(_proj_kernel, seq_len=seq_len, tn=tn),
        grid=(m // tm, 3),
        in_specs=in_specs,
        out_specs=pl.BlockSpec((None, tm, D_MODEL), lambda i, g: (g, i, 0)),
        out_shape=jax.ShapeDtypeStruct((3, m, D_MODEL), out_dtype),
        scratch_shapes=[pltpu.VMEM((tm, D_MODEL), BF16)],
        compiler_params=_cparams("arbitrary", "arbitrary"),
        name="proj_conv" if seq_len else "proj",
    )(*args)


def _lane_tiles(a):
    return [a[:, i:i + LANES] for i in range(0, a.shape[1], LANES)]


def _head_masks():
    lane = lax.broadcasted_iota(jnp.int32, (1, LANES), 1)
    return lane < HEAD_DIM, lane >= HEAD_DIM


def _attn_ctx_kernel(x_ref, mod_ref, wqkv_ref, wo_ref, g_ref, b_ref, o_ref, ko_ref, vo_ref,
                     q_scr, k_scr, v_scr, a_scr, *, seq, nseq):
    x = x_ref[...]
    h = (x * (1.0 + mod_ref[4:5, :]) + mod_ref[3:4, :]).astype(BF16)
    q_scr[...] = (_dot(h, wqkv_ref[:, :D_MODEL]) * HEAD_DIM ** -0.5).astype(BF16)
    k_scr[...] = _dot(h, wqkv_ref[:, D_MODEL:2 * D_MODEL])
    v_scr[...] = _dot(h, wqkv_ref[:, 2 * D_MODEL:])
    masks = _head_masks()
    for s in range(nseq):
        rows = slice(s * seq, (s + 1) * seq)
        ko_ref[s] = k_scr[rows, :].reshape(seq, N_HEADS, HEAD_DIM)
        vo_ref[s] = v_scr[rows, :].reshape(seq, N_HEADS, HEAD_DIM)
        for p in range(D_MODEL // LANES):
            sl = slice(p * LANES, (p + 1) * LANES)
            q2 = q_scr[rows, sl]
            k2 = k_scr[rows, sl].astype(BF16)
            v2 = v_scr[rows, sl].astype(BF16)
            o2 = jnp.zeros(q2.shape, F32)
            for hm in masks:
                kh = jnp.where(hm, k2, 0.0)
                vh = jnp.where(hm, v2, 0.0)
                sc = _dot_nt(q2, kh)
                e = jnp.exp(sc - jnp.max(sc, axis=-1, keepdims=True))
                inv = 1.0 / jnp.sum(e, axis=-1, keepdims=True)
                o2 = o2 + _dot(e.astype(BF16), vh) * inv
            a_scr[rows, sl] = o2.astype(BF16)
    y = ALPHA * x + mod_ref[5:6, :] * _dot(a_scr[...], wo_ref[...])
    o_ref[...] = _layer_norm(y, g_ref[...], b_ref[...])


def _attn_ctx(x, mods, crow, w_qkv, w_o, widx, ln_g, ln_b, layer, batch, seq, nseq=2):
    rows = nseq * seq
    cache_spec = pl.BlockSpec((nseq, None, seq, N_HEADS, HEAD_DIM), lambda i: (i, 0, 0, 0, 0))
    cache_shape = jax.ShapeDtypeStruct((batch, 1, seq, N_HEADS, HEAD_DIM), F32)
    return pl.pallas_call(
        functools.partial(_attn_ctx_kernel, seq=seq, nseq=nseq),
        grid=(batch // nseq,),
        in_specs=[
            pl.BlockSpec((rows, D_MODEL), lambda i: (i, 0)),
            pl.BlockSpec((None, None, N_MOD, D_MODEL), lambda i: (layer, crow(i * rows), 0, 0)),
            _resident((None, D_MODEL, 3 * D_MODEL), lambda i: (widx, 0, 0)),
            _resident((None, D_MODEL, D_MODEL), lambda i: (widx, 0, 0)),
            pl.BlockSpec((None, None, 1, D_MODEL), lambda i: (layer, 1, 0, 0)),
            pl.BlockSpec((None, None, 1, D_MODEL), lambda i: (layer, 1, 0, 0)),
        ],
        out_specs=[pl.BlockSpec((rows, D_MODEL), lambda i: (i, 0)), cache_spec, cache_spec],
        out_shape=[jax.ShapeDtypeStruct((batch * seq, D_MODEL), F32), cache_shape, cache_shape],
        scratch_shapes=[pltpu.VMEM((rows, D_MODEL), BF16), pltpu.VMEM((rows, D_MODEL), F32),
                        pltpu.VMEM((rows, D_MODEL), F32), pltpu.VMEM((rows, D_MODEL), BF16)],
        compiler_params=_cparams("arbitrary"),
        name="attn_ctx",
    )(x, mods, w_qkv, w_o, ln_g, ln_b)


def _nbr_geometry(rows):
    kr = min(WIN_ROWS, rows)
    blocks = []
    for blk in range(rows // Q_ROWS):
        r0 = blk * Q_ROWS
        lo = min(max(r0 - kr // 2, 0), rows - kr)
        key_start = min(lo, rows - K_ROWS)
        bands = []
        for r in range(r0, r0 + Q_ROWS):
            rs = min(max(r - kr // 2, 0), rows - kr)
            a = rs - key_start
            band_start = min(a - a % 2, K_ROWS - BAND_ROWS)
            off = a - band_start
            assert 0 <= band_start and 0 <= off and off + kr <= BAND_ROWS
            bands.append((band_start, off, rs - r + WIN_ROWS - 1))
        blocks.append((key_start, bands))
    return kr, blocks


def _nbr_bias_rows(rpb):
    pad = GRID_W - (2 * WIN_COLS - 1)
    row = jnp.pad(rpb, ((0, 0), (0, 0), (0, pad)))
    return jnp.concatenate([row[:, :-1], row[:, 1:]], axis=-1)


def _attn_nbr_kernel(q_ref, k_ref, v_ref, kc_ref, vc_ref, rp_ref, o_ref,
                     s_ref, sc_ref, p_ref, pc_ref, inv_ref, tab_ref, *, blocks, kr):
    masks = _head_masks()
    left = masks[0]
    qn = Q_ROWS * GRID_W
    kn = K_ROWS * GRID_W
    qcol = lax.broadcasted_iota(jnp.int32, (GRID_W, LANES), 0)
    kcol = lax.broadcasted_iota(jnp.int32, (GRID_W, LANES), 1) % GRID_W
    cstart = jnp.clip(qcol - WIN_COLS // 2, 0, GRID_W - WIN_COLS)
    col_ok = (kcol >= cstart) & (kcol < cstart + WIN_COLS)
    for hh in range(2):
        for d in range(2 * WIN_ROWS - 2):
            row = jnp.broadcast_to(rp_ref[hh, d:d + 1, :], (GRID_W, LANES))
            toep = pltpu.roll(row, LANES - (WIN_COLS - 1), 1, stride=1, stride_axis=0)
            tab_ref[hh, d] = jnp.where(col_ok, toep, NEG)
    kc, vc = kc_ref[...].astype(BF16), vc_ref[...].astype(BF16)
    ctx_kv = [(jnp.where(hm, kc, 0.0), jnp.where(hm, vc, 0.0)) for hm in masks]
    for blk, (key_start, bands) in enumerate(blocks):
        q2 = (q_ref[blk * qn:(blk + 1) * qn, :] * HEAD_DIM ** -0.5).astype(BF16)
        k2 = k_ref[key_start * GRID_W:key_start * GRID_W + kn, :]
        v2 = v_ref[key_start * GRID_W:key_start * GRID_W + kn, :]
        o2 = jnp.zeros(q2.shape, F32)
        p_ref[...] = jnp.zeros_like(p_ref)
        for hh, hm in enumerate(masks):
            kh = jnp.where(hm, k2, 0.0)
            vh = jnp.where(hm, v2, 0.0)
            kch, vch = ctx_kv[hh]
            s_ref[...] = _dot_nt(q2, kh)
            sc_ref[...] = _dot_nt(q2, kch)
            for rl, (band_start, off, drow) in enumerate(bands):
                rsl = slice(rl * GRID_W, (rl + 1) * GRID_W)
                tiles = []
                for j in range(0, BAND_ROWS, 2):
                    ok0, ok1 = off <= j < off + kr, off <= j + 1 < off + kr
                    if not (ok0 or ok1):
                        continue
                    bias = tab_ref[hh, drow - off + j]
                    if not ok0:
                        bias = jnp.where(left, NEG, bias)
                    if not ok1:
                        bias = jnp.where(left, bias, NEG)
                    csl = slice((band_start + j) * GRID_W, (band_start + j + 2) * GRID_W)
                    tiles.append((csl, s_ref[rsl, csl] + bias))
                sc = sc_ref[rsl, :]
                top = functools.reduce(jnp.maximum, _lane_tiles(sc) + [sb for _, sb in tiles])
                mx = jnp.max(top, axis=-1, keepdims=True)
                ec = jnp.exp(sc - mx)
                tot = functools.reduce(jnp.add, _lane_tiles(ec))
                for csl, sb in tiles:
                    eb = jnp.exp(sb - mx)
                    tot = tot + eb
                    p_ref[rsl, csl] = eb.astype(BF16)
                inv_ref[rsl, :] = 1.0 / jnp.sum(tot, axis=-1, keepdims=True)
                pc_ref[rsl, :] = ec.astype(BF16)
            o2 = o2 + (_dot(p_ref[...], vh) + _dot(pc_ref[...], vch)) * inv_ref[...]
        o_ref[blk * qn:(blk + 1) * qn, :] = o2.astype(o_ref.dtype)


def _attn_nbr(qkv, kc, vc, rpb, batch, seq):
    rows = seq // GRID_W
    kr, blocks = _nbr_geometry(rows)
    n_pairs = D_MODEL // LANES
    n_d = 2 * WIN_ROWS - 2
    rp = _nbr_bias_rows(rpb).reshape(n_pairs, 2, n_d, LANES)
    s_ctx = kc.shape[1]
    qspec = lambda g: pl.BlockSpec((None, seq, LANES), lambda b, p: (g, b, p))
    cspec = pl.BlockSpec((None, s_ctx, LANES), lambda b, p: (b, 0, p))
    qn, kn = Q_ROWS * GRID_W, K_ROWS * GRID_W
    return pl.pallas_call(
        functools.partial(_attn_nbr_kernel, blocks=blocks, kr=kr),
        grid=(batch, n_pairs),
        in_specs=[qspec(0), qspec(1), qspec(2), cspec, cspec,
                  pl.BlockSpec((None, 2, n_d, LANES), lambda b, p: (p, 0, 0, 0))],
        out_specs=pl.BlockSpec((seq, LANES), lambda b, p: (b, p)),
        out_shape=jax.ShapeDtypeStruct((batch * seq, D_MODEL), BF16),
        scratch_shapes=[pltpu.VMEM((qn, kn), F32), pltpu.VMEM((qn, s_ctx), F32),
                        pltpu.VMEM((qn, kn), BF16), pltpu.VMEM((qn, s_ctx), BF16),
                        pltpu.VMEM((qn, 1), F32), pltpu.VMEM((2, n_d, GRID_W, LANES), F32)],
        compiler_params=_cparams("arbitrary", "arbitrary"),
        name="attn_nbr",
    )(qkv, qkv, qkv, kc, vc, rp)


def _dft_tables(seq_len, tq, tr):
    n = 2 * seq_len
    f = np.arange(seq_len, dtype=np.float64)[:, None] + 0.5
    s = np.arange(seq_len, dtype=np.float64)[None, :]
    ang = 2.0 * np.pi * f * s / n
    re = np.cos(ang).reshape(seq_len // tq, tq, seq_len)
    im = (-np.sin(ang)).reshape(seq_len // tq, tq, seq_len)
    fwd = np.concatenate([re, im], axis=1).astype(np.float32)
    bwd = fwd.reshape(2 * seq_len, seq_len).T
    return fwd, np.ascontiguousarray(bwd).reshape(seq_len // tr, tr, 2 * seq_len)


def _split_spectrum_table(seq_len, tq):
    half = seq_len // 2
    f = np.arange(half, dtype=np.float64)[:, None] + 0.5
    s = np.arange(seq_len, dtype=np.float64)[None, :]
    lo = np.pi * f * s / seq_len
    hi = np.pi * (seq_len - f) * s / seq_len
    parts = [np.cos(lo), -np.sin(lo), np.cos(hi), np.sin(hi)]
    return np.concatenate([p.reshape(half // tq, tq, seq_len) for p in parts], axis=1).astype(np.float32)


def _twiddles(seq_len):
    th = np.pi * (np.arange(seq_len // 2, dtype=np.float64) + 0.5) / seq_len
    return np.stack([np.cos(th), np.sin(th)])[:, :, None].repeat(LANES, axis=2).astype(np.float32)


def _deinterleave_matrix():
    sel = np.zeros((SPLIT_ROWS, SPLIT_ROWS), np.float32)
    i = np.arange(SPLIT_ROWS // 2)
    sel[i, 2 * i] = 1.0
    sel[SPLIT_ROWS // 2 + i, 2 * i + 1] = 1.0
    return sel


def _filter_features(seq_len):
    pos = np.arange(seq_len, dtype=np.float64)[:, None]
    t = pos / max(seq_len - 1, 1)
    bands = np.linspace(1e-4, HY_BANDS - 1, HY_BANDS)
    ang = 2.0 * np.pi * pos / seq_len * bands
    z = np.concatenate([t, np.cos(ang), -np.sin(ang)], axis=-1)
    zp = np.zeros((seq_len, HY_FILTER_W), np.float64)
    zp[:, :HY_EMB] = z
    deltas = np.abs(np.linspace(math.log(HY_TARGET) / HY_SLOW_DECAY, math.log(HY_TARGET) / HY_FAST_DECAY, D_MODEL))
    return zp.astype(np.float32), t.astype(np.float32), deltas.astype(np.float32)[None, :]


def _filter_kernel(z_ref, t_ref, dl_ref, w1_ref, b1_ref, f1_ref, w2_ref, b2_ref, f2_ref, w3_ref, b3_ref, o_ref,
                   hid_ref, decay_ref):
    hp = lax.Precision.HIGHEST

    @pl.when(pl.program_id(0) == 0)
    def _():
        hid = jnp.sin(f1_ref[...] * (jnp.dot(z_ref[...], w1_ref[...], precision=hp, preferred_element_type=F32) + b1_ref[...]))
        hid_ref[...] = jnp.sin(f2_ref[...] * (jnp.dot(hid, w2_ref[...], precision=hp, preferred_element_type=F32) + b2_ref[...]))
        decay_ref[...] = jnp.exp(-t_ref[...] * dl_ref[...])

    hid, w3 = hid_ref[...], w3_ref[...]
    hid_hi, w3_hi = hid.astype(BF16), w3.astype(BF16)
    hid_lo = (hid - hid_hi.astype(F32)).astype(BF16)
    w3_lo = (w3 - w3_hi.astype(F32)).astype(BF16)
    filt = _dot(hid_hi, w3_hi) + (_dot(hid_lo, w3_hi) + _dot(hid_hi, w3_lo)) + b3_ref[...]
    decay = decay_ref[...]
    hf = filt[:, :D_MODEL] * decay
    hb = filt[:, D_MODEL:] * decay
    row = lax.broadcasted_iota(jnp.int32, (hb.shape[0], 1), 0)
    hb = jnp.where(row == 0, 0.0, hb)
    o_ref[:, :D_MODEL] = (hf + hb).astype(o_ref.dtype)
    o_ref[:, D_MODEL:] = (hf - hb).astype(o_ref.dtype)


def _filters(seq_len, w1, b1, f1, w2, b2, f2, w3, b3):
    z, t, deltas = _filter_features(seq_len)
    w1p = jnp.zeros((HY_FILTER_W, HY_FILTER_W), F32).at[:HY_EMB].set(w1)
    full = lambda shape: pl.BlockSpec(shape, lambda o: (0,) * len(shape))
    row = lambda a: a.reshape(1, -1)
    return pl.pallas_call(
        _filter_kernel,
        grid=(2,),
        in_specs=[full((seq_len, HY_FILTER_W)), full((seq_len, 1)), full((1, D_MODEL)),
                  full((HY_FILTER_W, HY_FILTER_W)), full((1, HY_FILTER_W)), full((1, HY_FILTER_W)),
                  full((HY_FILTER_W, HY_FILTER_W)), full((1, HY_FILTER_W)), full((1, HY_FILTER_W)),
                  pl.BlockSpec((HY_FILTER_W, 2 * D_MODEL), lambda o: (0, o)),
                  pl.BlockSpec((1, 2 * D_MODEL), lambda o: (0, o))],
        out_specs=pl.BlockSpec((None, seq_len, 2 * D_MODEL), lambda o: (o, 0, 0)),
        out_shape=jax.ShapeDtypeStruct((2, seq_len, 2 * D_MODEL), BF16),
        scratch_shapes=[pltpu.VMEM((seq_len, HY_FILTER_W), F32), pltpu.VMEM((seq_len, D_MODEL), F32)],
        compiler_params=_cparams("arbitrary"),
        name="hyena_filters",
    )(jnp.asarray(z),jnp.asarray(t), jnp.asarray(deltas), w1p, row(b1), row(f1), w2, row(b2), row(f2), w3, row(b3))


def _spectrum_kernel(g_ref, hs_ref, hd_ref, o_ref, *, tq, scale):
    for part in range(g_ref.shape[0] // tq):
        rows = slice(part * tq, (part + 1) * tq)
        h_ref = hs_ref if part % 2 == 0 else hd_ref
        o_ref[rows, :] = scale * _dot(g_ref[rows, :], h_ref[...])


def _spectrum(table, filt, seq_len, tq, tc):
    nt, rows, _ = table.shape
    ncb = D_MODEL // tc
    return pl.pallas_call(
        functools.partial(_spectrum_kernel, tq=tq, scale=1.0 / seq_len),
        grid=(nt, 2, ncb),
        in_specs=[pl.BlockSpec((None, rows, seq_len), lambda f, o, c: (f, 0, 0)),
                  pl.BlockSpec((None, seq_len, tc), lambda f, o, c: (o, 0, c)),
                  pl.BlockSpec((None, seq_len, tc), lambda f, o, c: (o, 0, c + ncb))],
        out_specs=pl.BlockSpec((None, None, rows, tc), lambda f, o, c: (o, f, 0, c)),
        out_shape=jax.ShapeDtypeStruct((2, nt, rows, D_MODEL), F32),
        compiler_params=_cparams("arbitrary", "arbitrary", "arbitrary"),
        name="hyena_spectrum",
    )(table, filt, filt)


def _hyena_kernel(v_ref, gate_ref, g_ref, gt_ref, k_ref, tw_ref, sel_ref, selt_ref, b_ref, o_ref,
                  ze_ref, zo_ref, pu_ref, pv_ref, *, tq, nf, tr):
    o = pl.program_id(2)
    st = pl.program_id(3)
    half = ze_ref.shape[0]
    tc = ze_ref.shape[1]
    hs = SPLIT_ROWS // 2

    @pl.when((o == 0) & (st == 0))
    def _():
        for j in range(half // hs):
            sp = _dot(sel_ref[...], v_ref[j * SPLIT_ROWS:(j + 1) * SPLIT_ROWS, :])
            ze_ref[j * hs:(j + 1) * hs, :] = sp[:hs].astype(BF16)
            zo_ref[j * hs:(j + 1) * hs, :] = sp[hs:].astype(BF16)

    @pl.when(st < nf)
    def _():
        base = pl.multiple_of(st * 2 * tq, 2 * tq)
        c, s = tw_ref[0], tw_ref[1]
        group = 2 * LANES
        for lt in range(tc // LANES):
            cols = slice(lt * LANES, (lt + 1) * LANES)
            if lt % 2 == 0:
                gcols = slice(lt * LANES, lt * LANES + group)
                fa = _dot(g_ref[...], ze_ref[:, gcols])
                fb = _dot(g_ref[...], zo_ref[:, gcols])
            loc = slice((lt % 2) * LANES, (lt % 2 + 1) * LANES)
            ar, ai, br, bi = fa[:tq, loc], fa[tq:, loc], fb[:tq, loc], fb[tq:, loc]
            tr_, ti = c * br + s * bi, c * bi - s * br
            s1r, s1i, s2r, s2i = ar + tr_, ai + ti, ar - tr_, ai - ti
            k1r, k1i = k_ref[0:tq, cols], k_ref[tq:2 * tq, cols]
            k2r, k2i = k_ref[2 * tq:3 * tq, cols], k_ref[3 * tq:, cols]
            p1r, p1i = s1r * k1r - s1i * k1i, s1r * k1i + s1i * k1r
            p2r, p2i = s2r * k2r - s2i * k2i, s2r * k2i + s2i * k2r
            wr, wi = p1r - p2r, p1i - p2i
            pu_ref[pl.ds(base, tq), cols] = (p1r + p2r).astype(BF16)
            pu_ref[pl.ds(base + tq, tq), cols] = (p1i + p2i).astype(BF16)
            pv_ref[pl.ds(base, tq), cols] = (c * wr - s * wi).astype(BF16)
            pv_ref[pl.ds(base + tq, tq), cols] = (c * wi + s * wr).astype(BF16)

    @pl.when(st >= nf)
    def _():
        r0 = pl.multiple_of((st - nf) * tr, tr)
        rows = pl.ds(r0, tr)
        ye = _dot(gt_ref[...], pu_ref[...]) + ze_ref[rows, :].astype(F32) * b_ref[...]
        yo = _dot(gt_ref[...], pv_ref[...]) + zo_ref[rows, :].astype(F32) * b_ref[...]
        block = lambda j: pl.ds(pl.multiple_of(2 * r0 + j * SPLIT_ROWS, SPLIT_ROWS), SPLIT_ROWS)
        gates = [_dot(sel_ref[...], gate_ref[block(j), :]) for j in range(tr // hs)]
        ge = ye * jnp.concatenate([g[:hs] for g in gates], axis=0)
        go = yo * jnp.concatenate([g[hs:] for g in gates], axis=0)

        @pl.when(o == 0)
        def _():
            ze_ref[rows, :] = ge.astype(BF16)
            zo_ref[rows, :] = go.astype(BF16)

        @pl.when(o == 1)
        def _():
            for j in range(tr // hs):
                blk = jnp.concatenate([ge[j * hs:(j + 1) * hs], go[j * hs:(j + 1) * hs]], axis=0).astype(BF16)
                o_ref[block(j), :] = _dot(selt_ref[...], blk).astype(o_ref.dtype)


def _hyena_core(u, fwd_h, bwd_h, spec, tw, sel, bias, batch, seq_len, tq, tc):
    half = seq_len // 2
    nf = half // tq
    nr, tr, _ = bwd_h.shape
    ncb = D_MODEL // tc
    fidx = lambda st: jnp.minimum(st, nf - 1)
    whole = lambda shape: pl.BlockSpec(shape, lambda b, c, o, st: (0,) * len(shape))
    return pl.pallas_call(
        functools.partial(_hyena_kernel, tq=tq, nf=nf, tr=tr),
        grid=(batch, ncb, 2, nf + nr),
        in_specs=[pl.BlockSpec((None, seq_len, tc), lambda b, c, o, st: (0, b, c)),
                  pl.BlockSpec((None, seq_len, tc), lambda b, c, o, st: (1 + o, b, c)),
                  pl.BlockSpec((None, 2 * tq, half), lambda b, c, o, st: (fidx(st), 0, 0)),
                  pl.BlockSpec((None, tr, seq_len), lambda b, c, o, st: (jnp.maximum(st - nf, 0), 0, 0)),
                  pl.BlockSpec((None, None, 4 * tq, tc), lambda b, c, o, st: (o, fidx(st), 0, c)),
                  pl.BlockSpec((2, tq, LANES), lambda b, c, o, st: (0, fidx(st), 0)),
                  whole((SPLIT_ROWS, SPLIT_ROWS)), whole((SPLIT_ROWS, SPLIT_ROWS)),
                  pl.BlockSpec((None, 1, tc), lambda b, c, o, st: (o, 0, c))],
        out_specs=pl.BlockSpec((seq_len, tc), lambda b, c, o, st: (b, c)),
        out_shape=jax.ShapeDtypeStruct((batch * seq_len, D_MODEL), BF16),
        scratch_shapes=[pltpu.VMEM((half, tc), BF16), pltpu.VMEM((half, tc), BF16),
                        pltpu.VMEM((seq_len, tc), BF16), pltpu.VMEM((seq_len, tc), BF16)],
        compiler_params=_cparams("arbitrary", "arbitrary", "arbitrary", "arbitrary"),
        name="hyena_conv",
    )(u, u, fwd_h, bwd_h, spec, tw, sel, sel.T, bias.reshape(2, 1, D_MODEL))


def _hyena_layer_kernel(x_ref, mod_ref, win_ref, cw_ref, cb_ref, g_ref, gt_ref, k_ref, hb_ref, wout_ref,
                        lg_ref, lb_ref, o_ref, u_scr, *, seq_len, nseq, tn):
    x = x_ref[...]
    h = (x * (1.0 + mod_ref[4:5, :]) + mod_ref[3:4, :]).astype(BF16)
    rows = nseq * seq_len
    pos = lax.broadcasted_iota(jnp.int32, (rows, 1), 0) % seq_len
    first, last = pos == 0, pos == seq_len - 1
    for g in range(3):
        for cc in range(D_MODEL // tn):
            cols = slice(g * D_MODEL + cc * tn, g * D_MODEL + (cc + 1) * tn)
            u = _dot(h, win_ref[:, cols])
            prev = jnp.where(first, 0.0, pltpu.roll(u, 1, 0))
            nxt = jnp.where(last, 0.0, pltpu.roll(u, rows - 1, 0))
            u = prev * cw_ref[0:1, cols] + u * cw_ref[1:2, cols] + nxt * cw_ref[2:3, cols] + cb_ref[:, cols]
            u_scr[g, :, cc * tn:(cc + 1) * tn] = u.astype(BF16)
    for o in range(2):
        kr, ki = k_ref[o, :seq_len, :], k_ref[o, seq_len:, :]
        for s in range(nseq):
            r = slice(s * seq_len, (s + 1) * seq_len)
            z = u_scr[0, r, :]
            zf = _dot(g_ref[...], z)
            zr, zi = zf[:seq_len], zf[seq_len:]
            prod = jnp.concatenate([zr * kr - zi * ki, zr * ki + zi * kr], axis=0).astype(BF16)
            y = _dot(gt_ref[...], prod) + z.astype(F32) * hb_ref[o]
            u_scr[0, r, :] = (y * u_scr[1 + o, r, :].astype(F32)).astype(BF16)
    y = ALPHA * x + mod_ref[5:6, :] * _dot(u_scr[0], wout_ref[...])
    o_ref[...] = _layer_norm(y, lg_ref[...], lb_ref[...])


def _hyena_layer(x, mods, crow, w_in, w_out, widx, conv_w, conv_b, fwd, bwd, spec, bias, ln_g, ln_b, layer,
                 seq_len, nseq, tn=512):
    m = x.shape[0]
    rows = nseq * seq_len
    assert fwd.shape[0] == 1
    return pl.pallas_call(
        functools.partial(_hyena_layer_kernel, seq_len=seq_len, nseq=nseq, tn=tn),
        grid=(m // rows,),
        in_specs=[
            pl.BlockSpec((rows, D_MODEL), lambda i: (i, 0)),
            pl.BlockSpec((None, None, N_MOD, D_MODEL), lambda i: (layer, crow(i * rows), 0, 0)),
            _resident((None, D_MODEL, 3 * D_MODEL), lambda i: (widx, 0, 0)),
            pl.BlockSpec((None, 3, 3 * D_MODEL), lambda i: (widx, 0, 0)),
            pl.BlockSpec((None, 1, 3 * D_MODEL), lambda i: (widx, 0, 0)),
            pl.BlockSpec((None, 2 * seq_len, seq_len), lambda i: (0, 0, 0)),
            pl.BlockSpec((None, seq_len, 2 * seq_len), lambda i: (0, 0, 0)),
            _resident((2, None, 2 * seq_len, D_MODEL), lambda i: (0, 0, 0, 0)),
            pl.BlockSpec((2, 1, D_MODEL), lambda i: (0, 0, 0)),
            _resident((None, D_MODEL, D_MODEL), lambda i: (widx, 0, 0)),
            pl.BlockSpec((None, None, 1, D_MODEL), lambda i: (layer, 1, 0, 0)),
            pl.BlockSpec((None, None, 1, D_MODEL), lambda i: (layer, 1, 0, 0)),
        ],
        out_specs=pl.BlockSpec((rows, D_MODEL), lambda i: (i, 0)),
        out_shape=jax.ShapeDtypeStruct((m, D_MODEL), F32),
        scratch_shapes=[pltpu.VMEM((3, rows, D_MODEL), BF16)],
        compiler_params=_cparams("arbitrary"),
        name="hyena_layer",
    )(x, mods, w_in, conv_w, conv_b.reshape(conv_b.shape[0], 1, conv_b.shape[1]), fwd, bwd, spec,
      bias.reshape(2, 1, D_MODEL), w_out, ln_g, ln_b)


def _hyena_tiles(seq_len, batch):
    tq = tr = min(seq_len, 512)
    tc = D_MODEL if seq_len <= 256 else 512
    nseq = math.gcd(batch, max(1, 1024 // seq_len))
    return tq, tr, tc, nseq


def kernel(x_prompt, x_sample, cache_k, cache_v, c, c_ctx, mod_w, mod_b, ln_g, ln_b, ffn_w_in, ffn_w_out,
           attn_w_qkv, attn_w_o, attn_rpb, hy_w_in, hy_conv_w, hy_conv_b, hy_f_w1, hy_f_b1, hy_f_freq1,
           hy_f_w2, hy_f_b2, hy_f_freq2, hy_f_w3, hy_f_b3, hy_bias, hy_w_out):
    batch, seq, _ = x_prompt.shape
    dec_batch, dec_seq, _ = x_sample.shape
    assert dec_batch + 1 <= COND_ROWS

    conds = jnp.zeros((COND_ROWS, D_MODEL), F32).at[0].set(c_ctx).at[1:1 + dec_batch].set(c)
    mods = _mods(conds, mod_w, mod_b).reshape(DEPTH, COND_ROWS, N_MOD, D_MODEL)
    ln_g4 = ln_g.reshape(DEPTH, 3, 1, D_MODEL)
    ln_b4 = ln_b.reshape(DEPTH, 3, 1, D_MODEL)

    streams = {
        "ctx": dict(x=x_prompt.reshape(batch * seq, D_MODEL), batch=batch, seq=seq, crow=lambda t: 0),
        "lat": dict(x=x_sample.reshape(dec_batch * dec_seq, D_MODEL), batch=dec_batch, seq=dec_seq,
                    crow=lambda t: 1 + t // dec_seq),
    }
    as_bf16 = lambda table: jnp.asarray(table).astype(BF16)
    hyena_consts = {}
    for name, st in streams.items():
        sl = st["seq"]
        tq, tr, tc, nseq = _hyena_tiles(sl, st["batch"])
        if sl == tq:
            fwd, bwd = _dft_tables(sl, tq, tr)
            hyena_consts[name] = dict(tq=tq, tc=tc, nseq=nseq, fwd=as_bf16(fwd), bwd=as_bf16(bwd), ktab=as_bf16(fwd))
        else:
            fwd, bwd = _dft_tables(sl // 2, tq, tr)
            hyena_consts[name] = dict(tq=tq, tc=tc, nseq=nseq, fwd=as_bf16(fwd), bwd=as_bf16(bwd),
                                      ktab=as_bf16(_split_spectrum_table(sl, tq)),
                                      tw=jnp.asarray(_twiddles(sl)), sel=as_bf16(_deinterleave_matrix()))

    attn_w_qkv, attn_w_o, hy_w_in, hy_w_out = (w.astype(BF16) for w in (attn_w_qkv, attn_w_o, hy_w_in, hy_w_out))
    ffn_order = [(layer, which) for layer in range(DEPTH) for which in (0, 1)]
    ffn_bf16 = {ffn_order[0]: (ffn_w_in[0, 0].astype(BF16), ffn_w_out[0, 0].astype(BF16))}
    names = list(streams)
    crows = [streams[name]["crow"] for name in names]

    def ffn(xs, layer, which, mixer_outs, w_mix=None):
        key = (layer, which)
        later = ffn_order[ffn_order.index(key) + 1:]
        cast_next = None
        if later and _ffn_can_cast(xs[names[0]].shape[0]):
            cast_next = (ffn_w_in, ffn_w_out, *later[0])
        elif later:
            ffn_bf16[later[0]] = (ffn_w_in[later[0]].astype(BF16), ffn_w_out[later[0]].astype(BF16))
        ys, cast = _ffn([xs[name] for name in names], mods, crows, *ffn_bf16[key], ln_g4, ln_b4, layer, which,
                        [mixer_outs.get(name) for name in names], w_mix=w_mix, cast_next=cast_next)
        if cast_next is not None:
            ffn_bf16[later[0]] = cast
        return dict(zip(names, ys))

    new_k, new_v = [], []
    xs = {name: st["x"] for name, st in streams.items()}
    for layer in range(DEPTH):
        j = layer // 2
        xs = ffn(xs, layer, 0, {})
        mixer_outs = {}
        for name, st in streams.items():
            x, nb, sl, crow = xs[name], st["batch"], st["seq"], st["crow"]
            if layer % 2 == 0:
                w_mix = (attn_w_o, j)
                if name == "ctx":
                    xs[name], k_new, v_new = _attn_ctx(x, mods, crow, attn_w_qkv, attn_w_o, j, ln_g4, ln_b4, layer,
                                                       nb, sl)
                    new_k.append(k_new)
                    new_v.append(v_new)
                else:
                    qkv = _proj(x, mods, crow, attn_w_qkv, layer, j, tm=sl, tn=1024, out_dtype=BF16)
                    kc = cache_k[:, j].reshape(nb, -1, D_MODEL)
                    vc = cache_v[:, j].reshape(nb, -1, D_MODEL)
                    mixer_outs[name] = _attn_nbr(qkv, kc, vc, attn_rpb[j], nb, sl)
            else:
                w_mix = (hy_w_out, j)
                hc = hyena_consts[name]
                tq, tc = hc["tq"], hc["tc"]
                filt = _filters(sl, hy_f_w1[j], hy_f_b1[j], hy_f_freq1[j], hy_f_w2[j], hy_f_b2[j], hy_f_freq2[j],
                                hy_f_w3[j], hy_f_b3[j])
                spec = _spectrum(hc["ktab"], filt, sl, tq, tc)
                if sl == tq:
                    xs[name] = _hyena_layer(x, mods, crow, hy_w_in, hy_w_out, j, hy_conv_w, hy_conv_b, hc["fwd"],
                                            hc["bwd"], spec, hy_bias[j], ln_g4, ln_b4, layer, sl, hc["nseq"])
                else:
                    u = _proj(x, mods, crow, hy_w_in, layer, j, tm=max(sl, 1024), tn=512, out_dtype=BF16,
                              conv=(hy_conv_w, hy_conv_b), seq_len=sl)
                    mixer_outs[name] = _hyena_core(u, hc["fwd"], hc["bwd"], spec, hc["tw"], hc["sel"], hy_bias[j],
                                                   nb, sl, tq, tc)
        xs = ffn(xs, layer, 1, mixer_outs, w_mix=w_mix)

    y_prompt = xs["ctx"].reshape(batch, seq, D_MODEL)
    y_sample = xs["lat"].reshape(dec_batch, dec_seq, D_MODEL)
    return (y_prompt, y_sample, jnp.concatenate(new_k, axis=1), jnp.concatenate(new_v, axis=1))
```

```python
import functools
import math

import numpy as np
import jax
import jax.numpy as jnp
from jax import lax
from jax.experimental import pallas as pl
from jax.experimental.pallas import tpu as pltpu

D_MODEL = 1024
DEPTH = 2
GRID_W = 64
N_HEADS = 16
HEAD_DIM = D_MODEL // N_HEADS
WIN_ROWS = 8
WIN_COLS = 16
D_FF = 2816
N_MOD = 9
HY_BANDS = 16
HY_EMB = 1 + 2 * HY_BANDS
HY_FILTER_W = 64
HY_FAST_DECAY = 0.3
HY_SLOW_DECAY = 1.5
HY_TARGET = 1e-2
ALPHA = (2 * DEPTH) ** 0.25
LN_EPS = 1e-5

F32 = jnp.float32
BF16 = jnp.bfloat16
NEG = -1e30
LANES = 128
VMEM_LIMIT = 56 * 1024 * 1024
COND_ROWS = 8

Q_ROWS = 8
K_ROWS = 16
BAND_ROWS = 10
SPLIT_ROWS = 256


def _cparams(*sem):
    return pltpu.CompilerParams(dimension_semantics=sem, vmem_limit_bytes=VMEM_LIMIT)


def _dot(a, b):
    return jnp.dot(a, b, preferred_element_type=F32)


def _dot_nt(a, b):
    return lax.dot_general(a, b, (((1,), (1,)), ((), ())), preferred_element_type=F32)


def _layer_norm(y, g, b):
    mu = jnp.mean(y, axis=-1, keepdims=True)
    yc = y - mu
    var = jnp.mean(yc * yc, axis=-1, keepdims=True)
    return yc * lax.rsqrt(var + LN_EPS) * g + b


def _mods_kernel(c_ref, w_ref, b_ref, o_ref):
    c = c_ref[...]
    s = c * jax.nn.sigmoid(c)
    o_ref[...] = _dot(s.astype(BF16), w_ref[...].astype(BF16)) + b_ref[...]


def _mods(conds, mod_w, mod_b):
    nd = N_MOD * D_MODEL
    tn = nd // 4
    return pl.pallas_call(
        _mods_kernel,
        grid=(DEPTH, nd // tn),
        in_specs=[
            pl.BlockSpec((COND_ROWS, D_MODEL), lambda l, j: (0, 0)),
            pl.BlockSpec((None, D_MODEL, tn), lambda l, j: (l, 0, j)),
            pl.BlockSpec((None, 1, tn), lambda l, j: (l, 0, j)),
        ],
        out_specs=pl.BlockSpec((None, COND_ROWS, tn), lambda l, j: (l, 0, j)),
        out_shape=jax.ShapeDtypeStruct((DEPTH, COND_ROWS, nd), F32),
        compiler_params=_cparams("arbitrary", "arbitrary"),
        name="mods",
    )(conds, mod_w, mod_b.reshape(DEPTH, 1, nd))


def _ffn_kernel(*refs, mi, n_chunks, starts, mixers, cast_next):
    refs = list(refs)
    take = lambda n: [refs.pop(0) for _ in range(n)]
    stream_refs = [take(3 if has_mixer else 2) for has_mixer in mixers]
    wi_ref, wo_ref, g_ref, b_ref = take(4)
    if any(mixers):
        wmix_ref, gmix_ref, bmix_ref = take(3)
    if cast_next:
        wi_next_ref, wo_next_ref = take(2)
    out_refs = take(len(mixers))
    s = pl.program_id(0)

    def half_step(x_ref, mod_ref, a_ref, o_ref):
        x = x_ref[...]
        if a_ref is not None:
            x = _layer_norm(ALPHA * x + mod_ref[5:6, :] * _dot(a_ref[...], wmix_ref[...]), gmix_ref[...], bmix_ref[...])
        sh = mod_ref[3 * mi:3 * mi + 1, :]
        sc = mod_ref[3 * mi + 1:3 * mi + 2, :]
        gt = mod_ref[3 * mi + 2:3 * mi + 3, :]
        h = (x * (1.0 + sc) + sh).astype(BF16)
        tf = D_FF // n_chunks
        f = None
        for c in range(n_chunks):
            gate = _dot(h, wi_ref[:, c * tf:(c + 1) * tf])
            up = _dot(h, wi_ref[:, D_FF + c * tf:D_FF + (c + 1) * tf])
            a = (gate * jax.nn.sigmoid(gate) * up).astype(BF16)
            part = _dot(a, wo_ref[c * tf:(c + 1) * tf, :])
            f = part if f is None else f + part
        y = ALPHA * x + (0.5 * gt) * f
        o_ref[...] = _layer_norm(y, g_ref[...], b_ref[...])

    for k, (x_ref, mod_ref, *a_ref) in enumerate(stream_refs):
        @pl.when((s >= starts[k]) & (s < starts[k + 1]))
        def _(k=k, x_ref=x_ref, mod_ref=mod_ref, a_ref=a_ref):
            half_step(x_ref, mod_ref, a_ref[0] if a_ref else None, out_refs[k])
            if cast_next and k == 0:
                wi_cast_ref, wo_cast_ref = refs
                wi_cast_ref[...] = wi_next_ref[...].astype(BF16)
                wo_cast_ref[...] = wo_next_ref[...].astype(BF16)


def _resident(shape, index_map):
    return pl.BlockSpec(shape, index_map, pipeline_mode=pl.Buffered(1))


BF16_ROWS = 16


FFN_ROWS = 512


def _ffn_can_cast(m, tm=FFN_ROWS):
    steps = m // tm
    return D_MODEL % (steps * BF16_ROWS) == 0 and D_FF % (steps * BF16_ROWS) == 0


def _ffn(xs, mods, crows, w_in, w_out, ln_g, ln_b, layer, which, mixer_outs, w_mix=None, cast_next=None,
         tm=FFN_ROWS, n_chunks=11):
    mi = 0 if which == 0 else 2
    steps = [x.shape[0] // tm for x in xs]
    starts = [sum(steps[:k]) for k in range(len(xs) + 1)]
    mixers = tuple(a is not None for a in mixer_outs)
    ln_spec = lambda idx: pl.BlockSpec((None, None, 1, D_MODEL), lambda s: (layer, idx, 0, 0))

    def tile_of(k):
        return lambda s: jnp.clip(s - starts[k], 0, steps[k] - 1)

    in_specs, args, out_specs, out_shape = [], [], [], []
    for k, (x, crow, a) in enumerate(zip(xs, crows, mixer_outs)):
        tile = tile_of(k)
        row_spec = pl.BlockSpec((tm, D_MODEL), lambda s, tile=tile: (tile(s), 0))
        in_specs += [row_spec, pl.BlockSpec((None, None, N_MOD, D_MODEL),
                                            lambda s, tile=tile, crow=crow: (layer, crow(tile(s) * tm), 0, 0))]
        args += [x, mods]
        if a is not None:
            in_specs.append(row_spec)
            args.append(a)
        out_specs.append(row_spec)
        out_shape.append(jax.ShapeDtypeStruct(x.shape, F32))
    in_specs += [_resident((D_MODEL, 2 * D_FF), lambda s: (0, 0)), _resident((D_FF, D_MODEL), lambda s: (0, 0)),
                 ln_spec(mi), ln_spec(mi)]
    args += [w_in, w_out, ln_g, ln_b]
    if any(mixers):
        w, widx = w_mix
        in_specs += [_resident((None, D_MODEL, D_MODEL), lambda s: (widx, 0, 0)), ln_spec(1), ln_spec(1)]
        args += [w, ln_g, ln_b]
    if cast_next is not None:
        w_in_all, w_out_all, layer2, which2 = cast_next
        slab = tile_of(0)
        slab_in, slab_out = D_MODEL // steps[0], D_FF // steps[0]
        in_specs += [pl.BlockSpec((None, None, slab_in, 2 * D_FF), lambda s: (layer2, which2, slab(s), 0)),
                     pl.BlockSpec((None, None, slab_out, D_MODEL), lambda s: (layer2, which2, slab(s), 0))]
        args += [w_in_all, w_out_all]
        out_specs += [pl.BlockSpec((slab_in, 2 * D_FF), lambda s: (slab(s), 0)),
                      pl.BlockSpec((slab_out, D_MODEL), lambda s: (slab(s), 0))]
        out_shape += [jax.ShapeDtypeStruct((D_MODEL, 2 * D_FF), BF16), jax.ShapeDtypeStruct((D_FF, D_MODEL), BF16)]
    outs = pl.pallas_call(
        functools.partial(_ffn_kernel, mi=mi, n_chunks=n_chunks, starts=tuple(starts), mixers=mixers,
                          cast_next=cast_next is not None),
        grid=(starts[-1],),
        in_specs=in_specs,
        out_specs=out_specs,
        out_shape=out_shape,
        compiler_params=_cparams("arbitrary"),
        name="ffn" + ("_mix" if any(mixers) else "") + ("_cast" if cast_next is not None else ""),
    )(*args)
    return list(outs[:len(xs)]), tuple(outs[len(xs):])


def _proj_kernel(x_ref, mod_ref, w_ref, *rest, seq_len, tn):
    if seq_len:
        cw_ref, cb_ref, o_ref, h_ref = rest
    else:
        o_ref, h_ref = rest

    @pl.when(pl.program_id(1) == 0)
    def _():
        sh = mod_ref[3:4, :]
        sc = mod_ref[4:5, :]
        h_ref[...] = (x_ref[...] * (1.0 + sc) + sh).astype(BF16)

    tm = h_ref.shape[0]
    if seq_len:
        pos = lax.broadcasted_iota(jnp.int32, (tm, 1), 0) % seq_len
        first, last = pos == 0, pos == seq_len - 1
    for cc in range(D_MODEL // tn):
        cols = slice(cc * tn, (cc + 1) * tn)
        u = _dot(h_ref[...], w_ref[:, cols])
        if seq_len:
            prev = jnp.where(first, 0.0, pltpu.roll(u, 1, 0))
            nxt = jnp.where(last, 0.0, pltpu.roll(u, tm - 1, 0))
            u = prev * cw_ref[0:1, cols] + u * cw_ref[1:2, cols] + nxt * cw_ref[2:3, cols] + cb_ref[:, cols]
        o_ref[:, cols] = u.astype(o_ref.dtype)


def _proj(x, mods, crow, w, layer, widx, *, tm, tn, out_dtype, conv=None, seq_len=0):
    m = x.shape[0]
    in_specs = [
        pl.BlockSpec((tm, D_MODEL), lambda i, g: (i, 0)),
        pl.BlockSpec((None, None, N_MOD, D_MODEL), lambda i, g: (layer, crow(i * tm), 0, 0)),
        pl.BlockSpec((None, D_MODEL, D_MODEL), lambda i, g: (widx, 0, g)),
    ]
    args = [x, mods, w]
    if conv is not None:
        cw, cb = conv
        in_specs += [
            pl.BlockSpec((None, 3, D_MODEL), lambda i, g: (widx, 0, g)),
            pl.BlockSpec((None, 1, D_MODEL), lambda i, g: (widx, 0, g)),
        ]
        args += [cw, cb.reshape(cb.shape[0], 1, cb.shape[1])]
    return pl.pallas_call(
        functools.partial(_proj_kernel, seq_len=seq_len, tn=tn),
        grid=(m // tm, 3),
        in_specs=in_specs,
        out_specs=pl.BlockSpec((None, tm, D_MODEL), lambda i, g: (g, i, 0)),
        out_shape=jax.ShapeDtypeStruct((3, m, D_MODEL), out_dtype),
        scratch_shapes=[pltpu.VMEM((tm, D_MODEL), BF16)],
        compiler_params=_cparams("arbitrary", "arbitrary"),
        name="proj_conv" if seq_len else "proj",
    )(*args)


def _lane_tiles(a):
    return [a[:, i:i + LANES] for i in range(0, a.shape[1], LANES)]


def _head_masks():
    lane = lax.broadcasted_iota(jnp.int32, (1, LANES), 1)
    return lane < HEAD_DIM, lane >= HEAD_DIM


def _attn_ctx_kernel(x_ref, mod_ref, wqkv_ref, wo_ref, g_ref, b_ref, o_ref, ko_ref, vo_ref,
                     q_scr, k_scr, v_scr, a_scr, *, seq, nseq):
    x = x_ref[...]
    h = (x * (1.0 + mod_ref[4:5, :]) + mod_ref[3:4, :]).astype(BF16)
    q_scr[...] = (_dot(h, wqkv_ref[:, :D_MODEL]) * HEAD_DIM ** -0.5).astype(BF16)
    k_scr[...] = _dot(h, wqkv_ref[:, D_MODEL:2 * D_MODEL])
    v_scr[...] = _dot(h, wqkv_ref[:, 2 * D_MODEL:])
    masks = _head_masks()
    for s in range(nseq):
        rows = slice(s * seq, (s + 1) * seq)
        ko_ref[s] = k_scr[rows, :].reshape(seq, N_HEADS, HEAD_DIM)
        vo_ref[s] = v_scr[rows, :].reshape(seq, N_HEADS, HEAD_DIM)
        for p in range(D_MODEL // LANES):
            sl = slice(p * LANES, (p + 1) * LANES)
            q2 = q_scr[rows, sl]
            k2 = k_scr[rows, sl].astype(BF16)
            v2 = v_scr[rows, sl].astype(BF16)
            o2 = jnp.zeros(q2.shape, F32)
            for hm in masks:
                kh = jnp.where(hm, k2, 0.0)
                vh = jnp.where(hm, v2, 0.0)
                sc = _dot_nt(q2, kh)
                e = jnp.exp(sc - jnp.max(sc, axis=-1, keepdims=True))
                inv = 1.0 / jnp.sum(e, axis=-1, keepdims=True)
                o2 = o2 + _dot(e.astype(BF16), vh) * inv
            a_scr[rows, sl] = o2.astype(BF16)
    y = ALPHA * x + mod_ref[5:6, :] * _dot(a_scr[...], wo_ref[...])
    o_ref[...] = _layer_norm(y, g_ref[...], b_ref[...])


def _attn_ctx(x, mods, crow, w_qkv, w_o, widx, ln_g, ln_b, layer, batch, seq, nseq=2):
    rows = nseq * seq
    cache_spec = pl.BlockSpec((nseq, None, seq, N_HEADS, HEAD_DIM), lambda i: (i, 0, 0, 0, 0))
    cache_shape = jax.ShapeDtypeStruct((batch, 1, seq, N_HEADS, HEAD_DIM), F32)
    return pl.pallas_call(
        functools.partial(_attn_ctx_kernel, seq=seq, nseq=nseq),
        grid=(batch // nseq,),
        in_specs=[
            pl.BlockSpec((rows, D_MODEL), lambda i: (i, 0)),
            pl.BlockSpec((None, None, N_MOD, D_MODEL), lambda i: (layer, crow(i * rows), 0, 0)),
            _resident((None, D_MODEL, 3 * D_MODEL), lambda i: (widx, 0, 0)),
            _resident((None, D_MODEL, D_MODEL), lambda i: (widx, 0, 0)),
            pl.BlockSpec((None, None, 1, D_MODEL), lambda i: (layer, 1, 0, 0)),
            pl.BlockSpec((None, None, 1, D_MODEL), lambda i: (layer, 1, 0, 0)),
        ],
        out_specs=[pl.BlockSpec((rows, D_MODEL), lambda i: (i, 0)), cache_spec, cache_spec],
        out_shape=[jax.ShapeDtypeStruct((batch * seq, D_MODEL), F32), cache_shape, cache_shape],
        scratch_shapes=[pltpu.VMEM((rows, D_MODEL), BF16), pltpu.VMEM((rows, D_MODEL), F32),
                        pltpu.VMEM((rows, D_MODEL), F32), pltpu.VMEM((rows, D_MODEL), BF16)],
        compiler_params=_cparams("arbitrary"),
        name="attn_ctx",
    )(x, mods, w_qkv, w_o, ln_g, ln_b)


def _nbr_geometry(rows):
    kr = min(WIN_ROWS, rows)
    blocks = []
    for blk in range(rows // Q_ROWS):
        r0 = blk * Q_ROWS
        lo = min(max(r0 - kr // 2, 0), rows - kr)
        key_start = min(lo, rows - K_ROWS)
        bands = []
        for r in range(r0, r0 + Q_ROWS):
            rs = min(max(r - kr // 2, 0), rows - kr)
            a = rs - key_start
            band_start = min(a - a % 2, K_ROWS - BAND_ROWS)
            off = a - band_start
            assert 0 <= band_start and 0 <= off and off + kr <= BAND_ROWS
            bands.append((band_start, off, rs - r + WIN_ROWS - 1))
        blocks.append((key_start, bands))
    return kr, blocks


def _nbr_bias_rows(rpb):
    pad = GRID_W - (2 * WIN_COLS - 1)
    row = jnp.pad(rpb, ((0, 0), (0, 0), (0, pad)))
    return jnp.concatenate([row[:, :-1], row[:, 1:]], axis=-1)


def _attn_nbr_kernel(q_ref, k_ref, v_ref, kc_ref, vc_ref, rp_ref, o_ref,
                     s_ref, sc_ref, p_ref, pc_ref, inv_ref, tab_ref, *, blocks, kr):
    masks = _head_masks()
    left = masks[0]
    qn = Q_ROWS * GRID_W
    kn = K_ROWS * GRID_W
    qcol = lax.broadcasted_iota(jnp.int32, (GRID_W, LANES), 0)
    kcol = lax.broadcasted_iota(jnp.int32, (GRID_W, LANES), 1) % GRID_W
    cstart = jnp.clip(qcol - WIN_COLS // 2, 0, GRID_W - WIN_COLS)
    col_ok = (kcol >= cstart) & (kcol < cstart + WIN_COLS)
    for hh in range(2):
        for d in range(2 * WIN_ROWS - 2):
            row = jnp.broadcast_to(rp_ref[hh, d:d + 1, :], (GRID_W, LANES))
            toep = pltpu.roll(row, LANES - (WIN_COLS - 1), 1, stride=1, stride_axis=0)
            tab_ref[hh, d] = jnp.where(col_ok, toep, NEG)
    kc, vc = kc_ref[...].astype(BF16), vc_ref[...].astype(BF16)
    ctx_kv = [(jnp.where(hm, kc, 0.0), jnp.where(hm, vc, 0.0)) for hm in masks]
    for blk, (key_start, bands) in enumerate(blocks):
        q2 = (q_ref[blk * qn:(blk + 1) * qn, :] * HEAD_DIM ** -0.5).astype(BF16)
        k2 = k_ref[key_start * GRID_W:key_start * GRID_W + kn, :]
        v2 = v_ref[key_start * GRID_W:key_start * GRID_W + kn, :]
        o2 = jnp.zeros(q2.shape, F32)
        p_ref[...] = jnp.zeros_like(p_ref)
        for hh, hm in enumerate(masks):
            kh = jnp.where(hm, k2, 0.0)
            vh = jnp.where(hm, v2, 0.0)
            kch, vch = ctx_kv[hh]
            s_ref[...] = _dot_nt(q2, kh)
            sc_ref[...] = _dot_nt(q2, kch)
            for rl, (band_start, off, drow) in enumerate(bands):
                rsl = slice(rl * GRID_W, (rl + 1) * GRID_W)
                tiles = []
                for j in range(0, BAND_ROWS, 2):
                    ok0, ok1 = off <= j < off + kr, off <= j + 1 < off + kr
                    if not (ok0 or ok1):
                        continue
                    bias = tab_ref[hh, drow - off + j]
                    if not ok0:
                        bias = jnp.where(left, NEG, bias)
                    if not ok1:
                        bias = jnp.where(left, bias, NEG)
                    csl = slice((band_start + j) * GRID_W, (band_start + j + 2) * GRID_W)
                    tiles.append((csl, s_ref[rsl, csl] + bias))
                sc = sc_ref[rsl, :]
                top = functools.reduce(jnp.maximum, _lane_tiles(sc) + [sb for _, sb in tiles])
                mx = jnp.max(top, axis=-1, keepdims=True)
                ec = jnp.exp(sc - mx)
                tot = functools.reduce(jnp.add, _lane_tiles(ec))
                for csl, sb in tiles:
                    eb = jnp.exp(sb - mx)
                    tot = tot + eb
                    p_ref[rsl, csl] = eb.astype(BF16)
                inv_ref[rsl, :] = 1.0 / jnp.sum(tot, axis=-1, keepdims=True)
                pc_ref[rsl, :] = ec.astype(BF16)
            o2 = o2 + (_dot(p_ref[...], vh) + _dot(pc_ref[...], vch)) * inv_ref[...]
        o_ref[blk * qn:(blk + 1) * qn, :] = o2.astype(o_ref.dtype)


def _attn_nbr(qkv, kc, vc, rpb, batch, seq):
    rows = seq // GRID_W
    kr, blocks = _nbr_geometry(rows)
    n_pairs = D_MODEL // LANES
    n_d = 2 * WIN_ROWS - 2
    rp = _nbr_bias_rows(rpb).reshape(n_pairs, 2, n_d, LANES)
    s_ctx = kc.shape[1]
    qspec = lambda g: pl.BlockSpec((None, seq, LANES), lambda b, p: (g, b, p))
    cspec = pl.BlockSpec((None, s_ctx, LANES), lambda b, p: (b, 0, p))
    qn, kn = Q_ROWS * GRID_W, K_ROWS * GRID_W
    return pl.pallas_call(
        functools.partial(_attn_nbr_kernel, blocks=blocks, kr=kr),
        grid=(batch, n_pairs),
        in_specs=[qspec(0), qspec(1), qspec(2), cspec, cspec,
                  pl.BlockSpec((None, 2, n_d, LANES), lambda b, p: (p, 0, 0, 0))],
        out_specs=pl.BlockSpec((seq, LANES), lambda b, p: (b, p)),
        out_shape=jax.ShapeDtypeStruct((batch * seq, D_MODEL), BF16),
        scratch_shapes=[pltpu.VMEM((qn, kn), F32), pltpu.VMEM((qn, s_ctx), F32),
                        pltpu.VMEM((qn, kn), BF16), pltpu.VMEM((qn, s_ctx), BF16),
                        pltpu.VMEM((qn, 1), F32), pltpu.VMEM((2, n_d, GRID_W, LANES), F32)],
        compiler_params=_cparams("arbitrary", "arbitrary"),
        name="attn_nbr",
    )(qkv, qkv, qkv, kc, vc, rp)


def _dft_tables(seq_len, tq, tr):
    n = 2 * seq_len
    f = np.arange(seq_len, dtype=np.float64)[:, None] + 0.5
    s = np.arange(seq_len, dtype=np.float64)[None, :]
    ang = 2.0 * np.pi * f * s / n
    re = np.cos(ang).reshape(seq_len // tq, tq, seq_len)
    im = (-np.sin(ang)).reshape(seq_len // tq, tq, seq_len)
    fwd = np.concatenate([re, im], axis=1).astype(np.float32)
    bwd = fwd.reshape(2 * seq_len, seq_len).T
    return fwd, np.ascontiguousarray(bwd).reshape(seq_len // tr, tr, 2 * seq_len)


def _split_spectrum_table(seq_len, tq):
    half = seq_len // 2
    f = np.arange(half, dtype=np.float64)[:, None] + 0.5
    s = np.arange(seq_len, dtype=np.float64)[None, :]
    lo = np.pi * f * s / seq_len
    hi = np.pi * (seq_len - f) * s / seq_len
    parts = [np.cos(lo), -np.sin(lo), np.cos(hi), np.sin(hi)]
    return np.concatenate([p.reshape(half // tq, tq, seq_len) for p in parts], axis=1).astype(np.float32)


def _twiddles(seq_len):
    th = np.pi * (np.arange(seq_len // 2, dtype=np.float64) + 0.5) / seq_len
    return np.stack([np.cos(th), np.sin(th)])[:, :, None].repeat(LANES, axis=2).astype(np.float32)


def _deinterleave_matrix():
    sel = np.zeros((SPLIT_ROWS, SPLIT_ROWS), np.float32)
    i = np.arange(SPLIT_ROWS // 2)
    sel[i, 2 * i] = 1.0
    sel[SPLIT_ROWS // 2 + i, 2 * i + 1] = 1.0
    return sel


def _filter_features(seq_len):
    pos = np.arange(seq_len, dtype=np.float64)[:, None]
    t = pos / max(seq_len - 1, 1)
    bands = np.linspace(1e-4, HY_BANDS - 1, HY_BANDS)
    ang = 2.0 * np.pi * pos / seq_len * bands
    z = np.concatenate([t, np.cos(ang), -np.sin(ang)], axis=-1)
    zp = np.zeros((seq_len, HY_FILTER_W), np.float64)
    zp[:, :HY_EMB] = z
    deltas = np.abs(np.linspace(math.log(HY_TARGET) / HY_SLOW_DECAY, math.log(HY_TARGET) / HY_FAST_DECAY, D_MODEL))
    return zp.astype(np.float32), t.astype(np.float32), deltas.astype(np.float32)[None, :]


def _filter_kernel(z_ref, t_ref, dl_ref, w1_ref, b1_ref, f1_ref, w2_ref, b2_ref, f2_ref, w3_ref, b3_ref, o_ref,
                   hid_ref, decay_ref):
    hp = lax.Precision.HIGHEST

    @pl.when(pl.program_id(0) == 0)
    def _():
        hid = jnp.sin(f1_ref[...] * (jnp.dot(z_ref[...], w1_ref[...], precision=hp, preferred_element_type=F32) + b1_ref[...]))
        hid_ref[...] = jnp.sin(f2_ref[...] * (jnp.dot(hid, w2_ref[...], precision=hp, preferred_element_type=F32) + b2_ref[...]))
        decay_ref[...] = jnp.exp(-t_ref[...] * dl_ref[...])

    hid, w3 = hid_ref[...], w3_ref[...]
    hid_hi, w3_hi = hid.astype(BF16), w3.astype(BF16)
    hid_lo = (hid - hid_hi.astype(F32)).astype(BF16)
    w3_lo = (w3 - w3_hi.astype(F32)).astype(BF16)
    filt = _dot(hid_hi, w3_hi) + (_dot(hid_lo, w3_hi) + _dot(hid_hi, w3_lo)) + b3_ref[...]
    decay = decay_ref[...]
    hf = filt[:, :D_MODEL] * decay
    hb = filt[:, D_MODEL:] * decay
    row = lax.broadcasted_iota(jnp.int32, (hb.shape[0], 1), 0)
    hb = jnp.where(row == 0, 0.0, hb)
    o_ref[:, :D_MODEL] = (hf + hb).astype(o_ref.dtype)
    o_ref[:, D_MODEL:] = (hf - hb).astype(o_ref.dtype)


def _filters(seq_len, w1, b1, f1, w2, b2, f2, w3, b3):
    z, t, deltas = _filter_features(seq_len)
    w1p = jnp.zeros((HY_FILTER_W, HY_FILTER_W), F32).at[:HY_EMB].set(w1)
    full = lambda shape: pl.BlockSpec(shape, lambda o: (0,) * len(shape))
    row = lambda a: a.reshape(1, -1)
    return pl.pallas_call(
        _filter_kernel,
        grid=(2,),
        in_specs=[full((seq_len, HY_FILTER_W)), full((seq_len, 1)), full((1, D_MODEL)),
                  full((HY_FILTER_W, HY_FILTER_W)), full((1, HY_FILTER_W)), full((1, HY_FILTER_W)),
                  full((HY_FILTER_W, HY_FILTER_W)), full((1, HY_FILTER_W)), full((1, HY_FILTER_W)),
                  pl.BlockSpec((HY_FILTER_W, 2 * D_MODEL), lambda o: (0, o)),
                  pl.BlockSpec((1, 2 * D_MODEL), lambda o: (0, o))],
        out_specs=pl.BlockSpec((None, seq_len, 2 * D_MODEL), lambda o: (o, 0, 0)),
        out_shape=jax.ShapeDtypeStruct((2, seq_len, 2 * D_MODEL), BF16),
        scratch_shapes=[pltpu.VMEM((seq_len, HY_FILTER_W), F32), pltpu.VMEM((seq_len, D_MODEL), F32)],
        compiler_params=_cparams("arbitrary"),
        name="hyena_filters",
    )(jnp.asarray(z),jnp.asarray(t), jnp.asarray(deltas), w1p, row(b1), row(f1), w2, row(b2), row(f2), w3, row(b3))


def _spectrum_kernel(g_ref, hs_ref, hd_ref, o_ref, *, tq, scale):
    for part in range(g_ref.shape[0] // tq):
        rows = slice(part * tq, (part + 1) * tq)
        h_ref = hs_ref if part % 2 == 0 else hd_ref
        o_ref[rows, :] = scale * _dot(g_ref[rows, :], h_ref[...])


def _spectrum(table, filt, seq_len, tq, tc):
    nt, rows, _ = table.shape
    ncb = D_MODEL // tc
    return pl.pallas_call(
        functools.partial(_spectrum_kernel, tq=tq, scale=1.0 / seq_len),
        grid=(nt, 2, ncb),
        in_specs=[pl.BlockSpec((None, rows, seq_len), lambda f, o, c: (f, 0, 0)),
                  pl.BlockSpec((None, seq_len, tc), lambda f, o, c: (o, 0, c)),
                  pl.BlockSpec((None, seq_len, tc), lambda f, o, c: (o, 0, c + ncb))],
        out_specs=pl.BlockSpec((None, None, rows, tc), lambda f, o, c: (o, f, 0, c)),
        out_shape=jax.ShapeDtypeStruct((2, nt, rows, D_MODEL), F32),
        compiler_params=_cparams("arbitrary", "arbitrary", "arbitrary"),
        name="hyena_spectrum",
    )(table, filt, filt)


def _hyena_kernel(v_ref, gate_ref, g_ref, gt_ref, k_ref, tw_ref, sel_ref, selt_ref, b_ref, o_ref,
                  ze_ref, zo_ref, pu_ref, pv_ref, *, tq, nf, tr):
    o = pl.program_id(2)
    st = pl.program_id(3)
    half = ze_ref.shape[0]
    tc = ze_ref.shape[1]
    hs = SPLIT_ROWS // 2

    @pl.when((o == 0) & (st == 0))
    def _():
        for j in range(half // hs):
            sp = _dot(sel_ref[...], v_ref[j * SPLIT_ROWS:(j + 1) * SPLIT_ROWS, :])
            ze_ref[j * hs:(j + 1) * hs, :] = sp[:hs].astype(BF16)
            zo_ref[j * hs:(j + 1) * hs, :] = sp[hs:].astype(BF16)

    @pl.when(st < nf)
    def _():
        base = pl.multiple_of(st * 2 * tq, 2 * tq)
        c, s = tw_ref[0], tw_ref[1]
        group = 2 * LANES
        for lt in range(tc // LANES):
            cols = slice(lt * LANES, (lt + 1) * LANES)
            if lt % 2 == 0:
                gcols = slice(lt * LANES, lt * LANES + group)
                fa = _dot(g_ref[...], ze_ref[:, gcols])
                fb = _dot(g_ref[...], zo_ref[:, gcols])
            loc = slice((lt % 2) * LANES, (lt % 2 + 1) * LANES)
            ar, ai, br, bi = fa[:tq, loc], fa[tq:, loc], fb[:tq, loc], fb[tq:, loc]
            tr_, ti = c * br + s * bi, c * bi - s * br
            s1r, s1i, s2r, s2i = ar + tr_, ai + ti, ar - tr_, ai - ti
            k1r, k1i = k_ref[0:tq, cols], k_ref[tq:2 * tq, cols]
            k2r, k2i = k_ref[2 * tq:3 * tq, cols], k_ref[3 * tq:, cols]
            p1r, p1i = s1r * k1r - s1i * k1i, s1r * k1i + s1i * k1r
            p2r, p2i = s2r * k2r - s2i * k2i, s2r * k2i + s2i * k2r
            wr, wi = p1r - p2r, p1i - p2i
            pu_ref[pl.ds(base, tq), cols] = (p1r + p2r).astype(BF16)
            pu_ref[pl.ds(base + tq, tq), cols] = (p1i + p2i).astype(BF16)
            pv_ref[pl.ds(base, tq), cols] = (c * wr - s * wi).astype(BF16)
            pv_ref[pl.ds(base + tq, tq), cols] = (c * wi + s * wr).astype(BF16)

    @pl.when(st >= nf)
    def _():
        r0 = pl.multiple_of((st - nf) * tr, tr)
        rows = pl.ds(r0, tr)
        ye = _dot(gt_ref[...], pu_ref[...]) + ze_ref[rows, :].astype(F32) * b_ref[...]
        yo = _dot(gt_ref[...], pv_ref[...]) + zo_ref[rows, :].astype(F32) * b_ref[...]
        block = lambda j: pl.ds(pl.multiple_of(2 * r0 + j * SPLIT_ROWS, SPLIT_ROWS), SPLIT_ROWS)
        gates = [_dot(sel_ref[...], gate_ref[block(j), :]) for j in range(tr // hs)]
        ge = ye * jnp.concatenate([g[:hs] for g in gates], axis=0)
        go = yo * jnp.concatenate([g[hs:] for g in gates], axis=0)

        @pl.when(o == 0)
        def _():
            ze_ref[rows, :] = ge.astype(BF16)
            zo_ref[rows, :] = go.astype(BF16)

        @pl.when(o == 1)
        def _():
            for j in range(tr // hs):
                blk = jnp.concatenate([ge[j * hs:(j + 1) * hs], go[j * hs:(j + 1) * hs]], axis=0).astype(BF16)
                o_ref[block(j), :] = _dot(selt_ref[...], blk).astype(o_ref.dtype)


def _hyena_core(u, fwd_h, bwd_h, spec, tw, sel, bias, batch, seq_len, tq, tc):
    half = seq_len // 2
    nf = half // tq
    nr, tr, _ = bwd_h.shape
    ncb = D_MODEL // tc
    fidx = lambda st: jnp.minimum(st, nf - 1)
    whole = lambda shape: pl.BlockSpec(shape, lambda b, c, o, st: (0,) * len(shape))
    return pl.pallas_call(
        functools.partial(_hyena_kernel, tq=tq, nf=nf, tr=tr),
        grid=(batch, ncb, 2, nf + nr),
        in_specs=[pl.BlockSpec((None, seq_len, tc), lambda b, c, o, st: (0, b, c)),
                  pl.BlockSpec((None, seq_len, tc), lambda b, c, o, st: (1 + o, b, c)),
                  pl.BlockSpec((None, 2 * tq, half), lambda b, c, o, st: (fidx(st), 0, 0)),
                  pl.BlockSpec((None, tr, seq_len), lambda b, c, o, st: (jnp.maximum(st - nf, 0), 0, 0)),
                  pl.BlockSpec((None, None, 4 * tq, tc), lambda b, c, o, st: (o, fidx(st), 0, c)),
                  pl.BlockSpec((2, tq, LANES), lambda b, c, o, st: (0, fidx(st), 0)),
                  whole((SPLIT_ROWS, SPLIT_ROWS)), whole((SPLIT_ROWS, SPLIT_ROWS)),
                  pl.BlockSpec((None, 1, tc), lambda b, c, o, st: (o, 0, c))],
        out_specs=pl.BlockSpec((seq_len, tc), lambda b, c, o, st: (b, c)),
        out_shape=jax.ShapeDtypeStruct((batch * seq_len, D_MODEL), BF16),
        scratch_shapes=[pltpu.VMEM((half, tc), BF16), pltpu.VMEM((half, tc), BF16),
                        pltpu.VMEM((seq_len, tc), BF16), pltpu.VMEM((seq_len, tc), BF16)],
        compiler_params=_cparams("arbitrary", "arbitrary", "arbitrary", "arbitrary"),
        name="hyena_conv",
    )(u, u, fwd_h, bwd_h, spec, tw, sel, sel.T, bias.reshape(2, 1, D_MODEL))


def _hyena_layer_kernel(x_ref, mod_ref, win_ref, cw_ref, cb_ref, g_ref, gt_ref, k_ref, hb_ref, wout_ref,
                        lg_ref, lb_ref, o_ref, u_scr, *, seq_len, nseq, tn):
    x = x_ref[...]
    h = (x * (1.0 + mod_ref[4:5, :]) + mod_ref[3:4, :]).astype(BF16)
    rows = nseq * seq_len
    pos = lax.broadcasted_iota(jnp.int32, (rows, 1), 0) % seq_len
    first, last = pos == 0, pos == seq_len - 1
    for g in range(3):
        for cc in range(D_MODEL // tn):
            cols = slice(g * D_MODEL + cc * tn, g * D_MODEL + (cc + 1) * tn)
            u = _dot(h, win_ref[:, cols])
            prev = jnp.where(first, 0.0, pltpu.roll(u, 1, 0))
            nxt = jnp.where(last, 0.0, pltpu.roll(u, rows - 1, 0))
            u = prev * cw_ref[0:1, cols] + u * cw_ref[1:2, cols] + nxt * cw_ref[2:3, cols] + cb_ref[:, cols]
            u_scr[g, :, cc * tn:(cc + 1) * tn] = u.astype(BF16)
    for o in range(2):
        kr, ki = k_ref[o, :seq_len, :], k_ref[o, seq_len:, :]
        for s in range(nseq):
            r = slice(s * seq_len, (s + 1) * seq_len)
            z = u_scr[0, r, :]
            zf = _dot(g_ref[...], z)
            zr, zi = zf[:seq_len], zf[seq_len:]
            prod = jnp.concatenate([zr * kr - zi * ki, zr * ki + zi * kr], axis=0).astype(BF16)
            y = _dot(gt_ref[...], prod) + z.astype(F32) * hb_ref[o]
            u_scr[0, r, :] = (y * u_scr[1 + o, r, :].astype(F32)).astype(BF16)
    y = ALPHA * x + mod_ref[5:6, :] * _dot(u_scr[0], wout_ref[...])
    o_ref[...] = _layer_norm(y, lg_ref[...], lb_ref[...])


def _hyena_layer(x, mods, crow, w_in, w_out, widx, conv_w, conv_b, fwd, bwd, spec, bias, ln_g, ln_b, layer,
                 seq_len, nseq, tn=512):
    m = x.shape[0]
    rows = nseq * seq_len
    assert fwd.shape[0] == 1
    return pl.pallas_call(
        functools.partial(_hyena_layer_kernel, seq_len=seq_len, nseq=nseq, tn=tn),
        grid=(m // rows,),
        in_specs=[
            pl.BlockSpec((rows, D_MODEL), lambda i: (i, 0)),
            pl.BlockSpec((None, None, N_MOD, D_MODEL), lambda i: (layer, crow(i * rows), 0, 0)),
            _resident((None, D_MODEL, 3 * D_MODEL), lambda i: (widx, 0, 0)),
            pl.BlockSpec((None, 3, 3 * D_MODEL), lambda i: (widx, 0, 0)),
            pl.BlockSpec((None, 1, 3 * D_MODEL), lambda i: (widx, 0, 0)),
            pl.BlockSpec((None, 2 * seq_len, seq_len), lambda i: (0, 0, 0)),
            pl.BlockSpec((None, seq_len, 2 * seq_len), lambda i: (0, 0, 0)),
            _resident((2, None, 2 * seq_len, D_MODEL), lambda i: (0, 0, 0, 0)),
            pl.BlockSpec((2, 1, D_MODEL), lambda i: (0, 0, 0)),
            _resident((None, D_MODEL, D_MODEL), lambda i: (widx, 0, 0)),
            pl.BlockSpec((None, None, 1, D_MODEL), lambda i: (layer, 1, 0, 0)),
            pl.BlockSpec((None, None, 1, D_MODEL), lambda i: (layer, 1, 0, 0)),
        ],
        out_specs=pl.BlockSpec((rows, D_MODEL), lambda i: (i, 0)),
        out_shape=jax.ShapeDtypeStruct((m, D_MODEL), F32),
        scratch_shapes=[pltpu.VMEM((3, rows, D_MODEL), BF16)],
        compiler_params=_cparams("arbitrary"),
        name="hyena_layer",
    )(x, mods, w_in, conv_w, conv_b.reshape(conv_b.shape[0], 1, conv_b.shape[1]), fwd, bwd, spec,
      bias.reshape(2, 1, D_MODEL), w_out, ln_g, ln_b)


def _hyena_tiles(seq_len, batch):
    tq = tr = min(seq_len, 512)
    tc = D_MODEL if seq_len <= 256 else 512
    nseq = math.gcd(batch, max(1, 1024 // seq_len))
    return tq, tr, tc, nseq


def kernel(x_prompt, x_sample, cache_k, cache_v, c, c_ctx, mod_w, mod_b, ln_g, ln_b, ffn_w_in, ffn_w_out,
           attn_w_qkv, attn_w_o, attn_rpb, hy_w_in, hy_conv_w, hy_conv_b, hy_f_w1, hy_f_b1, hy_f_freq1,
           hy_f_w2, hy_f_b2, hy_f_freq2, hy_f_w3, hy_f_b3, hy_bias, hy_w_out):
    batch, seq, _ = x_prompt.shape
    dec_batch, dec_seq, _ = x_sample.shape
    assert dec_batch + 1 <= COND_ROWS

    conds = jnp.zeros((COND_ROWS, D_MODEL), F32).at[0].set(c_ctx).at[1:1 + dec_batch].set(c)
    mods = _mods(conds, mod_w, mod_b).reshape(DEPTH, COND_ROWS, N_MOD, D_MODEL)
    ln_g4 = ln_g.reshape(DEPTH, 3, 1, D_MODEL)
    ln_b4 = ln_b.reshape(DEPTH, 3, 1, D_MODEL)

    streams = {
        "ctx": dict(x=x_prompt.reshape(batch * seq, D_MODEL), batch=batch, seq=seq, crow=lambda t: 0),
        "lat": dict(x=x_sample.reshape(dec_batch * dec_seq, D_MODEL), batch=dec_batch, seq=dec_seq,
                    crow=lambda t: 1 + t // dec_seq),
    }
    as_bf16 = lambda table: jnp.asarray(table).astype(BF16)
    hyena_consts = {}
    for name, st in streams.items():
        sl = st["seq"]
        tq, tr, tc, nseq = _hyena_tiles(sl, st["batch"])
        if sl == tq:
            fwd, bwd = _dft_tables(sl, tq, tr)
            hyena_consts[name] = dict(tq=tq, tc=tc, nseq=nseq, fwd=as_bf16(fwd), bwd=as_bf16(bwd), ktab=as_bf16(fwd))
        else:
            fwd, bwd = _dft_tables(sl // 2, tq, tr)
            hyena_consts[name] = dict(tq=tq, tc=tc, nseq=nseq, fwd=as_bf16(fwd), bwd=as_bf16(bwd),
                                      ktab=as_bf16(_split_spectrum_table(sl, tq)),
                                      tw=jnp.asarray(_twiddles(sl)), sel=as_bf16(_deinterleave_matrix()))

    attn_w_qkv, attn_w_o, hy_w_in, hy_w_out = (w.astype(BF16) for w in (attn_w_qkv, attn_w_o, hy_w_in, hy_w_out))
    ffn_order = [(layer, which) for layer in range(DEPTH) for which in (0, 1)]
    ffn_bf16 = {ffn_order[0]: (ffn_w_in[0, 0].astype(BF16), ffn_w_out[0, 0].astype(BF16))}
    names = list(streams)
    crows = [streams[name]["crow"] for name in names]

    def ffn(xs, layer, which, mixer_outs, w_mix=None):
        key = (layer, which)
        later = ffn_order[ffn_order.index(key) + 1:]
        cast_next = None
        if later and _ffn_can_cast(xs[names[0]].shape[0]):
            cast_next = (ffn_w_in, ffn_w_out, *later[0])
        elif later:
            ffn_bf16[later[0]] = (ffn_w_in[later[0]].astype(BF16), ffn_w_out[later[0]].astype(BF16))
        ys, cast = _ffn([xs[name] for name in names], mods, crows, *ffn_bf16[key], ln_g4, ln_b4, layer, which,
                        [mixer_outs.get(name) for name in names], w_mix=w_mix, cast_next=cast_next)
        if cast_next is not None:
            ffn_bf16[later[0]] = cast
        return dict(zip(names, ys))

    new_k, new_v = [], []
    xs = {name: st["x"] for name, st in streams.items()}
    for layer in range(DEPTH):
        j = layer // 2
        xs = ffn(xs, layer, 0, {})
        mixer_outs = {}
        for name in reversed(names):
            st = streams[name]
            x, nb, sl, crow = xs[name], st["batch"], st["seq"], st["crow"]
            if layer % 2 == 0:
                w_mix = (attn_w_o, j)
                if name == "ctx":
                    xs[name], k_new, v_new = _attn_ctx(x, mods, crow, attn_w_qkv, attn_w_o, j, ln_g4, ln_b4, layer,
                                                       nb, sl)
                    new_k.append(k_new)
                    new_v.append(v_new)
                else:
                    qkv = _proj(x, mods, crow, attn_w_qkv, layer, j, tm=sl, tn=1024, out_dtype=BF16)
                    kc = cache_k[:, j].reshape(nb, -1, D_MODEL)
                    vc = cache_v[:, j].reshape(nb, -1, D_MODEL)
                    mixer_outs[name] = _attn_nbr(qkv, kc, vc, attn_rpb[j], nb, sl)
            else:
                w_mix = (hy_w_out, j)
                hc = hyena_consts[name]
                tq, tc = hc["tq"], hc["tc"]
                filt = _filters(sl, hy_f_w1[j], hy_f_b1[j], hy_f_freq1[j], hy_f_w2[j], hy_f_b2[j], hy_f_freq2[j],
                                hy_f_w3[j], hy_f_b3[j])
                spec = _spectrum(hc["ktab"], filt, sl, tq, tc)
                if sl == tq:
                    xs[name] = _hyena_layer(x, mods, crow, hy_w_in, hy_w_out, j, hy_conv_w, hy_conv_b, hc["fwd"],
                                            hc["bwd"], spec, hy_bias[j], ln_g4, ln_b4, layer, sl, hc["nseq"])
                else:
                    u = _proj(x, mods, crow, hy_w_in, layer, j, tm=max(sl, 1024), tn=512, out_dtype=BF16,
                              conv=(hy_conv_w, hy_conv_b), seq_len=sl)
                    mixer_outs[name] = _hyena_core(u, hc["fwd"], hc["bwd"], spec, hc["tw"], hc["sel"], hy_bias[j],
                                                   nb, sl, tq, tc)
        xs = ffn(xs, layer, 1, mixer_outs, w_mix=w_mix)

    y_prompt = xs["ctx"].reshape(batch, seq, D_MODEL)
    y_sample = xs["lat"].reshape(dec_batch, dec_seq, D_MODEL)
    return (y_prompt, y_sample, jnp.concatenate(new_k, axis=1), jnp.concatenate(new_v, axis=1))
```
